```python
import jax
import jax.numpy as jnp
from jax import lax
import numpy as np

D_MODEL = 1024
BATCH = 8
SEQ = 4096
DEPTH = 2

CTX_LEN = 256
GRID_W = 64
ML_HEADS = 4
ML_DQK = 64
ML_DV = 128
ML_CHUNK = 64
MLA_HEADS = 8
MLA_Q_RANK = 384
MLA_KV_RANK = 256
MLA_NOPE = 64
MLA_ROPE = 32
MLA_V = 64
NA_HEADS = 8
NA_DH = 64
NA_WIN_R = 8
NA_WIN_C = 16
D_FF = 4 * D_MODEL
N_BRANCH = 3
ROPE_THETA = 10000.0
Q_BLOCK = 128
EPS = 1e-6
NEG_INF = -1e30
IN_SIZES = (
    ML_HEADS * ML_DQK, ML_HEADS * ML_DQK, ML_HEADS * ML_DV, ML_HEADS * ML_DV, 4 * ML_HEADS,
    MLA_Q_RANK, MLA_KV_RANK, MLA_ROPE,
    NA_HEADS * NA_DH, NA_HEADS * NA_DH, NA_HEADS * NA_DH,
    N_BRANCH * D_MODEL,
)
D_IN = sum(IN_SIZES)

kernel_name = 'hybrid_mlstm_mla_natten_dit'


def rms_norm(x, g):
    xf = x.astype(jnp.float32)
    y = xf * lax.rsqrt(jnp.mean(xf * xf, axis=-1, keepdims=True) + EPS)
    return (y * g.astype(jnp.float32)).astype(x.dtype)


def split_heads(t, n_heads):
    b, s, _ = t.shape
    return t.reshape(b, s, n_heads, -1)


def swap_ht(t):
    return t.transpose(0, 2, 1, 3)


def split_in(p):
    offsets = np.cumsum(IN_SIZES)[:-1].tolist()
    return jnp.split(p, offsets, axis=-1)


def attend(q, k, v, scale):
    s = jnp.einsum('bhqd,bhkd->bhqk', q, k).astype(jnp.float32) * scale
    p = jax.nn.softmax(s, axis=-1).astype(v.dtype)
    return jnp.einsum('bhqk,bhkd->bhqd', p, v)


def attend_blocked(q, k, v, scale):
    b, h, t, d = q.shape
    nb = t // Q_BLOCK
    qb = q.reshape(b, h, nb, Q_BLOCK, d).transpose(2, 0, 1, 3, 4)
    o = lax.map(lambda qi: attend(qi, k, v, scale), qb)
    return o.transpose(1, 2, 0, 3, 4).reshape(b, h, t, v.shape[-1])


def axial_rope_tables(t, dtype):
    nf = MLA_ROPE // 4
    inv = jnp.power(ROPE_THETA, -jnp.arange(nf, dtype=jnp.float32) / nf)
    pos = jnp.arange(t)
    ang_r = (pos // GRID_W).astype(jnp.float32)[:, None] * inv
    ang_c = (pos % GRID_W).astype(jnp.float32)[:, None] * inv
    return tuple(a[None, :, None, :].astype(dtype) for a in (jnp.cos(ang_r), jnp.sin(ang_r), jnp.cos(ang_c), jnp.sin(ang_c)))


def rotate(x, cos, sin):
    x1, x2 = jnp.split(x, 2, axis=-1)
    return jnp.concatenate([x1 * cos - x2 * sin, x2 * cos + x1 * sin], axis=-1)


def axial_rope(x, tabs):
    cos_r, sin_r, cos_c, sin_c = tabs
    half = x.shape[-1] // 2
    return jnp.concatenate([rotate(x[..., :half], cos_r, sin_r), rotate(x[..., half:], cos_c, sin_c)], axis=-1)


def mlstm_chunked(q, k, v, ig, lf, state, want_h):
    b, h, t, dk = q.shape
    nc = t // ML_CHUNK

    def chunks(a):
        return jnp.moveaxis(a.reshape(b, h, nc, ML_CHUNK, *a.shape[3:]), 2, 0)

    xs = tuple(chunks(a) for a in (q * dk ** -0.5, k, v, ig, lf))
    lower = jnp.tril(jnp.ones((ML_CHUNK, ML_CHUNK), dtype=bool))

    def step(carry, inp):
        C, n, m = carry
        qc, kc, vc, ic, fc = inp
        bcum = jnp.cumsum(fc, axis=-1)
        btot = bcum[..., -1]
        w_end = btot[..., None] - bcum + ic
        m_new = jnp.maximum(btot + m, jnp.max(w_end, axis=-1))
        a_state = jnp.exp(btot + m - m_new)
        wk = jnp.exp(w_end - m_new[..., None])
        C_new = a_state[..., None, None] * C + jnp.einsum('bhl,bhld,bhle->bhde', wk, kc, vc)
        n_new = a_state[..., None] * n + jnp.einsum('bhl,bhld->bhd', wk, kc)
        carry_new = (C_new, n_new, m_new)
        if not want_h:
            return carry_new, None
        dmat = jnp.where(lower, bcum[..., :, None] - bcum[..., None, :] + ic[..., None, :], -jnp.inf)
        inter = bcum + m[..., None]
        m_t = jnp.maximum(inter, jnp.max(dmat, axis=-1))
        a_inter = jnp.exp(inter - m_t)
        s = jnp.einsum('bhtd,bhsd->bhts', qc, kc) * jnp.exp(dmat - m_t[..., None])
        num = a_inter[..., None] * jnp.einsum('bhtd,bhde->bhte', qc, C) + jnp.einsum('bhts,bhse->bhte', s, vc)
        den = a_inter * jnp.einsum('bhtd,bhd->bht', qc, n) + jnp.sum(s, axis=-1)
        h_out = num / jnp.maximum(jnp.abs(den), jnp.exp(-m_t))[..., None]
        return carry_new, h_out

    state, hs = lax.scan(step, state, xs)
    if not want_h:
        return None, state
    return jnp.moveaxis(hs, 0, 2).reshape(b, h, t, -1), state


def mlstm_branch(lat, ctx, i_bias, f_bias, g_out, w_o, want_ctx):
    f32 = jnp.float32

    def prep(q, k, v, gates):
        b, t, _ = q.shape
        qh, kh, vh = (swap_ht(split_heads(a, ML_HEADS)).astype(f32) for a in (q, k, v))
        g = gates.astype(f32).reshape(b, t, 4, ML_HEADS).transpose(2, 0, 3, 1)
        dirs = [(g[2 * d] + i_bias[d].astype(f32)[None, :, None],
                 jax.nn.log_sigmoid(g[2 * d + 1] + f_bias[d].astype(f32)[None, :, None])) for d in range(2)]
        return qh, kh, vh, dirs

    ql, kl, vl, dl = prep(lat[0], lat[1], lat[2], lat[4])
    qc, kc, vc, dc = prep(ctx[0], ctx[1], ctx[2], ctx[4])
    b = qc.shape[0]
    zero = (jnp.zeros((b, ML_HEADS, ML_DQK, ML_DV), f32), jnp.zeros((b, ML_HEADS, ML_DQK), f32), jnp.zeros((b, ML_HEADS), f32))
    h_lat, h_ctx = [], []
    for d in range(2):
        flip = (lambda a: jnp.flip(a, axis=2)) if d == 1 else (lambda a: a)
        hc_d, st = mlstm_chunked(flip(qc), flip(kc), flip(vc), flip(dc[d][0]), flip(dc[d][1]), zero, want_ctx)
        hl_d, _ = mlstm_chunked(flip(ql), flip(kl), flip(vl), flip(dl[d][0]), flip(dl[d][1]), st, True)
        h_lat.append(flip(hl_d))
        if want_ctx:
            h_ctx.append(flip(hc_d))

    def out(hsum, o):
        bb, _, t, _ = hsum.shape
        hn = rms_norm(swap_ht(hsum), g_out.reshape(ML_HEADS, ML_DV)).reshape(bb, t, -1).astype(o.dtype)
        return (hn * jax.nn.sigmoid(o)) @ w_o

    y = out(h_lat[0] + h_lat[1], lat[3])
    yc = out(h_ctx[0] + h_ctx[1], ctx[3]) if want_ctx else None
    return y, yc


def mla_branch(lat, ctx, g_cq, w_uq, g_ckv, w_ukv, g_q, g_k, w_o, rope, want_ctx):
    scale = (MLA_NOPE + MLA_ROPE) ** -0.5

    def qkv(dq, dkv, kr, rope_tabs, need_q):
        b, t, _ = dkv.shape
        kv = (rms_norm(dkv, g_ckv) @ w_ukv).reshape(b, t, MLA_HEADS, MLA_NOPE + MLA_V)
        k_r = jnp.broadcast_to(kr[:, :, None, :], (b, t, MLA_HEADS, MLA_ROPE))
        k = rms_norm(jnp.concatenate([kv[..., :MLA_NOPE], k_r], axis=-1), g_k)
        q = None
        if need_q:
            q = rms_norm((rms_norm(dq, g_cq) @ w_uq).reshape(b, t, MLA_HEADS, MLA_NOPE + MLA_ROPE), g_q)
        if rope_tabs is not None:
            k = jnp.concatenate([k[..., :MLA_NOPE], axial_rope(k[..., MLA_NOPE:], rope_tabs)], axis=-1)
            q = jnp.concatenate([q[..., :MLA_NOPE], axial_rope(q[..., MLA_NOPE:], rope_tabs)], axis=-1)
        return (swap_ht(q) if need_q else None), swap_ht(k), swap_ht(kv[..., MLA_NOPE:])

    ql, kl, vl = qkv(*lat, rope, True)
    qc, kc, vc = qkv(*ctx, None, want_ctx)
    k_all = jnp.concatenate([kc, kl], axis=2)
    v_all = jnp.concatenate([vc, vl], axis=2)
    o = attend_blocked(ql, k_all, v_all, scale)
    b, _, t, _ = o.shape
    y = swap_ht(o).reshape(b, t, -1) @ w_o
    yc = None
    if want_ctx:
        oc = attend(qc, kc, vc, scale)
        yc = swap_ht(oc).reshape(b, oc.shape[2], -1) @ w_o
    return y, yc


def natten_branch(lat, ctx, g_q, g_k, rpb, w_o, want_ctx):
    scale = NA_DH ** -0.5
    f32 = jnp.float32

    def heads(q, k, v):
        q, k, v = (split_heads(a, NA_HEADS) for a in (q, k, v))
        return swap_ht(rms_norm(q, g_q)), swap_ht(rms_norm(k, g_k)), swap_ht(v)

    ql, kl, vl = heads(*lat)
    qc, kc, vc = heads(*ctx)
    b, h, t, dh = ql.shape
    rows = t // GRID_W
    wr = min(NA_WIN_R, rows)
    n_band = wr * GRID_W
    kg = kl.reshape(b, h, rows, GRID_W, dh)
    vg = vl.reshape(b, h, rows, GRID_W, dh)
    col = jnp.arange(GRID_W)
    cs = jnp.clip(col - NA_WIN_C // 2, 0, GRID_W - NA_WIN_C)
    col_ok = (col[None, :] >= cs[:, None]) & (col[None, :] < cs[:, None] + NA_WIN_C)
    dc_idx = jnp.clip(col[None, :] - col[:, None] + NA_WIN_C - 1, 0, 2 * NA_WIN_C - 2)

    def one_row(args):
        r, q_row = args
        rs = jnp.clip(r - wr // 2, 0, rows - wr)
        k_band = lax.dynamic_slice_in_dim(kg, rs, wr, axis=2)
        v_band = lax.dynamic_slice_in_dim(vg, rs, wr, axis=2)
        dr_idx = rs + jnp.arange(wr) - r + NA_WIN_R - 1
        bias = rpb[:, dr_idx[None, :, None], dc_idx[:, None, :]].astype(f32)
        s_win = jnp.einsum('bhqd,bhrkd->bhqrk', q_row, k_band).astype(f32) * scale + bias
        s_win = jnp.where(col_ok[:, None, :], s_win, NEG_INF)
        s_ctx = jnp.einsum('bhqd,bhcd->bhqc', q_row, kc).astype(f32) * scale
        p = jax.nn.softmax(jnp.concatenate([s_win.reshape(b, h, GRID_W, n_band), s_ctx], axis=-1), axis=-1).astype(vl.dtype)
        return (jnp.einsum('bhqn,bhnd->bhqd', p[..., :n_band], v_band.reshape(b, h, n_band, dh))
                + jnp.einsum('bhqc,bhcd->bhqd', p[..., n_band:], vc))

    o = lax.map(one_row, (jnp.arange(rows), jnp.moveaxis(ql.reshape(b, h, rows, GRID_W, dh), 2, 0)))
    y = o.transpose(1, 0, 3, 2, 4).reshape(b, t, h * dh) @ w_o
    yc = None
    if want_ctx:
        oc = attend(qc, kc, vc, scale)
        yc = swap_ht(oc).reshape(b, oc.shape[2], -1) @ w_o
    return y, yc


def merge(y_a, y_b, y_c, gate_pre, w_out):
    g = jax.nn.sigmoid(gate_pre.astype(jnp.float32)).astype(y_a.dtype)
    g_a, g_b, g_c = jnp.split(g, N_BRANCH, axis=-1)
    return (g_a * y_a + g_b * y_b + g_c * y_c) @ w_out


def token_mixer(h, hc, rope, want_ctx, w_in, ml_i_bias, ml_f_bias, ml_g_out, ml_w_o,
                mla_g_cq, mla_w_uq, mla_g_ckv, mla_w_ukv, mla_g_q, mla_g_k, mla_w_o,
                na_g_q, na_g_k, na_rpb, na_w_o, w_out):
    p = split_in(h @ w_in)
    pc = split_in(hc @ w_in)
    y_a, yc_a = mlstm_branch(p[0:5], pc[0:5], ml_i_bias, ml_f_bias, ml_g_out, ml_w_o, want_ctx)
    y_b, yc_b = mla_branch(p[5:8], pc[5:8], mla_g_cq, mla_w_uq, mla_g_ckv, mla_w_ukv, mla_g_q, mla_g_k, mla_w_o, rope, want_ctx)
    y_c, yc_c = natten_branch(p[8:11], pc[8:11], na_g_q, na_g_k, na_rpb, na_w_o, want_ctx)
    y = merge(y_a, y_b, y_c, p[11], w_out)
    yc = merge(yc_a, yc_b, yc_c, pc[11], w_out) if want_ctx else None
    return y, yc


def sqrelu_mlp(h, w1, w2):
    return jnp.square(jax.nn.relu(h @ w1)) @ w2


def setup_inputs(seed: int = 0) -> dict:
    key = jax.random.key(seed)
    ks = iter(jax.random.split(key, 32))

    def nrm(shape, s):
        return jax.random.normal(next(ks), shape, jnp.float32) * s

    L, D = DEPTH, D_MODEL
    return {
        'x': nrm((BATCH, SEQ, D), 1.0),
        'c': nrm((BATCH, D), 1.0),
        'ctx': nrm((BATCH, CTX_LEN, D), 1.0),
        'c_ctx': nrm((D,), 1.0),
        'w_mod': nrm((L, D, 6 * D), 0.3 * D ** -0.5),
        'b_mod': nrm((L, 6 * D), 0.02),
        'g_norm1': 1.0 + nrm((L, D), 0.05),
        'g_norm2': 1.0 + nrm((L, D), 0.05),
        'w_in': nrm((L, D, D_IN), D ** -0.5),
        'ml_i_bias': nrm((L, 2, ML_HEADS), 0.1),
        'ml_f_bias': 3.0 + nrm((L, 2, ML_HEADS), 0.5),
        'ml_g_out': 1.0 + nrm((L, ML_HEADS * ML_DV), 0.05),
        'ml_w_o': nrm((L, ML_HEADS * ML_DV, D), (ML_HEADS * ML_DV) ** -0.5),
        'mla_g_cq': 1.0 + nrm((L, MLA_Q_RANK), 0.05),
        'mla_w_uq': nrm((L, MLA_Q_RANK, MLA_HEADS * (MLA_NOPE + MLA_ROPE)), MLA_Q_RANK ** -0.5),
        'mla_g_ckv': 1.0 + nrm((L, MLA_KV_RANK), 0.05),
        'mla_w_ukv': nrm((L, MLA_KV_RANK, MLA_HEADS * (MLA_NOPE + MLA_V)), MLA_KV_RANK ** -0.5),
        'mla_g_q': 1.0 + nrm((L, MLA_NOPE + MLA_ROPE), 0.05),
        'mla_g_k': 1.0 + nrm((L, MLA_NOPE + MLA_ROPE), 0.05),
        'mla_w_o': nrm((L, MLA_HEADS * MLA_V, D), (MLA_HEADS * MLA_V) ** -0.5),
        'na_g_q': 1.0 + nrm((L, NA_DH), 0.05),
        'na_g_k': 1.0 + nrm((L, NA_DH), 0.05),
        'na_rpb': nrm((L, NA_HEADS, 2 * NA_WIN_R - 1, 2 * NA_WIN_C - 1), 0.5),
        'na_w_o': nrm((L, NA_HEADS * NA_DH, D), (NA_HEADS * NA_DH) ** -0.5),
        'w_out': nrm((L, D, D), D ** -0.5),
        'w_ff1': nrm((L, D, D_FF), D ** -0.5),
        'w_ff2': nrm((L, D_FF, D), D_FF ** -0.5),
    }


def reference(x, c, ctx, c_ctx, w_mod, b_mod, g_norm1, g_norm2, w_in, ml_i_bias, ml_f_bias, ml_g_out, ml_w_o,
              mla_g_cq, mla_w_uq, mla_g_ckv, mla_w_ukv, mla_g_q, mla_g_k, mla_w_o,
              na_g_q, na_g_k, na_rpb, na_w_o, w_out, w_ff1, w_ff2):
    rope = axial_rope_tables(x.shape[1], x.dtype)
    xc = ctx
    for l in range(DEPTH):
        want_ctx = l < DEPTH - 1
        mod = jax.nn.silu(c) @ w_mod[l] + b_mod[l]
        sh1, sc1, g1, sh2, sc2, g2 = jnp.split(mod[:, None, :], 6, axis=-1)
        modc = jax.nn.silu(c_ctx) @ w_mod[l] + b_mod[l]
        shc1, scc1, gc1, shc2, scc2, gc2 = jnp.split(modc, 6, axis=-1)
        h = rms_norm(x, g_norm1[l]) * (1 + sc1) + sh1
        hc = rms_norm(xc, g_norm1[l]) * (1 + scc1) + shc1
        y, yc = token_mixer(h, hc, rope, want_ctx, w_in[l], ml_i_bias[l], ml_f_bias[l], ml_g_out[l], ml_w_o[l],
                            mla_g_cq[l], mla_w_uq[l], mla_g_ckv[l], mla_w_ukv[l], mla_g_q[l], mla_g_k[l], mla_w_o[l],
                            na_g_q[l], na_g_k[l], na_rpb[l], na_w_o[l], w_out[l])
        x = x + g1 * y
        x = x + g2 * sqrelu_mlp(rms_norm(x, g_norm2[l]) * (1 + sc2) + sh2, w_ff1[l], w_ff2[l])
        if want_ctx:
            xc = xc + gc1 * yc
            xc = xc + gc2 * sqrelu_mlp(rms_norm(xc, g_norm2[l]) * (1 + scc2) + shc2, w_ff1[l], w_ff2[l])
    return x
```

```python
import functools

import numpy as np
import jax
import jax.numpy as jnp
from jax import lax
from jax.experimental import pallas as pl
from jax.experimental.pallas import tpu as pltpu

F32 = jnp.float32
BF16 = jnp.bfloat16

D_MODEL = 1024
GRID_W = 64
ML_HEADS, ML_DQK, ML_DV = 4, 64, 128
ML_CHUNK = 128
MLA_HEADS, MLA_Q_RANK, MLA_KV_RANK, MLA_NOPE, MLA_ROPE, MLA_V = 8, 384, 256, 64, 32, 64
MLA_DQK = MLA_NOPE + MLA_ROPE
NA_HEADS, NA_DH, NA_WIN_R, NA_WIN_C = 8, 64, 8, 16
NA_Q_ROWS = 4
NA_BAND_ROWS = NA_Q_ROWS + NA_WIN_R
D_FF = 4 * D_MODEL
ROPE_THETA = 10000.0
EPS = 1e-6
NEG_INF = -1e30
IN_SIZES = (256, 256, 512, 512, 16, MLA_Q_RANK, MLA_KV_RANK, MLA_ROPE, 512, 512, 512, 3 * D_MODEL)
LANE = 128
TM = 256
MLA_TQ = 512
VMEM_LIMIT = 56 * 1024 * 1024


def _dot(a, b):
    return jnp.dot(a, b, preferred_element_type=F32)


def _dot_nt(a, b):
    return lax.dot_general(a, b, (((1,), (1,)), ((), ())), preferred_element_type=F32)


def _dot_tn(a, b):
    return lax.dot_general(a, b, (((0,), (0,)), ((), ())), preferred_element_type=F32)


def _rms(x, g, n):
    ms = jnp.sum(x * x, axis=-1, keepdims=True) * (1.0 / n)
    return x * lax.rsqrt(ms + EPS) * g


def _split3(x):
    hi = x.astype(BF16)
    r = x - hi.astype(F32)
    mid = r.astype(BF16)
    lo = (r - mid.astype(F32)).astype(BF16)
    return hi, mid, lo


def _params(*sem):
    return pltpu.CompilerParams(dimension_semantics=sem, vmem_limit_bytes=VMEM_LIMIT)


def _const_spec(shape):
    nd = len(shape)
    return pl.BlockSpec(shape, lambda *_: (0,) * nd)


def _mod_kernel(c_ref, w_ref, b_ref, o_ref):
    c = c_ref[...]
    s = c * jax.nn.sigmoid(c)
    o_ref[0] = _dot(s, w_ref[0]) + b_ref[0]


def _modulation(cc, w_mod, b_mod):
    L = w_mod.shape[0]
    tn = 1024
    return pl.pallas_call(
        _mod_kernel,
        out_shape=jax.ShapeDtypeStruct((L, 16, 6 * D_MODEL), F32),
        grid=(L, 6 * D_MODEL // tn),
        in_specs=[pl.BlockSpec((16, D_MODEL), lambda l, n: (0, 0)),
                  pl.BlockSpec((1, D_MODEL, tn), lambda l, n: (l, 0, n)),
                  pl.BlockSpec((1, 1, tn), lambda l, n: (l, 0, n))],
        out_specs=pl.BlockSpec((1, 16, tn), lambda l, n: (l, 0, n)),
        compiler_params=_params("arbitrary", "arbitrary"),
        name="modulation",
    )(cc, w_mod, b_mod.reshape(L, 1, 6 * D_MODEL))


def _rope(x, cos, sina, sinb):
    return x * cos + pltpu.roll(x, 8, 1) * sina + pltpu.roll(x, LANE - 8, 1) * sinb


def _inproj_kernel(x_ref, mod_ref, g1_ref, wa_ref, wat_ref, wb_ref, wc_ref, wct_ref, wuq_ref, wuk_ref, wuvt_ref,
                   gcq_ref, gckv_ref, gq_ref, gk_ref, ngq_ref, ngk_ref, cos_ref, sina_ref, sinb_ref,
                   q_ref, kt_ref, v_ref, o_ref, gcol_ref, grow_ref, qm_ref, km_ref, vmt_ref, qn_ref, kn_ref, vnt_ref):
    x = x_ref[0]
    m = mod_ref[0]
    hb = (_rms(x, g1_ref[...], D_MODEL) * (1.0 + m[1:2]) + m[0:1]).astype(BF16)

    a = _dot(hb, wa_ref[...])
    q_ref[0] = a[:, 0:256].astype(BF16)
    v_ref[0] = a[:, 256:768].astype(BF16)
    o_ref[0] = a[:, 768:1280]
    gcol_ref[0] = a[:, 1280:1408]
    at = _dot_nt(wat_ref[...], hb)
    kt_ref[0] = at[0:256].astype(BF16)
    grow_ref[0] = at[256:272]

    bs = _dot(hb, wb_ref[...])
    dqn = _rms(bs[:, 0:MLA_Q_RANK], gcq_ref[...], MLA_Q_RANK).astype(BF16)
    dkvn = _rms(bs[:, MLA_Q_RANK:MLA_Q_RANK + MLA_KV_RANK], gckv_ref[...], MLA_KV_RANK).astype(BF16)
    kr = bs[:, MLA_Q_RANK + MLA_KV_RANK:]
    qh_all = _dot(dqn, wuq_ref[...])
    kh_all = _dot(dkvn, wuk_ref[...])
    cos, sina, sinb = cos_ref[...], sina_ref[...], sinb_ref[...]
    for h in range(MLA_HEADS):
        sl = slice(h * LANE, (h + 1) * LANE)
        qm_ref[0, h] = _rope(_rms(qh_all[:, sl], gq_ref[...], MLA_DQK), cos, sina, sinb).astype(BF16)
        km_ref[0, h] = _rope(_rms(kh_all[:, sl] + kr, gk_ref[...], MLA_DQK), cos, sina, sinb).astype(BF16)
    vmt_ref[0] = _dot_nt(wuvt_ref[...], dkvn).astype(BF16)

    c = _dot(hb, wc_ref[...])
    lo = lax.broadcasted_iota(jnp.int32, (1, LANE), 1) < NA_DH
    for p in range(NA_HEADS // 2):
        for base, g_ref, out_ref in ((0, ngq_ref, qn_ref), (NA_HEADS * NA_DH, ngk_ref, kn_ref)):
            s = c[:, base + p * LANE: base + (p + 1) * LANE]
            sq = s * s
            s_lo = jnp.sum(jnp.where(lo, sq, 0.0), axis=-1, keepdims=True)
            s_hi = jnp.sum(jnp.where(lo, 0.0, sq), axis=-1, keepdims=True)
            ms = jnp.where(lo, s_lo, s_hi) * (1.0 / NA_DH)
            out_ref[0, p] = (s * lax.rsqrt(ms + EPS) * g_ref[...]).astype(BF16)
    vnt_ref[0] = _dot_nt(wct_ref[...], hb).astype(BF16)


def _inproj(x, mods, g1, w, tabs, n_lat_tiles):
    B, S, _ = x.shape
    nt = S // TM
    tok = lambda b, t: (b, t, 0)
    tokT = lambda b, t: (b, 0, t)
    head = lambda b, t: (b, 0, t, 0)
    in_specs = [pl.BlockSpec((1, TM, D_MODEL), tok),
                pl.BlockSpec((1, 8, D_MODEL), lambda b, t: (jnp.where(t >= n_lat_tiles, B, b), 0, 0)),
                _const_spec(g1.shape)]
    in_specs += [_const_spec(w[k].shape) for k in ("wa", "wat", "wb", "wc", "wct", "wuq", "wuk", "wuvt",
                                                   "gcq", "gckv", "gq", "gk", "ngq", "ngk")]
    in_specs += [pl.BlockSpec((TM, LANE), lambda b, t: (t, 0))] * 3
    out_shape = [jax.ShapeDtypeStruct((B, S, 256), BF16),
                 jax.ShapeDtypeStruct((B, 256, S), BF16),
                 jax.ShapeDtypeStruct((B, S, 512), BF16),
                 jax.ShapeDtypeStruct((B, S, 512), F32),
                 jax.ShapeDtypeStruct((B, S, LANE), F32),
                 jax.ShapeDtypeStruct((B, 16, S), F32),
                 jax.ShapeDtypeStruct((B, MLA_HEADS, S, LANE), BF16),
                 jax.ShapeDtypeStruct((B, MLA_HEADS, S, LANE), BF16),
                 jax.ShapeDtypeStruct((B, MLA_HEADS * MLA_V, S), BF16),
                 jax.ShapeDtypeStruct((B, NA_HEADS // 2, S, LANE), BF16),
                 jax.ShapeDtypeStruct((B, NA_HEADS // 2, S, LANE), BF16),
                 jax.ShapeDtypeStruct((B, NA_HEADS * NA_DH, S), BF16)]
    out_specs = [pl.BlockSpec((1, TM, 256), tok), pl.BlockSpec((1, 256, TM), tokT),
                 pl.BlockSpec((1, TM, 512), tok), pl.BlockSpec((1, TM, 512), tok),
                 pl.BlockSpec((1, TM, LANE), tok), pl.BlockSpec((1, 16, TM), tokT),
                 pl.BlockSpec((1, MLA_HEADS, TM, LANE), head), pl.BlockSpec((1, MLA_HEADS, TM, LANE), head),
                 pl.BlockSpec((1, MLA_HEADS * MLA_V, TM), tokT),
                 pl.BlockSpec((1, NA_HEADS // 2, TM, LANE), head), pl.BlockSpec((1, NA_HEADS // 2, TM, LANE), head),
                 pl.BlockSpec((1, NA_HEADS * NA_DH, TM), tokT)]
    return pl.pallas_call(
        _inproj_kernel, out_shape=out_shape, grid=(B, nt), in_specs=in_specs, out_specs=out_specs,
        compiler_params=_params("arbitrary", "arbitrary"), name="inproj",
    )(x, mods, g1, w["wa"], w["wat"], w["wb"], w["wc"], w["wct"], w["wuq"], w["wuk"], w["wuvt"],
      w["gcq"], w["gckv"], w["gq"], w["gk"], w["ngq"], w["ngk"], *tabs)


def _log_sigmoid(x):
    return jnp.minimum(x, 0.0) - jnp.log1p(jnp.exp(-jnp.abs(x)))


def _mlstm_kernel(qf_ref, ktf_ref, vf_ref, gcf_ref, grf_ref, qb_ref, ktb_ref, vb_ref, gcb_ref, grb_ref,
                  brow_ref, bcol_ref, hf_ref, hb_ref, c_ref, m_ref):
    L = ML_CHUNK

    @pl.when(pl.program_id(1) == 0)
    def _():
        c_ref[...] = jnp.zeros_like(c_ref)
        m_ref[...] = jnp.zeros_like(m_ref)

    row = lax.broadcasted_iota(jnp.int32, (L, L), 0)
    col = lax.broadcasted_iota(jnp.int32, (L, L), 1)
    lane = lax.broadcasted_iota(jnp.int32, (1, LANE), 1)
    isf_row = (lane & ML_HEADS) != 0
    sub = lax.broadcasted_iota(jnp.int32, (16, 1), 0)
    isf_col = (sub & ML_HEADS) != 0
    half_mask = [(lane < ML_DQK).astype(F32).astype(BF16), (lane >= ML_DQK).astype(F32).astype(BF16)]
    e0 = jnp.broadcast_to((lane == 0).astype(F32).astype(BF16), (L, LANE))
    dirs = ((qf_ref, ktf_ref, vf_ref, gcf_ref, grf_ref, hf_ref), (qb_ref, ktb_ref, vb_ref, gcb_ref, grb_ref, hb_ref))
    for d, (q_ref, kt_ref, v_ref, gc_ref, gr_ref, h_ref) in enumerate(dirs):
        causal = (col <= row) if d == 0 else (col >= row)
        tri = causal.astype(F32).astype(BF16)
        tri_t = ((row <= col) if d == 0 else (row >= col)).astype(F32).astype(BF16)
        for cc in range(TM // L):
            ch = cc if d == 0 else TM // L - 1 - cc
            rs = slice(ch * L, (ch + 1) * L)
            pre_c = gc_ref[0, rs, :] + brow_ref[...]
            val_c = jnp.where(isf_row, _log_sigmoid(pre_c), pre_c)
            pre_r = gr_ref[0, :, rs] + bcol_ref[...]
            val_r = jnp.where(isf_col, _log_sigmoid(pre_r), pre_r)
            cum_c = sum(_dot(tri, t) for t in _split3(val_c))
            cum_r = sum(_dot(t, tri_t) for t in _split3(val_r))
            tot_r = jnp.sum(val_r, axis=1, keepdims=True)
            for h in range(ML_HEADS):
                ji, jf = (2 * d) * ML_HEADS + h, (2 * d + 1) * ML_HEADS + h
                p, half = h // 2, h % 2
                idx = d * ML_HEADS + h
                bcum_c = cum_c[:, jf:jf + 1]
                bcum_r = cum_r[jf:jf + 1, :]
                i_r = val_r[ji:ji + 1, :]
                btot = tot_r[jf:jf + 1, :]
                m_old = m_ref[idx, 0:1, 0:1]
                w_end = btot - bcum_r + i_r
                m_new = jnp.maximum(btot + m_old, jnp.max(w_end, axis=1, keepdims=True))
                a_state = jnp.exp(btot + m_old - m_new)
                wk = jnp.exp(w_end - m_new)
                dm = jnp.where(causal, bcum_c - bcum_r + i_r, -jnp.inf)
                inter = bcum_c + m_old
                m_t = jnp.maximum(inter, jnp.max(dm, axis=1, keepdims=True))
                a_inter = jnp.exp(inter - m_t)
                q_h = q_ref[0, rs, p * LANE:(p + 1) * LANE] * half_mask[half]
                kt_pair = kt_ref[0, p * LANE:(p + 1) * LANE, rs]
                kt_h = kt_ref[0, h * ML_DQK:(h + 1) * ML_DQK, rs]
                v_ext = jnp.concatenate([v_ref[0, rs, h * ML_DV:(h + 1) * ML_DV], e0], axis=1)
                c_pair = c_ref[d, p]
                s = _dot(q_h, kt_pair) * jnp.exp(dm - m_t)
                tot = a_inter * _dot(q_h, c_pair.astype(BF16)) + _dot(s.astype(BF16), v_ext)
                den = tot[:, ML_DV:ML_DV + 1]
                h_ref[0, rs, h * ML_DV:(h + 1) * ML_DV] = tot[:, 0:ML_DV] / jnp.maximum(jnp.abs(den), jnp.exp(-m_t))
                kw = (kt_h.astype(F32) * wk).astype(BF16)
                hs = slice(half * ML_DQK, (half + 1) * ML_DQK)
                c_ref[d, p, hs, :] = a_state * c_pair[hs, :] + _dot(kw, v_ext)
                m_ref[idx] = jnp.broadcast_to(m_new, (8, LANE))


def _mlstm(q, kt, v, gcol, grow, brow, bcol, n_lat_tiles):
    B, S, _ = q.shape
    nblk = S // TM
    fwd = lambda j: jnp.where(j == 0, n_lat_tiles, j - 1)
    bwd = lambda j: jnp.where(j == 0, n_lat_tiles, n_lat_tiles - j)

    def specs(order):
        return [pl.BlockSpec((1, TM, 256), lambda b, j: (b, order(j), 0)),
                pl.BlockSpec((1, 256, TM), lambda b, j: (b, 0, order(j))),
                pl.BlockSpec((1, TM, 512), lambda b, j: (b, order(j), 0)),
                pl.BlockSpec((1, TM, LANE), lambda b, j: (b, order(j), 0)),
                pl.BlockSpec((1, 16, TM), lambda b, j: (b, 0, order(j)))]

    return pl.pallas_call(
        _mlstm_kernel,
        out_shape=[jax.ShapeDtypeStruct((B, S, 512), F32)] * 2,
        grid=(B, nblk),
        in_specs=specs(fwd) + specs(bwd) + [_const_spec(brow.shape), _const_spec(bcol.shape)],
        out_specs=[pl.BlockSpec((1, TM, 512), lambda b, j: (b, fwd(j), 0)),
                   pl.BlockSpec((1, TM, 512), lambda b, j: (b, bwd(j), 0))],
        scratch_shapes=[pltpu.VMEM((2, ML_HEADS // 2, 2 * ML_DQK, 2 * ML_DV), F32),
                        pltpu.VMEM((2 * ML_HEADS, 8, LANE), F32)],
        compiler_params=_params("arbitrary", "arbitrary"), name="mlstm",
    )(q, kt, v, gcol, grow, q, kt, v, gcol, grow, brow, bcol)


def _softmax_pv_t(parts, scale):
    ss = []
    for k, q, _, bias in parts:
        s = _dot_nt(k, q) * scale
        ss.append(s if bias is None else s + bias)
    m = functools.reduce(jnp.maximum, [jnp.max(s, axis=0, keepdims=True) for s in ss])
    ps = [jnp.exp(s - m) for s in ss]
    l = sum(jnp.sum(p, axis=0, keepdims=True) for p in ps)
    o = sum(_dot(vt, p.astype(BF16)) for (_, _, vt, _), p in zip(parts, ps))
    return o / l


def _mla_kernel(q_ref, k_ref, vt_ref, o_ref, *, chunks, scale):
    q = q_ref[0, 0]
    m = l = acc = None
    for off, size in chunks:
        s = _dot_nt(k_ref[0, 0, off:off + size, :], q) * scale
        mc = jnp.max(s, axis=0, keepdims=True)
        vt = vt_ref[0, :, off:off + size]
        if m is None:
            m = mc
            p = jnp.exp(s - m)
            l = jnp.sum(p, axis=0, keepdims=True)
            acc = _dot(vt, p.astype(BF16))
        else:
            m_new = jnp.maximum(m, mc)
            alpha = jnp.exp(m - m_new)
            p = jnp.exp(s - m_new)
            l = alpha * l + jnp.sum(p, axis=0, keepdims=True)
            acc = alpha * acc + _dot(vt, p.astype(BF16))
            m = m_new
    o_ref[0] = (acc / l).astype(BF16)


def _mla(qm, km, vmt, n_lat):
    B, H, S, _ = qm.shape
    kv = 512
    chunks = tuple((i * kv, kv) for i in range(n_lat // kv)) + ((n_lat, S - n_lat),)
    return pl.pallas_call(
        functools.partial(_mla_kernel, chunks=chunks, scale=MLA_DQK ** -0.5),
        out_shape=jax.ShapeDtypeStruct((B, H * MLA_V, S), BF16),
        grid=(B, H, n_lat // MLA_TQ),
        in_specs=[pl.BlockSpec((1, 1, MLA_TQ, LANE), lambda b, h, i: (b, h, i, 0)),
                  pl.BlockSpec((1, 1, S, LANE), lambda b, h, i: (b, h, 0, 0)),
                  pl.BlockSpec((1, MLA_V, S), lambda b, h, i: (b, h, 0))],
        out_specs=pl.BlockSpec((1, MLA_V, MLA_TQ), lambda b, h, i: (b, h, i)),
        compiler_params=_params("arbitrary", "arbitrary", "arbitrary"), name="mla_attn",
    )(qm, km, vmt)


def _natten_kernel(q_ref, k_ref, vt_ref, bias_ref, o_ref, *, n_lat, n_ctx, rows):
    blk = pl.program_id(2)
    band = NA_BAND_ROWS * GRID_W
    u0 = jnp.clip(blk * NA_Q_ROWS - NA_WIN_R // 2, 0, rows - NA_BAND_ROWS)
    koff = pl.multiple_of(u0 * GRID_W, 2 * LANE)
    kb = k_ref[0, 0, pl.ds(koff, band), :]
    kc = k_ref[0, 0, n_lat:n_lat + n_ctx, :]
    q = q_ref[0, 0]
    lane = lax.broadcasted_iota(jnp.int32, (1, LANE), 1)
    for hh in range(2):
        qh = q * ((lane < NA_DH) if hh == 0 else (lane >= NA_DH)).astype(F32).astype(BF16)
        hs = slice(hh * NA_DH, (hh + 1) * NA_DH)
        parts = [(kb, qh, vt_ref[0, hs, pl.ds(koff, band)], bias_ref[hh, 0]),
                 (kc, qh, vt_ref[0, hs, n_lat:n_lat + n_ctx], None)]
        o_ref[0, hs, :] = _softmax_pv_t(parts, NA_DH ** -0.5).astype(BF16)


def _natten(qn, kn, vnt, bias, n_lat):
    B, P, S, _ = qn.shape
    rows = n_lat // GRID_W
    nblk = rows // NA_Q_ROWS
    tq = NA_Q_ROWS * GRID_W
    band = NA_BAND_ROWS * GRID_W
    btype = lambda i: jnp.where(i == 0, 0, jnp.where(i == nblk - 1, 2, 1))
    return pl.pallas_call(
        functools.partial(_natten_kernel, n_lat=n_lat, n_ctx=S - n_lat, rows=rows),
        out_shape=jax.ShapeDtypeStruct((B, NA_HEADS * NA_DH, S), BF16),
        grid=(B, P, nblk),
        in_specs=[pl.BlockSpec((1, 1, tq, LANE), lambda b, p, i: (b, p, i, 0)),
                  pl.BlockSpec((1, 1, S, LANE), lambda b, p, i: (b, p, 0, 0)),
                  pl.BlockSpec((1, 2 * NA_DH, S), lambda b, p, i: (b, p, 0)),
                  pl.BlockSpec((2, 1, band, tq), lambda b, p, i: (p, btype(i), 0, 0))],
        out_specs=pl.BlockSpec((1, 2 * NA_DH, tq), lambda b, p, i: (b, p, i)),
        compiler_params=_params("arbitrary", "arbitrary", "arbitrary"), name="natten",
    )(qn, kn, vnt, bias)


def _na_bias_kernel(rpb_ref, o_ref, bc_ref, *, rows):
    l, h = pl.program_id(0), pl.program_id(1)
    tq = NA_Q_ROWS * GRID_W
    kc = lax.broadcasted_iota(jnp.int32, (GRID_W, tq), 0)
    ql = lax.broadcasted_iota(jnp.int32, (GRID_W, tq), 1)
    c = ql & (GRID_W - 1)
    qi = ql >> 6
    dc = kc - c + (NA_WIN_C - 1)
    cs = jnp.clip(c - NA_WIN_C // 2, 0, GRID_W - NA_WIN_C)
    col_ok = (kc >= cs) & (kc < cs + NA_WIN_C)
    n_dr, n_dc = 2 * NA_WIN_R - 1, 2 * NA_WIN_C - 1
    base = (l * NA_HEADS + h) * (n_dr * n_dc)
    for dr in range(n_dr):
        acc = jnp.zeros((GRID_W, tq), F32)
        for j in range(n_dc):
            acc = jnp.where(dc == j, rpb_ref[base + dr * n_dc + j], acc)
        bc_ref[dr] = jnp.where(col_ok, acc, NEG_INF)
    nblk = rows // NA_Q_ROWS
    for t, blk in enumerate((0, 1, nblk - 1)):
        r0 = blk * NA_Q_ROWS
        u0 = min(max(r0 - NA_WIN_R // 2, 0), rows - NA_BAND_ROWS)
        for j in range(NA_BAND_ROWS):
            tile = jnp.full((GRID_W, tq), NEG_INF, F32)
            for i in range(NA_Q_ROWS):
                r, kr = r0 + i, u0 + j
                rs = min(max(r - NA_WIN_R // 2, 0), rows - NA_WIN_R)
                if rs <= kr < rs + NA_WIN_R:
                    tile = jnp.where(qi == i, bc_ref[kr - r + NA_WIN_R - 1], tile)
            o_ref[0, 0, t, j * GRID_W:(j + 1) * GRID_W, :] = tile


def _na_bias(rpb, rows):
    L = rpb.shape[0]
    tq = NA_Q_ROWS * GRID_W
    band = NA_BAND_ROWS * GRID_W
    return pl.pallas_call(
        functools.partial(_na_bias_kernel, rows=rows),
        out_shape=jax.ShapeDtypeStruct((L, NA_HEADS, 3, band, tq), F32),
        grid=(L, NA_HEADS),
        in_specs=[pl.BlockSpec(memory_space=pltpu.SMEM)],
        out_specs=pl.BlockSpec((1, 1, 3, band, tq), lambda l, h: (l, h, 0, 0, 0)),
        scratch_shapes=[pltpu.VMEM((2 * NA_WIN_R - 1, GRID_W, tq), F32)],
        compiler_params=_params("arbitrary", "arbitrary"), name="na_bias",
    )(rpb.reshape(-1))


def _ctx_attn_kernel(mo_in, no_in, qm_ref, km_ref, vmt_ref, qn_ref, kn_ref, vnt_ref, mo_ref, no_ref):
    del mo_in, no_in
    lane = lax.broadcasted_iota(jnp.int32, (1, LANE), 1)
    for hh in range(2):
        hs = slice(hh * MLA_V, (hh + 1) * MLA_V)
        mo_ref[0, hs, :] = _softmax_pv_t([(km_ref[0, hh], qm_ref[0, hh], vmt_ref[0, hs, :], None)],
                                         MLA_DQK ** -0.5).astype(BF16)
        qh = qn_ref[0, 0] * ((lane < NA_DH) if hh == 0 else (lane >= NA_DH)).astype(F32).astype(BF16)
        no_ref[0, hs, :] = _softmax_pv_t([(kn_ref[0, 0], qh, vnt_ref[0, hs, :], None)],
                                         NA_DH ** -0.5).astype(BF16)


def _ctx_attn(mla_o, na_o, qm, km, vmt, qn, kn, vnt, n_lat):
    B, _, S, _ = qm.shape
    n_ctx = S - n_lat
    t = n_lat // n_ctx
    head = lambda b, p: (b, p, t, 0)
    rowsT = lambda b, p: (b, p, t)
    any_spec = pl.BlockSpec(memory_space=pl.ANY)
    return pl.pallas_call(
        _ctx_attn_kernel,
        out_shape=[jax.ShapeDtypeStruct(mla_o.shape, BF16), jax.ShapeDtypeStruct(na_o.shape, BF16)],
        grid=(B, NA_HEADS // 2),
        in_specs=[any_spec, any_spec,
                  pl.BlockSpec((1, 2, n_ctx, LANE), head), pl.BlockSpec((1, 2, n_ctx, LANE), head),
                  pl.BlockSpec((1, 2 * MLA_V, n_ctx), rowsT),
                  pl.BlockSpec((1, 1, n_ctx, LANE), head), pl.BlockSpec((1, 1, n_ctx, LANE), head),
                  pl.BlockSpec((1, 2 * NA_DH, n_ctx), rowsT)],
        out_specs=[pl.BlockSpec((1, 2 * MLA_V, n_ctx), rowsT), pl.BlockSpec((1, 2 * NA_DH, n_ctx), rowsT)],
        input_output_aliases={0: 0, 1: 1},
        compiler_params=_params("arbitrary", "arbitrary"), name="ctx_attn",
    )(mla_o, na_o, qm, km, vmt, qn, kn, vnt)


def _merge_kernel(x_ref, mod_ref, g1_ref, hf_ref, hb_ref, o_ref, mo_ref, no_ref, wg_ref, gout_ref,
                  wml_ref, wmla_ref, wna_ref, wout_ref, x1_ref):
    x = x_ref[0]
    m = mod_ref[0]
    hb = (_rms(x, g1_ref[...], D_MODEL) * (1.0 + m[1:2]) + m[0:1]).astype(BF16)
    gates = jax.nn.sigmoid(_dot(hb, wg_ref[...]))
    hs = hf_ref[0] + hb_ref[0]
    og = jax.nn.sigmoid(o_ref[0])
    gout = gout_ref[...]
    hn = jnp.concatenate(
        [_rms(hs[:, h * ML_DV:(h + 1) * ML_DV], gout[:, h * ML_DV:(h + 1) * ML_DV], ML_DV) for h in range(ML_HEADS)],
        axis=1)
    y_a = _dot((hn * og).astype(BF16), wml_ref[...])
    y_b = _dot_tn(mo_ref[0], wmla_ref[...])
    y_c = _dot_tn(no_ref[0], wna_ref[...])
    mg = gates[:, 0:D_MODEL] * y_a + gates[:, D_MODEL:2 * D_MODEL] * y_b + gates[:, 2 * D_MODEL:] * y_c
    x1_ref[0] = x + m[2:3] * _dot(mg.astype(BF16), wout_ref[...])


def _merge(x, mods, g1, hf, hb, o, mla_o, na_o, w, nt, n_lat_tiles):
    B = x.shape[0]
    tok = lambda b, t: (b, t, 0)
    tokT = lambda b, t: (b, 0, t)
    in_specs = [pl.BlockSpec((1, TM, D_MODEL), tok),
                pl.BlockSpec((1, 8, D_MODEL), lambda b, t: (jnp.where(t >= n_lat_tiles, B, b), 0, 0)),
                _const_spec(g1.shape),
                pl.BlockSpec((1, TM, 512), tok), pl.BlockSpec((1, TM, 512), tok), pl.BlockSpec((1, TM, 512), tok),
                pl.BlockSpec((1, 512, TM), tokT), pl.BlockSpec((1, 512, TM), tokT)]
    names = ("wg", "gout", "wml", "wmla", "wna", "wout")
    in_specs += [_const_spec(w[k].shape) for k in names]
    return pl.pallas_call(
        _merge_kernel,
        out_shape=jax.ShapeDtypeStruct((B, nt * TM, D_MODEL), F32),
        grid=(B, nt), in_specs=in_specs, out_specs=pl.BlockSpec((1, TM, D_MODEL), tok),
        compiler_params=_params("arbitrary", "arbitrary"), name="merge",
    )(x, mods, g1, hf, hb, o, mla_o, na_o, *[w[k] for k in names])


def _mlp_kernel(x_ref, mod_ref, g2_ref, w1_ref, w2_ref, o_ref):
    x = x_ref[0]
    m = mod_ref[0]
    hb = (_rms(x, g2_ref[...], D_MODEL) * (1.0 + m[4:5]) + m[3:4]).astype(BF16)
    acc = jnp.zeros((TM, D_MODEL), F32)
    fc = 1024
    for c in range(D_FF // fc):
        u = jnp.maximum(_dot(hb, w1_ref[:, c * fc:(c + 1) * fc]), 0.0)
        acc = acc + _dot((u * u).astype(BF16), w2_ref[c * fc:(c + 1) * fc, :])
    o_ref[0] = x + m[5:6] * acc


def _mlp(x1, mods, g2, w1, w2, nt, n_lat_tiles):
    B = x1.shape[0]
    tok = lambda b, t: (b, t, 0)
    return pl.pallas_call(
        _mlp_kernel,
        out_shape=jax.ShapeDtypeStruct((B, nt * TM, D_MODEL), F32),
        grid=(B, nt),
        in_specs=[pl.BlockSpec((1, TM, D_MODEL), tok),
                  pl.BlockSpec((1, 8, D_MODEL), lambda b, t: (jnp.where(t >= n_lat_tiles, B, b), 0, 0)),
                  _const_spec(g2.shape), _const_spec(w1.shape), _const_spec(w2.shape)],
        out_specs=pl.BlockSpec((1, TM, D_MODEL), tok),
        compiler_params=_params("arbitrary", "arbitrary"), name="mlp",
    )(x1, mods, g2, w1, w2)


def _rope_tables(n_lat, n_ctx):
    nf = MLA_ROPE // 4
    inv = jnp.power(ROPE_THETA, -jnp.arange(nf, dtype=F32) / nf)
    pos = jnp.arange(n_lat)
    ang_r = (pos // GRID_W).astype(F32)[:, None] * inv
    ang_c = (pos % GRID_W).astype(F32)[:, None] * inv
    z = jnp.zeros((n_lat, nf), F32)
    one = lambda n: jnp.ones((n_lat, n), F32)
    zero = lambda n: jnp.zeros((n_lat, n), F32)
    cr, sr, cc, sc = jnp.cos(ang_r), jnp.sin(ang_r), jnp.cos(ang_c), jnp.sin(ang_c)
    cos = jnp.concatenate([one(MLA_NOPE), cr, cr, cc, cc, one(LANE - MLA_DQK)], axis=1)
    sina = jnp.concatenate([zero(MLA_NOPE), z, sr, z, sc, zero(LANE - MLA_DQK)], axis=1)
    sinb = jnp.concatenate([zero(MLA_NOPE), -sr, z, -sc, z, zero(LANE - MLA_DQK)], axis=1)
    ident = (jnp.ones((n_ctx, LANE), F32), jnp.zeros((n_ctx, LANE), F32), jnp.zeros((n_ctx, LANE), F32))
    return tuple(jnp.concatenate([t, i], axis=0) for t, i in zip((cos, sina, sinb), ident))


def _layer_weights(l, w_in, ml_i_bias, ml_f_bias, ml_g_out, ml_w_o, mla_g_cq, mla_w_uq, mla_g_ckv, mla_w_ukv,
                   mla_g_q, mla_g_k, mla_w_o, na_g_q, na_g_k, na_w_o, w_out, w_ff1, w_ff2):
    offs = np.cumsum(IN_SIZES)[:-1].tolist()
    mlq, mlk, mlv, mlo, mlg, dq, dkv, kr, naq, nak, nav, gates = jnp.split(w_in[l], offs, axis=-1)
    padc = lambda a, n: jnp.pad(a, ((0, 0), (0, n - a.shape[1])))
    kr_slab = jnp.pad(kr, ((0, 0), (MLA_NOPE, LANE - MLA_DQK)))
    w_uq = jnp.pad(mla_w_uq[l].reshape(MLA_Q_RANK, MLA_HEADS, MLA_DQK), ((0, 0), (0, 0), (0, LANE - MLA_DQK)))
    w_ukv = mla_w_ukv[l].reshape(MLA_KV_RANK, MLA_HEADS, MLA_NOPE + MLA_V)
    w_uk = jnp.pad(w_ukv[..., :MLA_NOPE], ((0, 0), (0, 0), (0, LANE - MLA_NOPE)))
    w_uv = w_ukv[..., MLA_NOPE:].reshape(MLA_KV_RANK, MLA_HEADS * MLA_V)
    row = lambda a: a.reshape(1, -1).astype(F32)
    bias16 = jnp.stack([ml_i_bias[l, 0], ml_f_bias[l, 0], ml_i_bias[l, 1], ml_f_bias[l, 1]]).reshape(-1).astype(F32)
    return {
        "wa": jnp.concatenate([mlq * (ML_DQK ** -0.5), mlv, mlo, padc(mlg, LANE)], axis=1).astype(BF16),
        "wat": jnp.concatenate([mlk.T, mlg.T], axis=0).astype(BF16),
        "wb": jnp.concatenate([dq, dkv, kr_slab], axis=1).astype(BF16),
        "wc": jnp.concatenate([naq, nak], axis=1).astype(BF16),
        "wct": nav.T.astype(BF16),
        "wuq": w_uq.reshape(MLA_Q_RANK, MLA_HEADS * LANE).astype(BF16),
        "wuk": w_uk.reshape(MLA_KV_RANK, MLA_HEADS * LANE).astype(BF16),
        "wuvt": w_uv.T.astype(BF16),
        "gcq": row(mla_g_cq[l]), "gckv": row(mla_g_ckv[l]),
        "gq": padc(row(mla_g_q[l]), LANE), "gk": padc(row(mla_g_k[l]), LANE),
        "ngq": row(jnp.tile(na_g_q[l], 2)), "ngk": row(jnp.tile(na_g_k[l], 2)),
        "brow": padc(bias16.reshape(1, 16), LANE), "bcol": bias16.reshape(16, 1),
        "wg": gates.astype(BF16), "gout": row(ml_g_out[l]),
        "wml": ml_w_o[l].astype(BF16), "wmla": mla_w_o[l].astype(BF16), "wna": na_w_o[l].astype(BF16),
        "wout": w_out[l].astype(BF16), "w1": w_ff1[l].astype(BF16), "w2": w_ff2[l].astype(BF16),
    }


def kernel(x, c, ctx, c_ctx, w_mod, b_mod, g_norm1, g_norm2, w_in, ml_i_bias, ml_f_bias, ml_g_out, ml_w_o, mla_g_cq, mla_w_uq, mla_g_ckv, mla_w_ukv, mla_g_q, mla_g_k, mla_w_o, na_g_q, na_g_k, na_rpb, na_w_o, w_out, w_ff1, w_ff2):
    B, T, D = x.shape
    C = ctx.shape[1]
    depth = w_in.shape[0]
    assert D == D_MODEL and C == TM and T % MLA_TQ == 0 and B < 16
    rows = T // GRID_W
    assert rows % NA_Q_ROWS == 0 and rows >= NA_BAND_ROWS
    n_lat_tiles = T // TM

    cc = jnp.zeros((16, D), F32).at[:B].set(c).at[B].set(c_ctx)
    mod = _modulation(cc, w_mod, b_mod).reshape(depth, 16, 6, D)
    mod = jnp.pad(mod, ((0, 0), (0, 0), (0, 2), (0, 0)))
    tabs = _rope_tables(T, C)
    na_bias = _na_bias(na_rpb.astype(F32), rows)
    xs = jnp.concatenate([x, ctx], axis=1)

    for l in range(depth):
        last = l == depth - 1
        w = _layer_weights(l, w_in, ml_i_bias, ml_f_bias, ml_g_out, ml_w_o, mla_g_cq, mla_w_uq, mla_g_ckv,
                           mla_w_ukv, mla_g_q, mla_g_k, mla_w_o, na_g_q, na_g_k, na_w_o, w_out, w_ff1, w_ff2)
        g1 = g_norm1[l].reshape(1, D).astype(F32)
        g2 = g_norm2[l].reshape(1, D).astype(F32)
        q, kt, v, o, gcol, grow, qm, km, vmt, qn, kn, vnt = _inproj(xs, mod[l], g1, w, tabs, n_lat_tiles)
        hf, hb = _mlstm(q, kt, v, gcol, grow, w["brow"], w["bcol"], n_lat_tiles)
        mla_o = _mla(qm, km, vmt, T)
        na_o = _natten(qn, kn, vnt, na_bias[l], T)
        if not last:
            mla_o, na_o = _ctx_attn(mla_o, na_o, qm, km, vmt, qn, kn, vnt, T)
        nt = n_lat_tiles if last else n_lat_tiles + 1
        x1 = _merge(xs, mod[l], g1, hf, hb, o, mla_o, na_o, w, nt, n_lat_tiles)
        xs = _mlp(x1, mod[l], g2, w["w1"], w["w2"], nt, n_lat_tiles)
    return xs
```

```python
import functools

import numpy as np
import jax
import jax.numpy as jnp
from jax import lax
from jax.experimental import pallas as pl
from jax.experimental.pallas import tpu as pltpu

F32 = jnp.float32
BF16 = jnp.bfloat16

D_MODEL = 1024
GRID_W = 64
ML_HEADS, ML_DQK, ML_DV = 4, 64, 128
ML_CHUNK = 128
MLA_HEADS, MLA_Q_RANK, MLA_KV_RANK, MLA_NOPE, MLA_ROPE, MLA_V = 8, 384, 256, 64, 32, 64
MLA_DQK = MLA_NOPE + MLA_ROPE
NA_HEADS, NA_DH, NA_WIN_R, NA_WIN_C = 8, 64, 8, 16
NA_Q_ROWS = 4
NA_BAND_ROWS = NA_Q_ROWS + NA_WIN_R
D_FF = 4 * D_MODEL
ROPE_THETA = 10000.0
EPS = 1e-6
LOG2E = 1.4426950408889634
NEG_INF = -1e30
IN_SIZES = (256, 256, 512, 512, 16, MLA_Q_RANK, MLA_KV_RANK, MLA_ROPE, 512, 512, 512, 3 * D_MODEL)
LANE = 128
TM = 256
MLA_TQ = 512
VMEM_LIMIT = 56 * 1024 * 1024


def _dot(a, b):
    return jnp.dot(a, b, preferred_element_type=F32)


def _dot_nt(a, b):
    return lax.dot_general(a, b, (((1,), (1,)), ((), ())), preferred_element_type=F32)


def _dot_tn(a, b):
    return lax.dot_general(a, b, (((0,), (0,)), ((), ())), preferred_element_type=F32)


def _rms(x, g, n):
    ms = jnp.sum(x * x, axis=-1, keepdims=True) * (1.0 / n)
    return x * lax.rsqrt(ms + EPS) * g


def _split3(x):
    hi = x.astype(BF16)
    r = x - hi.astype(F32)
    mid = r.astype(BF16)
    lo = (r - mid.astype(F32)).astype(BF16)
    return hi, mid, lo


def _params(*sem):
    return pltpu.CompilerParams(dimension_semantics=sem, vmem_limit_bytes=VMEM_LIMIT)


def _const_spec(shape):
    nd = len(shape)
    return pl.BlockSpec(shape, lambda *_: (0,) * nd)


def _mod_kernel(c_ref, w_ref, b_ref, o_ref):
    c = c_ref[...]
    s = c * jax.nn.sigmoid(c)
    o_ref[0] = _dot(s, w_ref[0]) + b_ref[0]


def _modulation(cc, w_mod, b_mod):
    L = w_mod.shape[0]
    tn = 1024
    return pl.pallas_call(
        _mod_kernel,
        out_shape=jax.ShapeDtypeStruct((L, 16, 6 * D_MODEL), F32),
        grid=(L, 6 * D_MODEL // tn),
        in_specs=[pl.BlockSpec((16, D_MODEL), lambda l, n: (0, 0)),
                  pl.BlockSpec((1, D_MODEL, tn), lambda l, n: (l, 0, n)),
                  pl.BlockSpec((1, 1, tn), lambda l, n: (l, 0, n))],
        out_specs=pl.BlockSpec((1, 16, tn), lambda l, n: (l, 0, n)),
        compiler_params=_params("arbitrary", "arbitrary"),
        name="modulation",
    )(cc, w_mod, b_mod.reshape(L, 1, 6 * D_MODEL))


def _rope(x, cos, sina, sinb):
    return x * cos + pltpu.roll(x, 8, 1) * sina + pltpu.roll(x, LANE - 8, 1) * sinb


def _inproj_kernel(x_ref, mod_ref, g1_ref, wa_ref, wat_ref, wb_ref, wc_ref, wct_ref, wuq_ref, wuk_ref, wuvt_ref,
                   gcq_ref, gckv_ref, gq_ref, gk_ref, ngq_ref, ngk_ref, cos_ref, sina_ref, sinb_ref,
                   q_ref, kt_ref, v_ref, o_ref, gcol_ref, grow_ref, qm_ref, km_ref, vmt_ref, qn_ref, kn_ref, vnt_ref):
    x = x_ref[0]
    m = mod_ref[0]
    hb = (_rms(x, g1_ref[...], D_MODEL) * (1.0 + m[1:2]) + m[0:1]).astype(BF16)

    a = _dot(hb, wa_ref[...])
    q_ref[0] = a[:, 0:256].astype(BF16)
    v_ref[0] = a[:, 256:768].astype(BF16)
    o_ref[0] = a[:, 768:1280]
    gcol_ref[0] = a[:, 1280:1408]
    at = _dot_nt(wat_ref[...], hb)
    kt_ref[0] = at[0:256].astype(BF16)
    grow_ref[0] = at[256:272]

    bs = _dot(hb, wb_ref[...])
    dqn = _rms(bs[:, 0:MLA_Q_RANK], gcq_ref[...], MLA_Q_RANK).astype(BF16)
    dkvn = _rms(bs[:, MLA_Q_RANK:MLA_Q_RANK + MLA_KV_RANK], gckv_ref[...], MLA_KV_RANK).astype(BF16)
    kr = bs[:, MLA_Q_RANK + MLA_KV_RANK:]
    qh_all = _dot(dqn, wuq_ref[...])
    kh_all = _dot(dkvn, wuk_ref[...])
    cos, sina, sinb = cos_ref[...], sina_ref[...], sinb_ref[...]
    for h in range(MLA_HEADS):
        sl = slice(h * LANE, (h + 1) * LANE)
        qm_ref[0, h] = _rope(_rms(qh_all[:, sl], gq_ref[...], MLA_DQK), cos, sina, sinb).astype(BF16)
        km_ref[0, h] = _rope(_rms(kh_all[:, sl] + kr, gk_ref[...], MLA_DQK), cos, sina, sinb).astype(BF16)
    vmt_ref[0] = _dot_nt(wuvt_ref[...], dkvn).astype(BF16)

    c = _dot(hb, wc_ref[...])
    lo = lax.broadcasted_iota(jnp.int32, (1, LANE), 1) < NA_DH
    for p in range(NA_HEADS // 2):
        for base, g_ref, out_ref in ((0, ngq_ref, qn_ref), (NA_HEADS * NA_DH, ngk_ref, kn_ref)):
            s = c[:, base + p * LANE: base + (p + 1) * LANE]
            sq = s * s
            s_lo = jnp.sum(jnp.where(lo, sq, 0.0), axis=-1, keepdims=True)
            s_hi = jnp.sum(jnp.where(lo, 0.0, sq), axis=-1, keepdims=True)
            ms = jnp.where(lo, s_lo, s_hi) * (1.0 / NA_DH)
            out_ref[0, p] = (s * lax.rsqrt(ms + EPS) * g_ref[...]).astype(BF16)
    vnt_ref[0] = _dot_nt(wct_ref[...], hb).astype(BF16)


def _inproj(x, mods, g1, w, tabs, n_lat_tiles):
    B, S, _ = x.shape
    nt = S // TM
    tok = lambda b, t: (b, t, 0)
    tokT = lambda b, t: (b, 0, t)
    head = lambda b, t: (b, 0, t, 0)
    in_specs = [pl.BlockSpec((1, TM, D_MODEL), tok),
                pl.BlockSpec((1, 8, D_MODEL), lambda b, t: (jnp.where(t >= n_lat_tiles, B, b), 0, 0)),
                _const_spec(g1.shape)]
    in_specs += [_const_spec(w[k].shape) for k in ("wa", "wat", "wb", "wc", "wct", "wuq", "wuk", "wuvt",
                                                   "gcq", "gckv", "gq", "gk", "ngq", "ngk")]
    in_specs += [pl.BlockSpec((TM, LANE), lambda b, t: (t, 0))] * 3
    out_shape = [jax.ShapeDtypeStruct((B, S, 256), BF16),
                 jax.ShapeDtypeStruct((B, 256, S), BF16),
                 jax.ShapeDtypeStruct((B, S, 512), BF16),
                 jax.ShapeDtypeStruct((B, S, 512), F32),
                 jax.ShapeDtypeStruct((B, S, LANE), F32),
                 jax.ShapeDtypeStruct((B, 16, S), F32),
                 jax.ShapeDtypeStruct((B, MLA_HEADS, S, LANE), BF16),
                 jax.ShapeDtypeStruct((B, MLA_HEADS, S, LANE), BF16),
                 jax.ShapeDtypeStruct((B, MLA_HEADS * MLA_V, S), BF16),
                 jax.ShapeDtypeStruct((B, NA_HEADS // 2, S, LANE), BF16),
                 jax.ShapeDtypeStruct((B, NA_HEADS // 2, S, LANE), BF16),
                 jax.ShapeDtypeStruct((B, NA_HEADS * NA_DH, S), BF16)]
    out_specs = [pl.BlockSpec((1, TM, 256), tok), pl.BlockSpec((1, 256, TM), tokT),
                 pl.BlockSpec((1, TM, 512), tok), pl.BlockSpec((1, TM, 512), tok),
                 pl.BlockSpec((1, TM, LANE), tok), pl.BlockSpec((1, 16, TM), tokT),
                 pl.BlockSpec((1, MLA_HEADS, TM, LANE), head), pl.BlockSpec((1, MLA_HEADS, TM, LANE), head),
                 pl.BlockSpec((1, MLA_HEADS * MLA_V, TM), tokT),
                 pl.BlockSpec((1, NA_HEADS // 2, TM, LANE), head), pl.BlockSpec((1, NA_HEADS // 2, TM, LANE), head),
                 pl.BlockSpec((1, NA_HEADS * NA_DH, TM), tokT)]
    return pl.pallas_call(
        _inproj_kernel, out_shape=out_shape, grid=(B, nt), in_specs=in_specs, out_specs=out_specs,
        compiler_params=_params("arbitrary", "arbitrary"), name="inproj",
    )(x, mods, g1, w["wa"], w["wat"], w["wb"], w["wc"], w["wct"], w["wuq"], w["wuk"], w["wuvt"],
      w["gcq"], w["gckv"], w["gq"], w["gk"], w["ngq"], w["ngk"], *tabs)


def _log_sigmoid(x):
    return jnp.minimum(x, 0.0) - jnp.log1p(jnp.exp(-jnp.abs(x)))


def _mlstm_kernel(qf_ref, ktf_ref, vf_ref, gcf_ref, grf_ref, qb_ref, ktb_ref, vb_ref, gcb_ref, grb_ref,
                  brow_ref, bcol_ref, hf_ref, hb_ref, c_ref, m_ref):
    L = ML_CHUNK

    @pl.when(pl.program_id(1) == 0)
    def _():
        c_ref[...] = jnp.zeros_like(c_ref)
        m_ref[...] = jnp.zeros_like(m_ref)

    row = lax.broadcasted_iota(jnp.int32, (L, L), 0)
    col = lax.broadcasted_iota(jnp.int32, (L, L), 1)
    lane = lax.broadcasted_iota(jnp.int32, (1, LANE), 1)
    isf_row = (lane & ML_HEADS) != 0
    sub = lax.broadcasted_iota(jnp.int32, (16, 1), 0)
    isf_col = (sub & ML_HEADS) != 0
    half_mask = [(lane < ML_DQK).astype(F32).astype(BF16), (lane >= ML_DQK).astype(F32).astype(BF16)]
    e0 = jnp.broadcast_to((lane == 0).astype(F32).astype(BF16), (L, LANE))
    dirs = ((qf_ref, ktf_ref, vf_ref, gcf_ref, grf_ref, hf_ref), (qb_ref, ktb_ref, vb_ref, gcb_ref, grb_ref, hb_ref))
    for d, (q_ref, kt_ref, v_ref, gc_ref, gr_ref, h_ref) in enumerate(dirs):
        causal = (col <= row) if d == 0 else (col >= row)
        tri = causal.astype(F32).astype(BF16)
        tri_t = ((row <= col) if d == 0 else (row >= col)).astype(F32).astype(BF16)
        for cc in range(TM // L):
            ch = cc if d == 0 else TM // L - 1 - cc
            rs = slice(ch * L, (ch + 1) * L)
            pre_c = gc_ref[0, rs, :] + brow_ref[...]
            val_c = jnp.where(isf_row, _log_sigmoid(pre_c), pre_c)
            pre_r = gr_ref[0, :, rs] + bcol_ref[...]
            val_r = jnp.where(isf_col, _log_sigmoid(pre_r), pre_r)
            cum_c = sum(_dot(tri, t) for t in _split3(val_c))
            cum_r = sum(_dot(t, tri_t) for t in _split3(val_r))
            tot_r = jnp.sum(val_r, axis=1, keepdims=True)
            for h in range(ML_HEADS):
                ji, jf = (2 * d) * ML_HEADS + h, (2 * d + 1) * ML_HEADS + h
                p, half = h // 2, h % 2
                idx = d * ML_HEADS + h
                bcum_c = cum_c[:, jf:jf + 1]
                bcum_r = cum_r[jf:jf + 1, :]
                i_r = val_r[ji:ji + 1, :]
                btot = tot_r[jf:jf + 1, :]
                m_old = m_ref[idx, 0:1, 0:1]
                w_end = btot - bcum_r + i_r
                m_new = jnp.maximum(btot + m_old, jnp.max(w_end, axis=1, keepdims=True))
                a_state = jnp.exp(btot + m_old - m_new)
                wk = jnp.exp(w_end - m_new)
                dm = jnp.where(causal, bcum_c - bcum_r + i_r, -jnp.inf)
                inter = bcum_c + m_old
                m_t = jnp.maximum(inter, jnp.max(dm, axis=1, keepdims=True))
                a_inter = jnp.exp(inter - m_t)
                q_h = q_ref[0, rs, p * LANE:(p + 1) * LANE] * half_mask[half]
                kt_pair = kt_ref[0, p * LANE:(p + 1) * LANE, rs]
                kt_h = kt_ref[0, h * ML_DQK:(h + 1) * ML_DQK, rs]
                v_ext = jnp.concatenate([v_ref[0, rs, h * ML_DV:(h + 1) * ML_DV], e0], axis=1)
                c_pair = c_ref[d, p]
                s = _dot(q_h, kt_pair) * jnp.exp(dm - m_t)
                tot = a_inter * _dot(q_h, c_pair.astype(BF16)) + _dot(s.astype(BF16), v_ext)
                den = tot[:, ML_DV:ML_DV + 1]
                h_ref[0, rs, h * ML_DV:(h + 1) * ML_DV] = tot[:, 0:ML_DV] / jnp.maximum(jnp.abs(den), jnp.exp(-m_t))
                kw = (kt_h.astype(F32) * wk).astype(BF16)
                hs = slice(half * ML_DQK, (half + 1) * ML_DQK)
                c_ref[d, p, hs, :] = a_state * c_pair[hs, :] + _dot(kw, v_ext)
                m_ref[idx] = jnp.broadcast_to(m_new, (8, LANE))


def _mlstm(q, kt, v, gcol, grow, brow, bcol, n_lat_tiles):
    B, S, _ = q.shape
    nblk = S // TM
    fwd = lambda j: jnp.where(j == 0, n_lat_tiles, j - 1)
    bwd = lambda j: jnp.where(j == 0, n_lat_tiles, n_lat_tiles - j)

    def specs(order):
        return [pl.BlockSpec((1, TM, 256), lambda b, j: (b, order(j), 0)),
                pl.BlockSpec((1, 256, TM), lambda b, j: (b, 0, order(j))),
                pl.BlockSpec((1, TM, 512), lambda b, j: (b, order(j), 0)),
                pl.BlockSpec((1, TM, LANE), lambda b, j: (b, order(j), 0)),
                pl.BlockSpec((1, 16, TM), lambda b, j: (b, 0, order(j)))]

    return pl.pallas_call(
        _mlstm_kernel,
        out_shape=[jax.ShapeDtypeStruct((B, S, 512), F32)] * 2,
        grid=(B, nblk),
        in_specs=specs(fwd) + specs(bwd) + [_const_spec(brow.shape), _const_spec(bcol.shape)],
        out_specs=[pl.BlockSpec((1, TM, 512), lambda b, j: (b, fwd(j), 0)),
                   pl.BlockSpec((1, TM, 512), lambda b, j: (b, bwd(j), 0))],
        scratch_shapes=[pltpu.VMEM((2, ML_HEADS // 2, 2 * ML_DQK, 2 * ML_DV), F32),
                        pltpu.VMEM((2 * ML_HEADS, 8, LANE), F32)],
        compiler_params=_params("arbitrary", "arbitrary"), name="mlstm",
    )(q, kt, v, gcol, grow, q, kt, v, gcol, grow, brow, bcol)


def _softmax_pv_t(parts):
    ss = []
    for k, q, _, bias in parts:
        s = _dot_nt(k, q)
        ss.append(s if bias is None else s + bias)
    m = functools.reduce(jnp.maximum, [jnp.max(s, axis=0, keepdims=True) for s in ss])
    ps = [jnp.exp2(s - m) for s in ss]
    l = sum(jnp.sum(p, axis=0, keepdims=True) for p in ps)
    o = sum(_dot(vt, p.astype(BF16)) for (_, _, vt, _), p in zip(parts, ps))
    return o / l


def _mla_kernel(q_ref, k_ref, vt_ref, o_ref, s_scr, p_scr, acc_scr, *, chunks):
    n_heads, tq = q_ref.shape[1], q_ref.shape[2]
    items = [(hh, ci) for hh in range(n_heads) for ci in range(len(chunks))]

    def scores(n):
        hh, ci = items[n]
        off, size = chunks[ci]
        s_scr[n % 2, 0:size, :] = _dot_nt(k_ref[0, hh, off:off + size, :], q_ref[0, hh])

    scores(0)
    m = l = None
    for n, (hh, ci) in enumerate(items):
        off, size = chunks[ci]
        slot = n % 2
        if n + 1 < len(items):
            scores(n + 1)
        if ci == 0:
            m = [jnp.full((1, LANE), -jnp.inf, F32)] * (tq // LANE)
            l = [jnp.zeros((1, LANE), F32)] * (tq // LANE)
        alphas = []
        for st in range(tq // LANE):
            cs = slice(st * LANE, (st + 1) * LANE)
            m_new = jnp.maximum(m[st], jnp.max(s_scr[slot, 0:size, cs], axis=0, keepdims=True))
            alpha = jnp.exp2(m[st] - m_new)
            p = jnp.exp2(s_scr[slot, 0:size, cs] - m_new)
            l[st] = alpha * l[st] + jnp.sum(p, axis=0, keepdims=True)
            m[st] = m_new
            p_scr[slot, 0:size, cs] = p.astype(BF16)
            alphas.append(alpha)
        pv = _dot(vt_ref[0, hh * MLA_V:(hh + 1) * MLA_V, off:off + size], p_scr[slot, 0:size, :])
        if ci == 0:
            acc_scr[hh] = pv
        else:
            acc_scr[hh] = jnp.concatenate(alphas, axis=1) * acc_scr[hh] + pv
        if ci == len(chunks) - 1:
            o_ref[0, hh * MLA_V:(hh + 1) * MLA_V, :] = (acc_scr[hh] / jnp.concatenate(l, axis=1)).astype(BF16)


def _mla(qm, km, vmt, n_lat):
    B, H, S, _ = qm.shape
    kv, hp = 512, 2
    chunks = tuple((i * kv, kv) for i in range(n_lat // kv)) + ((n_lat, S - n_lat),)
    return pl.pallas_call(
        functools.partial(_mla_kernel, chunks=chunks),
        out_shape=jax.ShapeDtypeStruct((B, H * MLA_V, S), BF16),
        grid=(B, H // hp, n_lat // MLA_TQ),
        in_specs=[pl.BlockSpec((1, hp, MLA_TQ, LANE), lambda b, h, i: (b, h, i, 0)),
                  pl.BlockSpec((1, hp, S, LANE), lambda b, h, i: (b, h, 0, 0)),
                  pl.BlockSpec((1, hp * MLA_V, S), lambda b, h, i: (b, h, 0))],
        out_specs=pl.BlockSpec((1, hp * MLA_V, MLA_TQ), lambda b, h, i: (b, h, i)),
        scratch_shapes=[pltpu.VMEM((2, kv, MLA_TQ), F32), pltpu.VMEM((2, kv, MLA_TQ), BF16),
                        pltpu.VMEM((hp, MLA_V, MLA_TQ), F32)],
        compiler_params=_params("arbitrary", "arbitrary", "arbitrary"), name="mla_attn",
    )(qm, km, vmt)


def _natten_kernel(q_ref, k_ref, vt_ref, bias_ref, o_ref, s_scr, p_scr, *, n_lat, n_ctx, rows):
    blk = pl.program_id(1)
    band = NA_BAND_ROWS * GRID_W
    nk = band + n_ctx
    tq = q_ref.shape[2]
    u0 = jnp.clip(blk * NA_Q_ROWS - NA_WIN_R // 2, 0, rows - NA_BAND_ROWS)
    koff = pl.multiple_of(u0 * GRID_W, 2 * LANE)
    lane = lax.broadcasted_iota(jnp.int32, (1, LANE), 1)
    half_mask = [(lane < NA_DH).astype(F32).astype(BF16), (lane >= NA_DH).astype(F32).astype(BF16)]

    def scores(h):
        qh = q_ref[0, h // 2] * half_mask[h % 2]
        s_scr[h % 2, 0:band, :] = _dot_nt(k_ref[0, h // 2, pl.ds(koff, band), :], qh) + bias_ref[h, 0]
        s_scr[h % 2, band:nk, :] = _dot_nt(k_ref[0, h // 2, n_lat:n_lat + n_ctx, :], qh)

    scores(0)
    for h in range(NA_HEADS):
        slot = h % 2
        if h + 1 < NA_HEADS:
            scores(h + 1)
        ls = []
        for st in range(tq // LANE):
            cs = slice(st * LANE, (st + 1) * LANE)
            m = jnp.max(s_scr[slot, :, cs], axis=0, keepdims=True)
            p = jnp.exp2(s_scr[slot, :, cs] - m)
            ls.append(jnp.sum(p, axis=0, keepdims=True))
            p_scr[slot, :, cs] = p.astype(BF16)
        hs = slice(h * NA_DH, (h + 1) * NA_DH)
        o = (_dot(vt_ref[0, hs, pl.ds(koff, band)], p_scr[slot, 0:band, :])
             + _dot(vt_ref[0, hs, n_lat:n_lat + n_ctx], p_scr[slot, band:nk, :]))
        o_ref[0, hs, :] = (o / jnp.concatenate(ls, axis=1)).astype(BF16)


def _natten(qn, kn, vnt, bias, n_lat):
    B, P, S, _ = qn.shape
    rows = n_lat // GRID_W
    nblk = rows // NA_Q_ROWS
    tq = NA_Q_ROWS * GRID_W
    band = NA_BAND_ROWS * GRID_W
    nk = band + S - n_lat
    btype = lambda i: jnp.where(i == 0, 0, jnp.where(i == nblk - 1, 2, 1))
    return pl.pallas_call(
        functools.partial(_natten_kernel, n_lat=n_lat, n_ctx=S - n_lat, rows=rows),
        out_shape=jax.ShapeDtypeStruct((B, NA_HEADS * NA_DH, S), BF16),
        grid=(B, nblk),
        in_specs=[pl.BlockSpec((1, P, tq, LANE), lambda b, i: (b, 0, i, 0)),
                  pl.BlockSpec((1, P, S, LANE), lambda b, i: (b, 0, 0, 0)),
                  pl.BlockSpec((1, NA_HEADS * NA_DH, S), lambda b, i: (b, 0, 0)),
                  pl.BlockSpec((NA_HEADS, 1, band, tq), lambda b, i: (0, btype(i), 0, 0))],
        out_specs=pl.BlockSpec((1, NA_HEADS * NA_DH, tq), lambda b, i: (b, 0, i)),
        scratch_shapes=[pltpu.VMEM((2, nk, tq), F32), pltpu.VMEM((2, nk, tq), BF16)],
        compiler_params=_params("arbitrary", "arbitrary"), name="natten",
    )(qn, kn, vnt, bias)


def _na_bias_kernel(rpb_ref, o_ref, bc_ref, *, rows):
    l, h = pl.program_id(0), pl.program_id(1)
    tq = NA_Q_ROWS * GRID_W
    kc = lax.broadcasted_iota(jnp.int32, (GRID_W, tq), 0)
    ql = lax.broadcasted_iota(jnp.int32, (GRID_W, tq), 1)
    c = ql & (GRID_W - 1)
    qi = ql >> 6
    dc = kc - c + (NA_WIN_C - 1)
    cs = jnp.clip(c - NA_WIN_C // 2, 0, GRID_W - NA_WIN_C)
    col_ok = (kc >= cs) & (kc < cs + NA_WIN_C)
    n_dr, n_dc = 2 * NA_WIN_R - 1, 2 * NA_WIN_C - 1
    base = (l * NA_HEADS + h) * (n_dr * n_dc)
    for dr in range(n_dr):
        acc = jnp.zeros((GRID_W, tq), F32)
        for j in range(n_dc):
            acc = jnp.where(dc == j, rpb_ref[base + dr * n_dc + j], acc)
        bc_ref[dr] = jnp.where(col_ok, acc * LOG2E, NEG_INF)
    nblk = rows // NA_Q_ROWS
    for t, blk in enumerate((0, 1, nblk - 1)):
        r0 = blk * NA_Q_ROWS
        u0 = min(max(r0 - NA_WIN_R // 2, 0), rows - NA_BAND_ROWS)
        for j in range(NA_BAND_ROWS):
            tile = jnp.full((GRID_W, tq), NEG_INF, F32)
            for i in range(NA_Q_ROWS):
                r, kr = r0 + i, u0 + j
                rs = min(max(r - NA_WIN_R // 2, 0), rows - NA_WIN_R)
                if rs <= kr < rs + NA_WIN_R:
                    tile = jnp.where(qi == i, bc_ref[kr - r + NA_WIN_R - 1], tile)
            o_ref[0, 0, t, j * GRID_W:(j + 1) * GRID_W, :] = tile


def _na_bias(rpb, rows):
    L = rpb.shape[0]
    tq = NA_Q_ROWS * GRID_W
    band = NA_BAND_ROWS * GRID_W
    return pl.pallas_call(
        functools.partial(_na_bias_kernel, rows=rows),
        out_shape=jax.ShapeDtypeStruct((L, NA_HEADS, 3, band, tq), F32),
        grid=(L, NA_HEADS),
        in_specs=[pl.BlockSpec(memory_space=pltpu.SMEM)],
        out_specs=pl.BlockSpec((1, 1, 3, band, tq), lambda l, h: (l, h, 0, 0, 0)),
        scratch_shapes=[pltpu.VMEM((2 * NA_WIN_R - 1, GRID_W, tq), F32)],
        compiler_params=_params("arbitrary", "arbitrary"), name="na_bias",
    )(rpb.reshape(-1))


def _ctx_attn_kernel(mo_in, no_in, qm_ref, km_ref, vmt_ref, qn_ref, kn_ref, vnt_ref, mo_ref, no_ref):
    del mo_in, no_in
    lane = lax.broadcasted_iota(jnp.int32, (1, LANE), 1)
    for hh in range(2):
        hs = slice(hh * MLA_V, (hh + 1) * MLA_V)
        mo_ref[0, hs, :] = _softmax_pv_t([(km_ref[0, hh], qm_ref[0, hh], vmt_ref[0, hs, :], None)]).astype(BF16)
        qh = qn_ref[0, 0] * ((lane < NA_DH) if hh == 0 else (lane >= NA_DH)).astype(F32).astype(BF16)
        no_ref[0, hs, :] = _softmax_pv_t([(kn_ref[0, 0], qh, vnt_ref[0, hs, :], None)]).astype(BF16)


def _ctx_attn(mla_o, na_o, qm, km, vmt, qn, kn, vnt, n_lat):
    B, _, S, _ = qm.shape
    n_ctx = S - n_lat
    t = n_lat // n_ctx
    head = lambda b, p: (b, p, t, 0)
    rowsT = lambda b, p: (b, p, t)
    any_spec = pl.BlockSpec(memory_space=pl.ANY)
    return pl.pallas_call(
        _ctx_attn_kernel,
        out_shape=[jax.ShapeDtypeStruct(mla_o.shape, BF16), jax.ShapeDtypeStruct(na_o.shape, BF16)],
        grid=(B, NA_HEADS // 2),
        in_specs=[any_spec, any_spec,
                  pl.BlockSpec((1, 2, n_ctx, LANE), head), pl.BlockSpec((1, 2, n_ctx, LANE), head),
                  pl.BlockSpec((1, 2 * MLA_V, n_ctx), rowsT),
                  pl.BlockSpec((1, 1, n_ctx, LANE), head), pl.BlockSpec((1, 1, n_ctx, LANE), head),
                  pl.BlockSpec((1, 2 * NA_DH, n_ctx), rowsT)],
        out_specs=[pl.BlockSpec((1, 2 * MLA_V, n_ctx), rowsT), pl.BlockSpec((1, 2 * NA_DH, n_ctx), rowsT)],
        input_output_aliases={0: 0, 1: 1},
        compiler_params=_params("arbitrary", "arbitrary"), name="ctx_attn",
    )(mla_o, na_o, qm, km, vmt, qn, kn, vnt)


def _merge_kernel(x_ref, mod_ref, g1_ref, hf_ref, hb_ref, o_ref, mo_ref, no_ref, wg_ref, gout_ref,
                  wml_ref, wmla_ref, wna_ref, wout_ref, x1_ref):
    x = x_ref[0]
    m = mod_ref[0]
    hb = (_rms(x, g1_ref[...], D_MODEL) * (1.0 + m[1:2]) + m[0:1]).astype(BF16)
    gates = jax.nn.sigmoid(_dot(hb, wg_ref[...]))
    hs = hf_ref[0] + hb_ref[0]
    og = jax.nn.sigmoid(o_ref[0])
    gout = gout_ref[...]
    hn = jnp.concatenate(
        [_rms(hs[:, h * ML_DV:(h + 1) * ML_DV], gout[:, h * ML_DV:(h + 1) * ML_DV], ML_DV) for h in range(ML_HEADS)],
        axis=1)
    y_a = _dot((hn * og).astype(BF16), wml_ref[...])
    y_b = _dot_tn(mo_ref[0], wmla_ref[...])
    y_c = _dot_tn(no_ref[0], wna_ref[...])
    mg = gates[:, 0:D_MODEL] * y_a + gates[:, D_MODEL:2 * D_MODEL] * y_b + gates[:, 2 * D_MODEL:] * y_c
    x1_ref[0] = x + m[2:3] * _dot(mg.astype(BF16), wout_ref[...])


def _merge(x, mods, g1, hf, hb, o, mla_o, na_o, w, nt, n_lat_tiles):
    B = x.shape[0]
    tok = lambda b, t: (b, t, 0)
    tokT = lambda b, t: (b, 0, t)
    in_specs = [pl.BlockSpec((1, TM, D_MODEL), tok),
                pl.BlockSpec((1, 8, D_MODEL), lambda b, t: (jnp.where(t >= n_lat_tiles, B, b), 0, 0)),
                _const_spec(g1.shape),
                pl.BlockSpec((1, TM, 512), tok), pl.BlockSpec((1, TM, 512), tok), pl.BlockSpec((1, TM, 512), tok),
                pl.BlockSpec((1, 512, TM), tokT), pl.BlockSpec((1, 512, TM), tokT)]
    names = ("wg", "gout", "wml", "wmla", "wna", "wout")
    in_specs += [_const_spec(w[k].shape) for k in names]
    return pl.pallas_call(
        _merge_kernel,
        out_shape=jax.ShapeDtypeStruct((B, nt * TM, D_MODEL), F32),
        grid=(B, nt), in_specs=in_specs, out_specs=pl.BlockSpec((1, TM, D_MODEL), tok),
        compiler_params=_params("arbitrary", "arbitrary"), name="merge",
    )(x, mods, g1, hf, hb, o, mla_o, na_o, *[w[k] for k in names])


def _mlp_kernel(x_ref, mod_ref, g2_ref, w1_ref, w2_ref, o_ref):
    x = x_ref[0]
    m = mod_ref[0]
    hb = (_rms(x, g2_ref[...], D_MODEL) * (1.0 + m[4:5]) + m[3:4]).astype(BF16)
    acc = jnp.zeros((TM, D_MODEL), F32)
    fc = 1024
    for c in range(D_FF // fc):
        u = jnp.maximum(_dot(hb, w1_ref[:, c * fc:(c + 1) * fc]), 0.0)
        acc = acc + _dot((u * u).astype(BF16), w2_ref[c * fc:(c + 1) * fc, :])
    o_ref[0] = x + m[5:6] * acc


def _mlp(x1, mods, g2, w1, w2, nt, n_lat_tiles):
    B = x1.shape[0]
    tok = lambda b, t: (b, t, 0)
    return pl.pallas_call(
        _mlp_kernel,
        out_shape=jax.ShapeDtypeStruct((B, nt * TM, D_MODEL), F32),
        grid=(B, nt),
        in_specs=[pl.BlockSpec((1, TM, D_MODEL), tok),
                  pl.BlockSpec((1, 8, D_MODEL), lambda b, t: (jnp.where(t >= n_lat_tiles, B, b), 0, 0)),
                  _const_spec(g2.shape), _const_spec(w1.shape), _const_spec(w2.shape)],
        out_specs=pl.BlockSpec((1, TM, D_MODEL), tok),
        compiler_params=_params("arbitrary", "arbitrary"), name="mlp",
    )(x1, mods, g2, w1, w2)


def _rope_tables(n_lat, n_ctx):
    nf = MLA_ROPE // 4
    inv = jnp.power(ROPE_THETA, -jnp.arange(nf, dtype=F32) / nf)
    pos = jnp.arange(n_lat)
    ang_r = (pos // GRID_W).astype(F32)[:, None] * inv
    ang_c = (pos % GRID_W).astype(F32)[:, None] * inv
    z = jnp.zeros((n_lat, nf), F32)
    one = lambda n: jnp.ones((n_lat, n), F32)
    zero = lambda n: jnp.zeros((n_lat, n), F32)
    cr, sr, cc, sc = jnp.cos(ang_r), jnp.sin(ang_r), jnp.cos(ang_c), jnp.sin(ang_c)
    cos = jnp.concatenate([one(MLA_NOPE), cr, cr, cc, cc, one(LANE - MLA_DQK)], axis=1)
    sina = jnp.concatenate([zero(MLA_NOPE), z, sr, z, sc, zero(LANE - MLA_DQK)], axis=1)
    sinb = jnp.concatenate([zero(MLA_NOPE), -sr, z, -sc, z, zero(LANE - MLA_DQK)], axis=1)
    ident = (jnp.ones((n_ctx, LANE), F32), jnp.zeros((n_ctx, LANE), F32), jnp.zeros((n_ctx, LANE), F32))
    return tuple(jnp.concatenate([t, i], axis=0) for t, i in zip((cos, sina, sinb), ident))


def _layer_weights(l, w_in, ml_i_bias, ml_f_bias, ml_g_out, ml_w_o, mla_g_cq, mla_w_uq, mla_g_ckv, mla_w_ukv,
                   mla_g_q, mla_g_k, mla_w_o, na_g_q, na_g_k, na_w_o, w_out, w_ff1, w_ff2):
    offs = np.cumsum(IN_SIZES)[:-1].tolist()
    mlq, mlk, mlv, mlo, mlg, dq, dkv, kr, naq, nak, nav, gates = jnp.split(w_in[l], offs, axis=-1)
    padc = lambda a, n: jnp.pad(a, ((0, 0), (0, n - a.shape[1])))
    kr_slab = jnp.pad(kr, ((0, 0), (MLA_NOPE, LANE - MLA_DQK)))
    w_uq = jnp.pad(mla_w_uq[l].reshape(MLA_Q_RANK, MLA_HEADS, MLA_DQK), ((0, 0), (0, 0), (0, LANE - MLA_DQK)))
    w_ukv = mla_w_ukv[l].reshape(MLA_KV_RANK, MLA_HEADS, MLA_NOPE + MLA_V)
    w_uk = jnp.pad(w_ukv[..., :MLA_NOPE], ((0, 0), (0, 0), (0, LANE - MLA_NOPE)))
    w_uv = w_ukv[..., MLA_NOPE:].reshape(MLA_KV_RANK, MLA_HEADS * MLA_V)
    row = lambda a: a.reshape(1, -1).astype(F32)
    bias16 = jnp.stack([ml_i_bias[l, 0], ml_f_bias[l, 0], ml_i_bias[l, 1], ml_f_bias[l, 1]]).reshape(-1).astype(F32)
    return {
        "wa": jnp.concatenate([mlq * (ML_DQK ** -0.5), mlv, mlo, padc(mlg, LANE)], axis=1).astype(BF16),
        "wat": jnp.concatenate([mlk.T, mlg.T], axis=0).astype(BF16),
        "wb": jnp.concatenate([dq, dkv, kr_slab], axis=1).astype(BF16),
        "wc": jnp.concatenate([naq, nak], axis=1).astype(BF16),
        "wct": nav.T.astype(BF16),
        "wuq": w_uq.reshape(MLA_Q_RANK, MLA_HEADS * LANE).astype(BF16),
        "wuk": w_uk.reshape(MLA_KV_RANK, MLA_HEADS * LANE).astype(BF16),
        "wuvt": w_uv.T.astype(BF16),
        "gcq": row(mla_g_cq[l]), "gckv": row(mla_g_ckv[l]),
        "gq": padc(row(mla_g_q[l]) * (MLA_DQK ** -0.5 * LOG2E), LANE), "gk": padc(row(mla_g_k[l]), LANE),
        "ngq": row(jnp.tile(na_g_q[l], 2)) * (NA_DH ** -0.5 * LOG2E), "ngk": row(jnp.tile(na_g_k[l], 2)),
        "brow": padc(bias16.reshape(1, 16), LANE), "bcol": bias16.reshape(16, 1),
        "wg": gates.astype(BF16), "gout": row(ml_g_out[l]),
        "wml": ml_w_o[l].astype(BF16), "wmla": mla_w_o[l].astype(BF16), "wna": na_w_o[l].astype(BF16),
        "wout": w_out[l].astype(BF16), "w1": w_ff1[l].astype(BF16), "w2": w_ff2[l].astype(BF16),
    }


def kernel(x, c, ctx, c_ctx, w_mod, b_mod, g_norm1, g_norm2, w_in, ml_i_bias, ml_f_bias, ml_g_out, ml_w_o, mla_g_cq, mla_w_uq, mla_g_ckv, mla_w_ukv, mla_g_q, mla_g_k, mla_w_o, na_g_q, na_g_k, na_rpb, na_w_o, w_out, w_ff1, w_ff2):
    B, T, D = x.shape
    C = ctx.shape[1]
    depth = w_in.shape[0]
    assert D == D_MODEL and C == TM and T % MLA_TQ == 0 and B < 16
    rows = T // GRID_W
    assert rows % NA_Q_ROWS == 0 and rows >= NA_BAND_ROWS
    n_lat_tiles = T // TM

    cc = jnp.zeros((16, D), F32).at[:B].set(c).at[B].set(c_ctx)
    mod = _modulation(cc, w_mod, b_mod).reshape(depth, 16, 6, D)
    mod = jnp.pad(mod, ((0, 0), (0, 0), (0, 2), (0, 0)))
    tabs = _rope_tables(T, C)
    na_bias = _na_bias(na_rpb.astype(F32), rows)
    xs = jnp.concatenate([x, ctx], axis=1)

    for l in range(depth):
        last = l == depth - 1
        w = _layer_weights(l, w_in, ml_i_bias, ml_f_bias, ml_g_out, ml_w_o, mla_g_cq, mla_w_uq, mla_g_ckv,
                           mla_w_ukv, mla_g_q, mla_g_k, mla_w_o, na_g_q, na_g_k, na_w_o, w_out, w_ff1, w_ff2)
        g1 = g_norm1[l].reshape(1, D).astype(F32)
        g2 = g_norm2[l].reshape(1, D).astype(F32)
        q, kt, v, o, gcol, grow, qm, km, vmt, qn, kn, vnt = _inproj(xs, mod[l], g1, w, tabs, n_lat_tiles)
        hf, hb = _mlstm(q, kt, v, gcol, grow, w["brow"], w["bcol"], n_lat_tiles)
        mla_o = _mla(qm, km, vmt, T)
        na_o = _natten(qn, kn, vnt, na_bias[l], T)
        if not last:
            mla_o, na_o = _ctx_attn(mla_o, na_o, qm, km, vmt, qn, kn, vnt, T)
        nt = n_lat_tiles if last else n_lat_tiles + 1
        x1 = _merge(xs, mod[l], g1, hf, hb, o, mla_o, na_o, w, nt, n_lat_tiles)
        xs = _mlp(x1, mod[l], g2, w["w1"], w["w2"], nt, n_lat_tiles)
    return xs
```

```python
import functools

import numpy as np
import jax
import jax.numpy as jnp
from jax import lax
from jax.experimental import pallas as pl
from jax.experimental.pallas import tpu as pltpu

F32 = jnp.float32
BF16 = jnp.bfloat16

D_MODEL = 1024
GRID_W = 64
ML_HEADS, ML_DQK, ML_DV = 4, 64, 128
ML_CHUNK = 128
MLA_HEADS, MLA_Q_RANK, MLA_KV_RANK, MLA_NOPE, MLA_ROPE, MLA_V = 8, 384, 256, 64, 32, 64
MLA_DQK = MLA_NOPE + MLA_ROPE
NA_HEADS, NA_DH, NA_WIN_R, NA_WIN_C = 8, 64, 8, 16
NA_Q_ROWS = 4
NA_BAND_ROWS = NA_Q_ROWS + NA_WIN_R
D_FF = 4 * D_MODEL
ROPE_THETA = 10000.0
EPS = 1e-6
LOG2E = 1.4426950408889634
SCORE_BOUND_LIMIT = 60.0
NEG_INF = -1e30
IN_SIZES = (256, 256, 512, 512, 16, MLA_Q_RANK, MLA_KV_RANK, MLA_ROPE, 512, 512, 512, 3 * D_MODEL)
LANE = 128
TM = 256
MLA_TQ = 512
VMEM_LIMIT = 56 * 1024 * 1024


def _dot(a, b):
    return jnp.dot(a, b, preferred_element_type=F32)


def _dot_nt(a, b):
    return lax.dot_general(a, b, (((1,), (1,)), ((), ())), preferred_element_type=F32)


def _dot_tn(a, b):
    return lax.dot_general(a, b, (((0,), (0,)), ((), ())), preferred_element_type=F32)


def _rms(x, g, n):
    ms = jnp.sum(x * x, axis=-1, keepdims=True) * (1.0 / n)
    return x * lax.rsqrt(ms + EPS) * g


def _split3(x):
    hi = x.astype(BF16)
    r = x - hi.astype(F32)
    mid = r.astype(BF16)
    lo = (r - mid.astype(F32)).astype(BF16)
    return hi, mid, lo


def _params(*sem):
    return pltpu.CompilerParams(dimension_semantics=sem, vmem_limit_bytes=VMEM_LIMIT)


def _const_spec(shape):
    nd = len(shape)
    return pl.BlockSpec(shape, lambda *_: (0,) * nd)


def _mod_kernel(c_ref, w_ref, b_ref, o_ref):
    c = c_ref[...]
    s = c * jax.nn.sigmoid(c)
    o_ref[0] = _dot(s, w_ref[0]) + b_ref[0]


def _modulation(cc, w_mod, b_mod):
    L = w_mod.shape[0]
    tn = 1024
    return pl.pallas_call(
        _mod_kernel,
        out_shape=jax.ShapeDtypeStruct((L, 16, 6 * D_MODEL), F32),
        grid=(L, 6 * D_MODEL // tn),
        in_specs=[pl.BlockSpec((16, D_MODEL), lambda l, n: (0, 0)),
                  pl.BlockSpec((1, D_MODEL, tn), lambda l, n: (l, 0, n)),
                  pl.BlockSpec((1, 1, tn), lambda l, n: (l, 0, n))],
        out_specs=pl.BlockSpec((1, 16, tn), lambda l, n: (l, 0, n)),
        compiler_params=_params("arbitrary", "arbitrary"),
        name="modulation",
    )(cc, w_mod, b_mod.reshape(L, 1, 6 * D_MODEL))


def _rope(x, cos, sina, sinb):
    return x * cos + pltpu.roll(x, 8, 1) * sina + pltpu.roll(x, LANE - 8, 1) * sinb


def _inproj_kernel(x_ref, mod_ref, g1_ref, wa_ref, wat_ref, wb_ref, wc_ref, wct_ref, wuq_ref, wuk_ref, wuvt_ref,
                   gcq_ref, gckv_ref, gq_ref, gk_ref, ngq_ref, ngk_ref, cos_ref, sina_ref, sinb_ref,
                   q_ref, kt_ref, v_ref, o_ref, gcol_ref, grow_ref, qm_ref, km_ref, vmt_ref, qn_ref, kn_ref, vnt_ref):
    x = x_ref[0]
    m = mod_ref[0]
    hb = (_rms(x, g1_ref[...], D_MODEL) * (1.0 + m[1:2]) + m[0:1]).astype(BF16)

    a = _dot(hb, wa_ref[...])
    q_ref[0] = a[:, 0:256].astype(BF16)
    v_ref[0] = a[:, 256:768].astype(BF16)
    o_ref[0] = a[:, 768:1280]
    gcol_ref[0] = a[:, 1280:1408]
    at = _dot_nt(wat_ref[...], hb)
    kt_ref[0] = at[0:256].astype(BF16)
    grow_ref[0] = at[256:272]

    bs = _dot(hb, wb_ref[...])
    dqn = _rms(bs[:, 0:MLA_Q_RANK], gcq_ref[...], MLA_Q_RANK).astype(BF16)
    dkvn = _rms(bs[:, MLA_Q_RANK:MLA_Q_RANK + MLA_KV_RANK], gckv_ref[...], MLA_KV_RANK).astype(BF16)
    kr = bs[:, MLA_Q_RANK + MLA_KV_RANK:]
    qh_all = _dot(dqn, wuq_ref[...])
    kh_all = _dot(dkvn, wuk_ref[...])
    cos, sina, sinb = cos_ref[...], sina_ref[...], sinb_ref[...]
    for h in range(MLA_HEADS):
        sl = slice(h * LANE, (h + 1) * LANE)
        qm_ref[0, h] = _rope(_rms(qh_all[:, sl], gq_ref[...], MLA_DQK), cos, sina, sinb).astype(BF16)
        km_ref[0, h] = _rope(_rms(kh_all[:, sl] + kr, gk_ref[...], MLA_DQK), cos, sina, sinb).astype(BF16)
    vmt_ref[0] = _dot_nt(wuvt_ref[...], dkvn).astype(BF16)

    c = _dot(hb, wc_ref[...])
    lo = lax.broadcasted_iota(jnp.int32, (1, LANE), 1) < NA_DH
    for p in range(NA_HEADS // 2):
        for base, g_ref, out_ref in ((0, ngq_ref, qn_ref), (NA_HEADS * NA_DH, ngk_ref, kn_ref)):
            s = c[:, base + p * LANE: base + (p + 1) * LANE]
            sq = s * s
            s_lo = jnp.sum(jnp.where(lo, sq, 0.0), axis=-1, keepdims=True)
            s_hi = jnp.sum(jnp.where(lo, 0.0, sq), axis=-1, keepdims=True)
            ms = jnp.where(lo, s_lo, s_hi) * (1.0 / NA_DH)
            out_ref[0, p] = (s * lax.rsqrt(ms + EPS) * g_ref[...]).astype(BF16)
    vnt_ref[0] = _dot_nt(wct_ref[...], hb).astype(BF16)


def _inproj(x, mods, g1, w, tabs, n_lat_tiles):
    B, S, _ = x.shape
    nt = S // TM
    tok = lambda b, t: (b, t, 0)
    tokT = lambda b, t: (b, 0, t)
    head = lambda b, t: (b, 0, t, 0)
    in_specs = [pl.BlockSpec((1, TM, D_MODEL), tok),
                pl.BlockSpec((1, 8, D_MODEL), lambda b, t: (jnp.where(t >= n_lat_tiles, B, b), 0, 0)),
                _const_spec(g1.shape)]
    in_specs += [_const_spec(w[k].shape) for k in ("wa", "wat", "wb", "wc", "wct", "wuq", "wuk", "wuvt",
                                                   "gcq", "gckv", "gq", "gk", "ngq", "ngk")]
    in_specs += [pl.BlockSpec((TM, LANE), lambda b, t: (t, 0))] * 3
    out_shape = [jax.ShapeDtypeStruct((B, S, 256), BF16),
                 jax.ShapeDtypeStruct((B, 256, S), BF16),
                 jax.ShapeDtypeStruct((B, S, 512), BF16),
                 jax.ShapeDtypeStruct((B, S, 512), F32),
                 jax.ShapeDtypeStruct((B, S, LANE), F32),
                 jax.ShapeDtypeStruct((B, 16, S), F32),
                 jax.ShapeDtypeStruct((B, MLA_HEADS, S, LANE), BF16),
                 jax.ShapeDtypeStruct((B, MLA_HEADS, S, LANE), BF16),
                 jax.ShapeDtypeStruct((B, MLA_HEADS * MLA_V, S), BF16),
                 jax.ShapeDtypeStruct((B, NA_HEADS // 2, S, LANE), BF16),
                 jax.ShapeDtypeStruct((B, NA_HEADS // 2, S, LANE), BF16),
                 jax.ShapeDtypeStruct((B, NA_HEADS * NA_DH, S), BF16)]
    out_specs = [pl.BlockSpec((1, TM, 256), tok), pl.BlockSpec((1, 256, TM), tokT),
                 pl.BlockSpec((1, TM, 512), tok), pl.BlockSpec((1, TM, 512), tok),
                 pl.BlockSpec((1, TM, LANE), tok), pl.BlockSpec((1, 16, TM), tokT),
                 pl.BlockSpec((1, MLA_HEADS, TM, LANE), head), pl.BlockSpec((1, MLA_HEADS, TM, LANE), head),
                 pl.BlockSpec((1, MLA_HEADS * MLA_V, TM), tokT),
                 pl.BlockSpec((1, NA_HEADS // 2, TM, LANE), head), pl.BlockSpec((1, NA_HEADS // 2, TM, LANE), head),
                 pl.BlockSpec((1, NA_HEADS * NA_DH, TM), tokT)]
    return pl.pallas_call(
        _inproj_kernel, out_shape=out_shape, grid=(B, nt), in_specs=in_specs, out_specs=out_specs,
        compiler_params=_params("arbitrary", "arbitrary"), name="inproj",
    )(x, mods, g1, w["wa"], w["wat"], w["wb"], w["wc"], w["wct"], w["wuq"], w["wuk"], w["wuvt"],
      w["gcq"], w["gckv"], w["gq"], w["gk"], w["ngq"], w["ngk"], *tabs)


def _log_sigmoid(x):
    return jnp.minimum(x, 0.0) - jnp.log1p(jnp.exp(-jnp.abs(x)))


def _mlstm_kernel(qf_ref, ktf_ref, vf_ref, gcf_ref, grf_ref, qb_ref, ktb_ref, vb_ref, gcb_ref, grb_ref,
                  brow_ref, bcol_ref, hf_ref, hb_ref, c_ref, m_ref):
    L = ML_CHUNK

    @pl.when(pl.program_id(1) == 0)
    def _():
        c_ref[...] = jnp.zeros_like(c_ref)
        m_ref[...] = jnp.zeros_like(m_ref)

    row = lax.broadcasted_iota(jnp.int32, (L, L), 0)
    col = lax.broadcasted_iota(jnp.int32, (L, L), 1)
    lane = lax.broadcasted_iota(jnp.int32, (1, LANE), 1)
    isf_row = (lane & ML_HEADS) != 0
    sub = lax.broadcasted_iota(jnp.int32, (16, 1), 0)
    isf_col = (sub & ML_HEADS) != 0
    half_mask = [(lane < ML_DQK).astype(F32).astype(BF16), (lane >= ML_DQK).astype(F32).astype(BF16)]
    e0 = jnp.broadcast_to((lane == 0).astype(F32).astype(BF16), (L, LANE))
    dirs = ((qf_ref, ktf_ref, vf_ref, gcf_ref, grf_ref, hf_ref), (qb_ref, ktb_ref, vb_ref, gcb_ref, grb_ref, hb_ref))
    for d, (q_ref, kt_ref, v_ref, gc_ref, gr_ref, h_ref) in enumerate(dirs):
        causal = (col <= row) if d == 0 else (col >= row)
        tri = causal.astype(F32).astype(BF16)
        tri_t = ((row <= col) if d == 0 else (row >= col)).astype(F32).astype(BF16)
        local = {}
        for ch in range(TM // L):
            rs = slice(ch * L, (ch + 1) * L)
            pre_c = gc_ref[0, rs, :] + brow_ref[...]
            val_c = jnp.where(isf_row, _log_sigmoid(pre_c), pre_c)
            pre_r = gr_ref[0, :, rs] + bcol_ref[...]
            val_r = jnp.where(isf_col, _log_sigmoid(pre_r), pre_r)
            cum_c = sum(_dot(tri, t) for t in _split3(val_c))
            cum_r = sum(_dot(t, tri_t) for t in _split3(val_r))
            tot_r = jnp.sum(val_r, axis=1, keepdims=True)
            for h in range(ML_HEADS):
                ji, jf = (2 * d) * ML_HEADS + h, (2 * d + 1) * ML_HEADS + h
                p, half = h // 2, h % 2
                bcum_c = cum_c[:, jf:jf + 1]
                bcum_r = cum_r[jf:jf + 1, :]
                i_r = val_r[ji:ji + 1, :]
                btot = tot_r[jf:jf + 1, :]
                w_end = btot - bcum_r + i_r
                m_w = jnp.max(w_end, axis=1, keepdims=True)
                dm = jnp.where(causal, bcum_c - bcum_r + i_r, -jnp.inf)
                m_loc = jnp.max(dm, axis=1, keepdims=True)
                q_h = q_ref[0, rs, p * LANE:(p + 1) * LANE] * half_mask[half]
                kt_pair = kt_ref[0, p * LANE:(p + 1) * LANE, rs]
                kt_h = kt_ref[0, h * ML_DQK:(h + 1) * ML_DQK, rs]
                v_ext = jnp.concatenate([v_ref[0, rs, h * ML_DV:(h + 1) * ML_DV], e0], axis=1)
                s = _dot(q_h, kt_pair) * jnp.exp(dm - m_loc)
                sv = _dot(s.astype(BF16), v_ext)
                kw = (kt_h.astype(F32) * jnp.exp(w_end - m_w)).astype(BF16)
                dc = _dot(kw, v_ext)
                local[ch, h] = (q_h, bcum_c, btot, m_w, m_loc, sv, dc)
        for cc in range(TM // L):
            ch = cc if d == 0 else TM // L - 1 - cc
            rs = slice(ch * L, (ch + 1) * L)
            for h in range(ML_HEADS):
                p, half = h // 2, h % 2
                idx = d * ML_HEADS + h
                q_h, bcum_c, btot, m_w, m_loc, sv, dc = local[ch, h]
                m_old = m_ref[idx, 0:1, 0:1]
                c_pair = c_ref[d, p]
                inter = bcum_c + m_old
                m_t = jnp.maximum(inter, m_loc)
                tot = jnp.exp(inter - m_t) * _dot(q_h, c_pair.astype(BF16)) + jnp.exp(m_loc - m_t) * sv
                den = tot[:, ML_DV:ML_DV + 1]
                h_ref[0, rs, h * ML_DV:(h + 1) * ML_DV] = tot[:, 0:ML_DV] / jnp.maximum(jnp.abs(den), jnp.exp(-m_t))
                m_new = jnp.maximum(btot + m_old, m_w)
                hs = slice(half * ML_DQK, (half + 1) * ML_DQK)
                c_ref[d, p, hs, :] = jnp.exp(btot + m_old - m_new) * c_pair[hs, :] + jnp.exp(m_w - m_new) * dc
                m_ref[idx] = jnp.broadcast_to(m_new, (8, LANE))


def _mlstm(q, kt, v, gcol, grow, brow, bcol, n_lat_tiles):
    B, S, _ = q.shape
    nblk = S // TM
    fwd = lambda j: jnp.where(j == 0, n_lat_tiles, j - 1)
    bwd = lambda j: jnp.where(j == 0, n_lat_tiles, n_lat_tiles - j)

    def specs(order):
        return [pl.BlockSpec((1, TM, 256), lambda b, j: (b, order(j), 0)),
                pl.BlockSpec((1, 256, TM), lambda b, j: (b, 0, order(j))),
                pl.BlockSpec((1, TM, 512), lambda b, j: (b, order(j), 0)),
                pl.BlockSpec((1, TM, LANE), lambda b, j: (b, order(j), 0)),
                pl.BlockSpec((1, 16, TM), lambda b, j: (b, 0, order(j)))]

    return pl.pallas_call(
        _mlstm_kernel,
        out_shape=[jax.ShapeDtypeStruct((B, S, 512), F32)] * 2,
        grid=(B, nblk),
        in_specs=specs(fwd) + specs(bwd) + [_const_spec(brow.shape), _const_spec(bcol.shape)],
        out_specs=[pl.BlockSpec((1, TM, 512), lambda b, j: (b, fwd(j), 0)),
                   pl.BlockSpec((1, TM, 512), lambda b, j: (b, bwd(j), 0))],
        scratch_shapes=[pltpu.VMEM((2, ML_HEADS // 2, 2 * ML_DQK, 2 * ML_DV), F32),
                        pltpu.VMEM((2 * ML_HEADS, 8, LANE), F32)],
        compiler_params=_params("arbitrary", "arbitrary"), name="mlstm",
    )(q, kt, v, gcol, grow, q, kt, v, gcol, grow, brow, bcol)


def _softmax_pv_t(parts):
    ss = []
    for k, q, _, bias in parts:
        s = _dot_nt(k, q)
        ss.append(s if bias is None else s + bias)
    m = functools.reduce(jnp.maximum, [jnp.max(s, axis=0, keepdims=True) for s in ss])
    ps = [jnp.exp2(s - m) for s in ss]
    l = sum(jnp.sum(p, axis=0, keepdims=True) for p in ps)
    o = sum(_dot(vt, p.astype(BF16)) for (_, _, vt, _), p in zip(parts, ps))
    return o / l


def _mla_kernel(bounded_ref, q_ref, k_ref, vt_ref, o_ref, s_scr, p_scr, acc_scr, *, chunks):
    @pl.when(bounded_ref[0] != 0)
    def _():
        _mla_bounded(q_ref, k_ref, vt_ref, o_ref, p_scr, acc_scr, chunks)

    @pl.when(bounded_ref[0] == 0)
    def _():
        _mla_online(q_ref, k_ref, vt_ref, o_ref, s_scr, p_scr, acc_scr, chunks)


def _mla_bounded(q_ref, k_ref, vt_ref, o_ref, p_scr, acc_scr, chunks):
    n_heads = q_ref.shape[1]
    items = [(hh, ci) for hh in range(n_heads) for ci in range(len(chunks))]

    def probs(n):
        hh, ci = items[n]
        off, size = chunks[ci]
        p = jnp.exp2(_dot_nt(k_ref[0, hh, off:off + size, :], q_ref[0, hh]))
        p_scr[n % 2, 0:size, :] = p.astype(BF16)
        return jnp.sum(p, axis=0, keepdims=True)

    lsum = probs(0)
    l = None
    for n, (hh, ci) in enumerate(items):
        off, size = chunks[ci]
        l = lsum if ci == 0 else l + lsum
        if n + 1 < len(items):
            lsum = probs(n + 1)
        pv = _dot(vt_ref[0, hh * MLA_V:(hh + 1) * MLA_V, off:off + size], p_scr[n % 2, 0:size, :])
        if ci == 0:
            acc_scr[hh] = pv
        else:
            acc_scr[hh] += pv
        if ci == len(chunks) - 1:
            o_ref[0, hh * MLA_V:(hh + 1) * MLA_V, :] = (acc_scr[hh] / l).astype(BF16)


def _mla_online(q_ref, k_ref, vt_ref, o_ref, s_scr, p_scr, acc_scr, chunks):
    n_heads, tq = q_ref.shape[1], q_ref.shape[2]
    items = [(hh, ci) for hh in range(n_heads) for ci in range(len(chunks))]

    def scores(n):
        hh, ci = items[n]
        off, size = chunks[ci]
        s_scr[n % 2, 0:size, :] = _dot_nt(k_ref[0, hh, off:off + size, :], q_ref[0, hh])

    scores(0)
    m = l = None
    for n, (hh, ci) in enumerate(items):
        off, size = chunks[ci]
        slot = n % 2
        if n + 1 < len(items):
            scores(n + 1)
        if ci == 0:
            m = [jnp.full((1, LANE), -jnp.inf, F32)] * (tq // LANE)
            l = [jnp.zeros((1, LANE), F32)] * (tq // LANE)
        alphas = []
        for st in range(tq // LANE):
            cs = slice(st * LANE, (st + 1) * LANE)
            m_new = jnp.maximum(m[st], jnp.max(s_scr[slot, 0:size, cs], axis=0, keepdims=True))
            alpha = jnp.exp2(m[st] - m_new)
            p = jnp.exp2(s_scr[slot, 0:size, cs] - m_new)
            l[st] = alpha * l[st] + jnp.sum(p, axis=0, keepdims=True)
            m[st] = m_new
            p_scr[slot, 0:size, cs] = p.astype(BF16)
            alphas.append(alpha)
        pv = _dot(vt_ref[0, hh * MLA_V:(hh + 1) * MLA_V, off:off + size], p_scr[slot, 0:size, :])
        if ci == 0:
            acc_scr[hh] = pv
        else:
            acc_scr[hh] = jnp.concatenate(alphas, axis=1) * acc_scr[hh] + pv
        if ci == len(chunks) - 1:
            o_ref[0, hh * MLA_V:(hh + 1) * MLA_V, :] = (acc_scr[hh] / jnp.concatenate(l, axis=1)).astype(BF16)


def _mla(bounded, qm, km, vmt, n_lat):
    B, H, S, _ = qm.shape
    kv, hp = 512, 2
    chunks = tuple((i * kv, kv) for i in range(n_lat // kv)) + ((n_lat, S - n_lat),)
    return pl.pallas_call(
        functools.partial(_mla_kernel, chunks=chunks),
        out_shape=jax.ShapeDtypeStruct((B, H * MLA_V, n_lat), BF16),
        grid=(B, H // hp, n_lat // MLA_TQ),
        in_specs=[pl.BlockSpec(memory_space=pltpu.SMEM),
                  pl.BlockSpec((1, hp, MLA_TQ, LANE), lambda b, h, i: (b, h, i, 0)),
                  pl.BlockSpec((1, hp, S, LANE), lambda b, h, i: (b, h, 0, 0)),
                  pl.BlockSpec((1, hp * MLA_V, S), lambda b, h, i: (b, h, 0))],
        out_specs=pl.BlockSpec((1, hp * MLA_V, MLA_TQ), lambda b, h, i: (b, h, i)),
        scratch_shapes=[pltpu.VMEM((2, kv, MLA_TQ), F32), pltpu.VMEM((2, kv, MLA_TQ), BF16),
                        pltpu.VMEM((hp, MLA_V, MLA_TQ), F32)],
        compiler_params=_params("arbitrary", "arbitrary", "arbitrary"), name="mla_attn",
    )(bounded, qm, km, vmt)


def _natten_kernel(bounded_ref, q_ref, k_ref, vt_ref, bias_ref, o_ref, s_scr, p_scr, *, n_lat, n_ctx, rows):
    blk = pl.program_id(1)
    band = NA_BAND_ROWS * GRID_W
    nk = band + n_ctx
    tq = q_ref.shape[2]
    u0 = jnp.clip(blk * NA_Q_ROWS - NA_WIN_R // 2, 0, rows - NA_BAND_ROWS)
    koff = pl.multiple_of(u0 * GRID_W, 2 * LANE)
    lane = lax.broadcasted_iota(jnp.int32, (1, LANE), 1)
    half_mask = [(lane < NA_DH).astype(F32).astype(BF16), (lane >= NA_DH).astype(F32).astype(BF16)]

    def scores(h):
        qh = q_ref[0, h // 2] * half_mask[h % 2]
        return (_dot_nt(k_ref[0, h // 2, pl.ds(koff, band), :], qh) + bias_ref[h, 0],
                _dot_nt(k_ref[0, h // 2, n_lat:n_lat + n_ctx, :], qh))

    def finish(h, l):
        slot = h % 2
        hs = slice(h * NA_DH, (h + 1) * NA_DH)
        o = (_dot(vt_ref[0, hs, pl.ds(koff, band)], p_scr[slot, 0:band, :])
             + _dot(vt_ref[0, hs, n_lat:n_lat + n_ctx], p_scr[slot, band:nk, :]))
        o_ref[0, hs, :] = (o / l).astype(BF16)

    @pl.when(bounded_ref[0] != 0)
    def _():
        def probs(h):
            pb, pc = (jnp.exp2(s) for s in scores(h))
            p_scr[h % 2, 0:band, :] = pb.astype(BF16)
            p_scr[h % 2, band:nk, :] = pc.astype(BF16)
            return jnp.sum(pb, axis=0, keepdims=True) + jnp.sum(pc, axis=0, keepdims=True)

        l = probs(0)
        for h in range(NA_HEADS):
            l_next = probs(h + 1) if h + 1 < NA_HEADS else None
            finish(h, l)
            l = l_next

    @pl.when(bounded_ref[0] == 0)
    def _():
        def stage(h):
            s_scr[h % 2, 0:band, :], s_scr[h % 2, band:nk, :] = scores(h)

        stage(0)
        for h in range(NA_HEADS):
            slot = h % 2
            if h + 1 < NA_HEADS:
                stage(h + 1)
            ls = []
            for st in range(tq // LANE):
                cs = slice(st * LANE, (st + 1) * LANE)
                m = jnp.max(s_scr[slot, :, cs], axis=0, keepdims=True)
                p = jnp.exp2(s_scr[slot, :, cs] - m)
                ls.append(jnp.sum(p, axis=0, keepdims=True))
                p_scr[slot, :, cs] = p.astype(BF16)
            finish(h, jnp.concatenate(ls, axis=1))


def _natten(bounded, qn, kn, vnt, bias, n_lat):
    B, P, S, _ = qn.shape
    rows = n_lat // GRID_W
    nblk = rows // NA_Q_ROWS
    tq = NA_Q_ROWS * GRID_W
    band = NA_BAND_ROWS * GRID_W
    nk = band + S - n_lat
    btype = lambda i: jnp.where(i == 0, 0, jnp.where(i == nblk - 1, 2, 1))
    return pl.pallas_call(
        functools.partial(_natten_kernel, n_lat=n_lat, n_ctx=S - n_lat, rows=rows),
        out_shape=jax.ShapeDtypeStruct((B, NA_HEADS * NA_DH, n_lat), BF16),
        grid=(B, nblk),
        in_specs=[pl.BlockSpec(memory_space=pltpu.SMEM),
                  pl.BlockSpec((1, P, tq, LANE), lambda b, i: (b, 0, i, 0)),
                  pl.BlockSpec((1, P, S, LANE), lambda b, i: (b, 0, 0, 0)),
                  pl.BlockSpec((1, NA_HEADS * NA_DH, S), lambda b, i: (b, 0, 0)),
                  pl.BlockSpec((NA_HEADS, 1, band, tq), lambda b, i: (0, btype(i), 0, 0))],
        out_specs=pl.BlockSpec((1, NA_HEADS * NA_DH, tq), lambda b, i: (b, 0, i)),
        scratch_shapes=[pltpu.VMEM((2, nk, tq), F32), pltpu.VMEM((2, nk, tq), BF16)],
        compiler_params=_params("arbitrary", "arbitrary"), name="natten",
    )(bounded, qn, kn, vnt, bias)


def _na_bias_kernel(rpb_ref, o_ref, bc_ref, *, rows):
    l, h = pl.program_id(0), pl.program_id(1)
    tq = NA_Q_ROWS * GRID_W
    kc = lax.broadcasted_iota(jnp.int32, (GRID_W, tq), 0)
    ql = lax.broadcasted_iota(jnp.int32, (GRID_W, tq), 1)
    c = ql & (GRID_W - 1)
    qi = ql >> 6
    dc = kc - c + (NA_WIN_C - 1)
    cs = jnp.clip(c - NA_WIN_C // 2, 0, GRID_W - NA_WIN_C)
    col_ok = (kc >= cs) & (kc < cs + NA_WIN_C)
    n_dr, n_dc = 2 * NA_WIN_R - 1, 2 * NA_WIN_C - 1
    base = (l * NA_HEADS + h) * (n_dr * n_dc)
    for dr in range(n_dr):
        acc = jnp.zeros((GRID_W, tq), F32)
        for j in range(n_dc):
            acc = jnp.where(dc == j, rpb_ref[base + dr * n_dc + j], acc)
        bc_ref[dr] = jnp.where(col_ok, acc * LOG2E, NEG_INF)
    nblk = rows // NA_Q_ROWS
    for t, blk in enumerate((0, 1, nblk - 1)):
        r0 = blk * NA_Q_ROWS
        u0 = min(max(r0 - NA_WIN_R // 2, 0), rows - NA_BAND_ROWS)
        for j in range(NA_BAND_ROWS):
            tile = jnp.full((GRID_W, tq), NEG_INF, F32)
            for i in range(NA_Q_ROWS):
                r, kr = r0 + i, u0 + j
                rs = min(max(r - NA_WIN_R // 2, 0), rows - NA_WIN_R)
                if rs <= kr < rs + NA_WIN_R:
                    tile = jnp.where(qi == i, bc_ref[kr - r + NA_WIN_R - 1], tile)
            o_ref[0, 0, t, j * GRID_W:(j + 1) * GRID_W, :] = tile


def _na_bias(rpb, rows):
    L = rpb.shape[0]
    tq = NA_Q_ROWS * GRID_W
    band = NA_BAND_ROWS * GRID_W
    return pl.pallas_call(
        functools.partial(_na_bias_kernel, rows=rows),
        out_shape=jax.ShapeDtypeStruct((L, NA_HEADS, 3, band, tq), F32),
        grid=(L, NA_HEADS),
        in_specs=[pl.BlockSpec(memory_space=pltpu.SMEM)],
        out_specs=pl.BlockSpec((1, 1, 3, band, tq), lambda l, h: (l, h, 0, 0, 0)),
        scratch_shapes=[pltpu.VMEM((2 * NA_WIN_R - 1, GRID_W, tq), F32)],
        compiler_params=_params("arbitrary", "arbitrary"), name="na_bias",
    )(rpb.reshape(-1))


def _ctx_attn_kernel(qm_ref, km_ref, vmt_ref, qn_ref, kn_ref, vnt_ref, mo_ref, no_ref):
    lane = lax.broadcasted_iota(jnp.int32, (1, LANE), 1)
    for hh in range(2):
        hs = slice(hh * MLA_V, (hh + 1) * MLA_V)
        mo_ref[0, hs, :] = _softmax_pv_t([(km_ref[0, hh], qm_ref[0, hh], vmt_ref[0, hs, :], None)]).astype(BF16)
        qh = qn_ref[0, 0] * ((lane < NA_DH) if hh == 0 else (lane >= NA_DH)).astype(F32).astype(BF16)
        no_ref[0, hs, :] = _softmax_pv_t([(kn_ref[0, 0], qh, vnt_ref[0, hs, :], None)]).astype(BF16)


def _ctx_attn(qm, km, vmt, qn, kn, vnt, n_lat):
    B, _, S, _ = qm.shape
    n_ctx = S - n_lat
    t = n_lat // n_ctx
    head = lambda b, p: (b, p, t, 0)
    rowsT = lambda b, p: (b, p, t)
    out = lambda b, p: (b, p, 0)
    return pl.pallas_call(
        _ctx_attn_kernel,
        out_shape=[jax.ShapeDtypeStruct((B, MLA_HEADS * MLA_V, n_ctx), BF16),
                   jax.ShapeDtypeStruct((B, NA_HEADS * NA_DH, n_ctx), BF16)],
        grid=(B, NA_HEADS // 2),
        in_specs=[pl.BlockSpec((1, 2, n_ctx, LANE), head), pl.BlockSpec((1, 2, n_ctx, LANE), head),
                  pl.BlockSpec((1, 2 * MLA_V, n_ctx), rowsT),
                  pl.BlockSpec((1, 1, n_ctx, LANE), head), pl.BlockSpec((1, 1, n_ctx, LANE), head),
                  pl.BlockSpec((1, 2 * NA_DH, n_ctx), rowsT)],
        out_specs=[pl.BlockSpec((1, 2 * MLA_V, n_ctx), out), pl.BlockSpec((1, 2 * NA_DH, n_ctx), out)],
        compiler_params=_params("arbitrary", "arbitrary"), name="ctx_attn",
    )(qm, km, vmt, qn, kn, vnt)


def _merge_kernel(x_ref, mod_ref, g1_ref, hf_ref, hb_ref, o_ref, mo_ref, no_ref, moc_ref, noc_ref, wg_ref, gout_ref,
                  wml_ref, wmla_ref, wna_ref, wout_ref, x1_ref, *, n_lat_tiles):
    x = x_ref[0]
    m = mod_ref[0]
    hb = (_rms(x, g1_ref[...], D_MODEL) * (1.0 + m[1:2]) + m[0:1]).astype(BF16)
    gates = jax.nn.sigmoid(_dot(hb, wg_ref[...]))
    hs = hf_ref[0] + hb_ref[0]
    og = jax.nn.sigmoid(o_ref[0])
    gout = gout_ref[...]
    hn = jnp.concatenate(
        [_rms(hs[:, h * ML_DV:(h + 1) * ML_DV], gout[:, h * ML_DV:(h + 1) * ML_DV], ML_DV) for h in range(ML_HEADS)],
        axis=1)
    y_a = _dot((hn * og).astype(BF16), wml_ref[...])
    is_ctx = pl.program_id(1) >= n_lat_tiles
    y_b = _dot_tn(jnp.where(is_ctx, moc_ref[0], mo_ref[0]), wmla_ref[...])
    y_c = _dot_tn(jnp.where(is_ctx, noc_ref[0], no_ref[0]), wna_ref[...])
    mg = gates[:, 0:D_MODEL] * y_a + gates[:, D_MODEL:2 * D_MODEL] * y_b + gates[:, 2 * D_MODEL:] * y_c
    x1_ref[0] = x + m[2:3] * _dot(mg.astype(BF16), wout_ref[...])


def _merge(x, mods, g1, hf, hb, o, mla_o, na_o, mla_oc, na_oc, w, nt, n_lat_tiles):
    B = x.shape[0]
    tok = lambda b, t: (b, t, 0)
    lat_t = lambda b, t: (b, 0, jnp.minimum(t, n_lat_tiles - 1))
    ctx_t = lambda b, t: (b, 0, 0)
    in_specs = [pl.BlockSpec((1, TM, D_MODEL), tok),
                pl.BlockSpec((1, 8, D_MODEL), lambda b, t: (jnp.where(t >= n_lat_tiles, B, b), 0, 0)),
                _const_spec(g1.shape),
                pl.BlockSpec((1, TM, 512), tok), pl.BlockSpec((1, TM, 512), tok), pl.BlockSpec((1, TM, 512), tok),
                pl.BlockSpec((1, 512, TM), lat_t), pl.BlockSpec((1, 512, TM), lat_t),
                pl.BlockSpec((1, 512, TM), ctx_t), pl.BlockSpec((1, 512, TM), ctx_t)]
    names = ("wg", "gout", "wml", "wmla", "wna", "wout")
    in_specs += [_const_spec(w[k].shape) for k in names]
    return pl.pallas_call(
        functools.partial(_merge_kernel, n_lat_tiles=n_lat_tiles),
        out_shape=jax.ShapeDtypeStruct((B, nt * TM, D_MODEL), F32),
        grid=(B, nt), in_specs=in_specs, out_specs=pl.BlockSpec((1, TM, D_MODEL), tok),
        compiler_params=_params("arbitrary", "arbitrary"), name="merge",
    )(x, mods, g1, hf, hb, o, mla_o, na_o, mla_oc, na_oc, *[w[k] for k in names])


def _mlp_kernel(x_ref, mod_ref, g2_ref, w1_ref, w2_ref, o_ref):
    x = x_ref[0]
    m = mod_ref[0]
    hb = (_rms(x, g2_ref[...], D_MODEL) * (1.0 + m[4:5]) + m[3:4]).astype(BF16)
    acc = jnp.zeros((TM, D_MODEL), F32)
    fc = 1024
    for c in range(D_FF // fc):
        u = jnp.maximum(_dot(hb, w1_ref[:, c * fc:(c + 1) * fc]), 0.0)
        acc = acc + _dot((u * u).astype(BF16), w2_ref[c * fc:(c + 1) * fc, :])
    o_ref[0] = x + m[5:6] * acc


def _mlp(x1, mods, g2, w1, w2, nt, n_lat_tiles):
    B = x1.shape[0]
    tok = lambda b, t: (b, t, 0)
    return pl.pallas_call(
        _mlp_kernel,
        out_shape=jax.ShapeDtypeStruct((B, nt * TM, D_MODEL), F32),
        grid=(B, nt),
        in_specs=[pl.BlockSpec((1, TM, D_MODEL), tok),
                  pl.BlockSpec((1, 8, D_MODEL), lambda b, t: (jnp.where(t >= n_lat_tiles, B, b), 0, 0)),
                  _const_spec(g2.shape), _const_spec(w1.shape), _const_spec(w2.shape)],
        out_specs=pl.BlockSpec((1, TM, D_MODEL), tok),
        compiler_params=_params("arbitrary", "arbitrary"), name="mlp",
    )(x1, mods, g2, w1, w2)


def _rope_tables(n_lat, n_ctx):
    nf = MLA_ROPE // 4
    inv = jnp.power(ROPE_THETA, -jnp.arange(nf, dtype=F32) / nf)
    pos = jnp.arange(n_lat)
    ang_r = (pos // GRID_W).astype(F32)[:, None] * inv
    ang_c = (pos % GRID_W).astype(F32)[:, None] * inv
    z = jnp.zeros((n_lat, nf), F32)
    one = lambda n: jnp.ones((n_lat, n), F32)
    zero = lambda n: jnp.zeros((n_lat, n), F32)
    cr, sr, cc, sc = jnp.cos(ang_r), jnp.sin(ang_r), jnp.cos(ang_c), jnp.sin(ang_c)
    cos = jnp.concatenate([one(MLA_NOPE), cr, cr, cc, cc, one(LANE - MLA_DQK)], axis=1)
    sina = jnp.concatenate([zero(MLA_NOPE), z, sr, z, sc, zero(LANE - MLA_DQK)], axis=1)
    sinb = jnp.concatenate([zero(MLA_NOPE), -sr, z, -sc, z, zero(LANE - MLA_DQK)], axis=1)
    ident = (jnp.ones((n_ctx, LANE), F32), jnp.zeros((n_ctx, LANE), F32), jnp.zeros((n_ctx, LANE), F32))
    return tuple(jnp.concatenate([t, i], axis=0) for t, i in zip((cos, sina, sinb), ident))


def _layer_weights(l, w_in, ml_i_bias, ml_f_bias, ml_g_out, ml_w_o, mla_g_cq, mla_w_uq, mla_g_ckv, mla_w_ukv,
                   mla_g_q, mla_g_k, mla_w_o, na_g_q, na_g_k, na_w_o, w_out, w_ff1, w_ff2):
    offs = np.cumsum(IN_SIZES)[:-1].tolist()
    mlq, mlk, mlv, mlo, mlg, dq, dkv, kr, naq, nak, nav, gates = jnp.split(w_in[l], offs, axis=-1)
    padc = lambda a, n: jnp.pad(a, ((0, 0), (0, n - a.shape[1])))
    kr_slab = jnp.pad(kr, ((0, 0), (MLA_NOPE, LANE - MLA_DQK)))
    w_uq = jnp.pad(mla_w_uq[l].reshape(MLA_Q_RANK, MLA_HEADS, MLA_DQK), ((0, 0), (0, 0), (0, LANE - MLA_DQK)))
    w_ukv = mla_w_ukv[l].reshape(MLA_KV_RANK, MLA_HEADS, MLA_NOPE + MLA_V)
    w_uk = jnp.pad(w_ukv[..., :MLA_NOPE], ((0, 0), (0, 0), (0, LANE - MLA_NOPE)))
    w_uv = w_ukv[..., MLA_NOPE:].reshape(MLA_KV_RANK, MLA_HEADS * MLA_V)
    row = lambda a: a.reshape(1, -1).astype(F32)
    bias16 = jnp.stack([ml_i_bias[l, 0], ml_f_bias[l, 0], ml_i_bias[l, 1], ml_f_bias[l, 1]]).reshape(-1).astype(F32)
    return {
        "wa": jnp.concatenate([mlq * (ML_DQK ** -0.5), mlv, mlo, padc(mlg, LANE)], axis=1).astype(BF16),
        "wat": jnp.concatenate([mlk.T, mlg.T], axis=0).astype(BF16),
        "wb": jnp.concatenate([dq, dkv, kr_slab], axis=1).astype(BF16),
        "wc": jnp.concatenate([naq, nak], axis=1).astype(BF16),
        "wct": nav.T.astype(BF16),
        "wuq": w_uq.reshape(MLA_Q_RANK, MLA_HEADS * LANE).astype(BF16),
        "wuk": w_uk.reshape(MLA_KV_RANK, MLA_HEADS * LANE).astype(BF16),
        "wuvt": w_uv.T.astype(BF16),
        "gcq": row(mla_g_cq[l]), "gckv": row(mla_g_ckv[l]),
        "gq": padc(row(mla_g_q[l]) * (MLA_DQK ** -0.5 * LOG2E), LANE), "gk": padc(row(mla_g_k[l]), LANE),
        "ngq": row(jnp.tile(na_g_q[l], 2)) * (NA_DH ** -0.5 * LOG2E), "ngk": row(jnp.tile(na_g_k[l], 2)),
        "brow": padc(bias16.reshape(1, 16), LANE), "bcol": bias16.reshape(16, 1),
        "wg": gates.astype(BF16), "gout": row(ml_g_out[l]),
        "wml": ml_w_o[l].astype(BF16), "wmla": mla_w_o[l].astype(BF16), "wna": na_w_o[l].astype(BF16),
        "wout": w_out[l].astype(BF16), "w1": w_ff1[l].astype(BF16), "w2": w_ff2[l].astype(BF16),
    }


def kernel(x, c, ctx, c_ctx, w_mod, b_mod, g_norm1, g_norm2, w_in, ml_i_bias, ml_f_bias, ml_g_out, ml_w_o, mla_g_cq, mla_w_uq, mla_g_ckv, mla_w_ukv, mla_g_q, mla_g_k, mla_w_o, na_g_q, na_g_k, na_rpb, na_w_o, w_out, w_ff1, w_ff2):
    B, T, D = x.shape
    C = ctx.shape[1]
    depth = w_in.shape[0]
    assert D == D_MODEL and C == TM and T % MLA_TQ == 0 and B < 16
    rows = T // GRID_W
    assert rows % NA_Q_ROWS == 0 and rows >= NA_BAND_ROWS
    n_lat_tiles = T // TM

    cc = jnp.zeros((16, D), F32).at[:B].set(c).at[B].set(c_ctx)
    mod = _modulation(cc, w_mod, b_mod).reshape(depth, 16, 6, D)
    mod = jnp.pad(mod, ((0, 0), (0, 0), (0, 2), (0, 0)))
    tabs = _rope_tables(T, C)
    na_bias = _na_bias(na_rpb.astype(F32), rows)
    xs = jnp.concatenate([x, ctx], axis=1)

    for l in range(depth):
        last = l == depth - 1
        w = _layer_weights(l, w_in, ml_i_bias, ml_f_bias, ml_g_out, ml_w_o, mla_g_cq, mla_w_uq, mla_g_ckv,
                           mla_w_ukv, mla_g_q, mla_g_k, mla_w_o, na_g_q, na_g_k, na_w_o, w_out, w_ff1, w_ff2)
        g1 = g_norm1[l].reshape(1, D).astype(F32)
        g2 = g_norm2[l].reshape(1, D).astype(F32)
        q, kt, v, o, gcol, grow, qm, km, vmt, qn, kn, vnt = _inproj(xs, mod[l], g1, w, tabs, n_lat_tiles)
        hf, hb = _mlstm(q, kt, v, gcol, grow, w["brow"], w["bcol"], n_lat_tiles)
        mla_bound = MLA_DQK * jnp.max(jnp.abs(w["gq"])) * jnp.max(jnp.abs(w["gk"]))
        na_bound = (NA_DH * jnp.max(jnp.abs(w["ngq"])) * jnp.max(jnp.abs(w["ngk"]))
                    + LOG2E * jnp.max(jnp.abs(na_rpb[l].astype(F32))))
        mla_o = _mla((mla_bound <= SCORE_BOUND_LIMIT).astype(jnp.int32).reshape(1), qm, km, vmt, T)
        na_o = _natten((na_bound <= SCORE_BOUND_LIMIT).astype(jnp.int32).reshape(1), qn, kn, vnt, na_bias[l], T)
        mla_oc, na_oc = (mla_o, na_o) if last else _ctx_attn(qm, km, vmt, qn, kn, vnt, T)
        nt = n_lat_tiles if last else n_lat_tiles + 1
        x1 = _merge(xs, mod[l], g1, hf, hb, o, mla_o, na_o, mla_oc, na_oc, w, nt, n_lat_tiles)
        xs = _mlp(x1, mod[l], g2, w["w1"], w["w2"], nt, n_lat_tiles)
    return xs
```

```python
import functools

import numpy as np
import jax
import jax.numpy as jnp
from jax import lax
from jax.experimental import pallas as pl
from jax.experimental.pallas import tpu as pltpu

F32 = jnp.float32
BF16 = jnp.bfloat16

D_MODEL = 1024
GRID_W = 64
ML_HEADS, ML_DQK, ML_DV = 4, 64, 128
ML_CHUNK = 128
MLA_HEADS, MLA_Q_RANK, MLA_KV_RANK, MLA_NOPE, MLA_ROPE, MLA_V = 8, 384, 256, 64, 32, 64
MLA_DQK = MLA_NOPE + MLA_ROPE
NA_HEADS, NA_DH, NA_WIN_R, NA_WIN_C = 8, 64, 8, 16
NA_Q_ROWS = 4
NA_BAND_ROWS = NA_Q_ROWS + NA_WIN_R
D_FF = 4 * D_MODEL
ROPE_THETA = 10000.0
EPS = 1e-6
LOG2E = 1.4426950408889634
SCORE_BOUND_LIMIT = 60.0
NEG_INF = -1e30
IN_SIZES = (256, 256, 512, 512, 16, MLA_Q_RANK, MLA_KV_RANK, MLA_ROPE, 512, 512, 512, 3 * D_MODEL)
LANE = 128
TM = 256
MLA_TQ = 1024
VMEM_LIMIT = 56 * 1024 * 1024


def _dot(a, b):
    return jnp.dot(a, b, preferred_element_type=F32)


def _dot_nt(a, b):
    return lax.dot_general(a, b, (((1,), (1,)), ((), ())), preferred_element_type=F32)


def _dot_tn(a, b):
    return lax.dot_general(a, b, (((0,), (0,)), ((), ())), preferred_element_type=F32)


def _rms(x, g, n):
    ms = jnp.sum(x * x, axis=-1, keepdims=True) * (1.0 / n)
    return x * lax.rsqrt(ms + EPS) * g


def _split3(x):
    hi = x.astype(BF16)
    r = x - hi.astype(F32)
    mid = r.astype(BF16)
    lo = (r - mid.astype(F32)).astype(BF16)
    return hi, mid, lo


def _params(*sem):
    return pltpu.CompilerParams(dimension_semantics=sem, vmem_limit_bytes=VMEM_LIMIT)


def _const_spec(shape):
    nd = len(shape)
    return pl.BlockSpec(shape, lambda *_: (0,) * nd)


def _mod_kernel(c_ref, w_ref, b_ref, o_ref):
    c = c_ref[...]
    s = c * jax.nn.sigmoid(c)
    o_ref[0] = _dot(s, w_ref[0]) + b_ref[0]


def _modulation(cc, w_mod, b_mod):
    L = w_mod.shape[0]
    tn = 1024
    return pl.pallas_call(
        _mod_kernel,
        out_shape=jax.ShapeDtypeStruct((L, 16, 6 * D_MODEL), F32),
        grid=(L, 6 * D_MODEL // tn),
        in_specs=[pl.BlockSpec((16, D_MODEL), lambda l, n: (0, 0)),
                  pl.BlockSpec((1, D_MODEL, tn), lambda l, n: (l, 0, n)),
                  pl.BlockSpec((1, 1, tn), lambda l, n: (l, 0, n))],
        out_specs=pl.BlockSpec((1, 16, tn), lambda l, n: (l, 0, n)),
        compiler_params=_params("arbitrary", "arbitrary"),
        name="modulation",
    )(cc, w_mod, b_mod.reshape(L, 1, 6 * D_MODEL))


def _head_inv_rms(y, bd, n):
    y2 = (y * y).astype(BF16)
    w = bd.shape[0]
    ss = jnp.concatenate([_dot(y2[:, j:j + w], bd) for j in range(0, y.shape[1], w)], axis=1)
    return lax.rsqrt(ss * (1.0 / n) + EPS)


def _inproj_kernel(x_ref, xc_ref, mod_ref, g1_ref, wa_ref, wat_ref, wb_ref, wc_ref, wct_ref, wuq_ref, wuqs_ref, wuk_ref,
                   wuvt_ref, gcq_ref, gckv_ref, ng_ref, bd128_ref, bd64_ref, cgq_ref, sgq_ref, cgk_ref, sgk_ref,
                   q_ref, kt_ref, v_ref, o_ref, gcol_ref, grow_ref, qm_ref, km_ref, vmt_ref, qn_ref, kn_ref, vnt_ref,
                   *, n_lat_tiles):
    x = jnp.where(pl.program_id(1) >= n_lat_tiles, xc_ref[0], x_ref[0])
    m = mod_ref[0]
    hb = (_rms(x, g1_ref[...], D_MODEL) * (1.0 + m[1:2]) + m[0:1]).astype(BF16)

    a = _dot(hb, wa_ref[...])
    q_ref[0] = a[:, 0:256].astype(BF16)
    v_ref[0] = a[:, 256:768].astype(BF16)
    o_ref[0] = a[:, 768:1280]
    gcol_ref[0] = a[:, 1280:1408]
    at = _dot_nt(wat_ref[...], hb)
    kt_ref[0] = at[0:256].astype(BF16)
    grow_ref[0] = at[256:272]

    bs = _dot(hb, wb_ref[...])
    dqn = _rms(bs[:, 0:MLA_Q_RANK], gcq_ref[...], MLA_Q_RANK).astype(BF16)
    dkvn = _rms(bs[:, MLA_Q_RANK:MLA_Q_RANK + MLA_KV_RANK], gckv_ref[...], MLA_KV_RANK).astype(BF16)
    kr = bs[:, MLA_Q_RANK + MLA_KV_RANK:MLA_Q_RANK + MLA_KV_RANK + LANE]
    kr_sw = bs[:, MLA_Q_RANK + MLA_KV_RANK + LANE:]
    qh_all = _dot(dqn, wuq_ref[...])
    qsw_all = _dot(dqn, wuqs_ref[...])
    kh_all = _dot(dkvn, wuk_ref[...]) + jnp.concatenate([kr] * MLA_HEADS, axis=1)
    inv_q = _head_inv_rms(qh_all, bd128_ref[...], MLA_DQK)
    inv_k = _head_inv_rms(kh_all, bd128_ref[...], MLA_DQK)
    cgq, sgq, cgk, sgk = cgq_ref[...], sgq_ref[...], cgk_ref[...], sgk_ref[...]
    for h in range(MLA_HEADS):
        sl = slice(h * LANE, (h + 1) * LANE)
        qm_ref[0, h] = ((qh_all[:, sl] * cgq + qsw_all[:, sl] * sgq) * inv_q[:, sl]).astype(BF16)
        km_ref[0, h] = ((kh_all[:, sl] * cgk + kr_sw * sgk) * inv_k[:, sl]).astype(BF16)
    vmt_ref[0] = _dot_nt(wuvt_ref[...], dkvn).astype(BF16)

    c = _dot(hb, wc_ref[...])
    cn = c * _head_inv_rms(c, bd64_ref[...], NA_DH) * ng_ref[...]
    for p in range(NA_HEADS // 2):
        qn_ref[0, p] = cn[:, p * LANE:(p + 1) * LANE].astype(BF16)
        kn_ref[0, p] = cn[:, NA_HEADS * NA_DH + p * LANE:NA_HEADS * NA_DH + (p + 1) * LANE].astype(BF16)
    vnt_ref[0] = _dot_nt(wct_ref[...], hb).astype(BF16)


def _stream_specs(ctx_block, n_lat_tiles):
    return [pl.BlockSpec((1, TM, D_MODEL), lambda b, t: (b, jnp.minimum(t, n_lat_tiles - 1), 0)),
            pl.BlockSpec((1, TM, D_MODEL), lambda b, t: (b, ctx_block, 0))]


def _inproj(x, xc, ctx_block, mods, g1, w, tabs, n_lat_tiles):
    B = x.shape[0]
    nt = n_lat_tiles + 1
    S = nt * TM
    tok = lambda b, t: (b, t, 0)
    tokT = lambda b, t: (b, 0, t)
    head = lambda b, t: (b, 0, t, 0)
    in_specs = _stream_specs(ctx_block, n_lat_tiles)
    in_specs += [pl.BlockSpec((1, 8, D_MODEL), lambda b, t: (jnp.where(t >= n_lat_tiles, B, b), 0, 0)),
                 _const_spec(g1.shape)]
    consts = ("wa", "wat", "wb", "wc", "wct", "wuq", "wuqs", "wuk", "wuvt", "gcq", "gckv", "ng", "bd128", "bd64")
    in_specs += [_const_spec(w[k].shape) for k in consts]
    in_specs += [pl.BlockSpec((TM, LANE), lambda b, t: (t, 0))] * 4
    out_shape = [jax.ShapeDtypeStruct((B, S, 256), BF16),
                 jax.ShapeDtypeStruct((B, 256, S), BF16),
                 jax.ShapeDtypeStruct((B, S, 512), BF16),
                 jax.ShapeDtypeStruct((B, S, 512), F32),
                 jax.ShapeDtypeStruct((B, S, LANE), F32),
                 jax.ShapeDtypeStruct((B, 16, S), F32),
                 jax.ShapeDtypeStruct((B, MLA_HEADS, S, LANE), BF16),
                 jax.ShapeDtypeStruct((B, MLA_HEADS, S, LANE), BF16),
                 jax.ShapeDtypeStruct((B, MLA_HEADS * MLA_V, S), BF16),
                 jax.ShapeDtypeStruct((B, NA_HEADS // 2, S, LANE), BF16),
                 jax.ShapeDtypeStruct((B, NA_HEADS // 2, S, LANE), BF16),
                 jax.ShapeDtypeStruct((B, NA_HEADS * NA_DH, S), BF16)]
    out_specs = [pl.BlockSpec((1, TM, 256), tok), pl.BlockSpec((1, 256, TM), tokT),
                 pl.BlockSpec((1, TM, 512), tok), pl.BlockSpec((1, TM, 512), tok),
                 pl.BlockSpec((1, TM, LANE), tok), pl.BlockSpec((1, 16, TM), tokT),
                 pl.BlockSpec((1, MLA_HEADS, TM, LANE), head), pl.BlockSpec((1, MLA_HEADS, TM, LANE), head),
                 pl.BlockSpec((1, MLA_HEADS * MLA_V, TM), tokT),
                 pl.BlockSpec((1, NA_HEADS // 2, TM, LANE), head), pl.BlockSpec((1, NA_HEADS // 2, TM, LANE), head),
                 pl.BlockSpec((1, NA_HEADS * NA_DH, TM), tokT)]
    return pl.pallas_call(
        functools.partial(_inproj_kernel, n_lat_tiles=n_lat_tiles),
        out_shape=out_shape, grid=(B, nt), in_specs=in_specs, out_specs=out_specs,
        compiler_params=_params("arbitrary", "arbitrary"), name="inproj",
    )(x, xc, mods, g1, *[w[k] for k in consts], *tabs)


def _log_sigmoid(x):
    return jnp.minimum(x, 0.0) - jnp.log1p(jnp.exp(-jnp.abs(x)))


def _mlstm_kernel(qf_ref, ktf_ref, vf_ref, gcf_ref, grf_ref, qb_ref, ktb_ref, vb_ref, gcb_ref, grb_ref,
                  brow_ref, bcol_ref, hf_ref, hb_ref, c_ref, m_ref):
    L = ML_CHUNK

    @pl.when(pl.program_id(1) == 0)
    def _():
        c_ref[...] = jnp.zeros_like(c_ref)
        m_ref[...] = jnp.zeros_like(m_ref)

    row = lax.broadcasted_iota(jnp.int32, (L, L), 0)
    col = lax.broadcasted_iota(jnp.int32, (L, L), 1)
    lane = lax.broadcasted_iota(jnp.int32, (1, LANE), 1)
    isf_row = (lane & ML_HEADS) != 0
    sub = lax.broadcasted_iota(jnp.int32, (16, 1), 0)
    isf_col = (sub & ML_HEADS) != 0
    half_mask = [(lane < ML_DQK).astype(F32).astype(BF16), (lane >= ML_DQK).astype(F32).astype(BF16)]
    e0 = jnp.broadcast_to((lane == 0).astype(F32).astype(BF16), (L, LANE))
    dirs = ((qf_ref, ktf_ref, vf_ref, gcf_ref, grf_ref, hf_ref), (qb_ref, ktb_ref, vb_ref, gcb_ref, grb_ref, hb_ref))
    for d, (q_ref, kt_ref, v_ref, gc_ref, gr_ref, h_ref) in enumerate(dirs):
        causal = (col <= row) if d == 0 else (col >= row)
        tri = causal.astype(F32).astype(BF16)
        tri_t = ((row <= col) if d == 0 else (row >= col)).astype(F32).astype(BF16)
        local = {}
        for ch in range(TM // L):
            rs = slice(ch * L, (ch + 1) * L)
            pre_c = gc_ref[0, rs, :] + brow_ref[...]
            val_c = jnp.where(isf_row, _log_sigmoid(pre_c), pre_c)
            pre_r = gr_ref[0, :, rs] + bcol_ref[...]
            val_r = jnp.where(isf_col, _log_sigmoid(pre_r), pre_r)
            cum_c = sum(_dot(tri, t) for t in _split3(val_c))
            cum_r = sum(_dot(t, tri_t) for t in _split3(val_r))
            tot_r = jnp.sum(val_r, axis=1, keepdims=True)
            for h in range(ML_HEADS):
                ji, jf = (2 * d) * ML_HEADS + h, (2 * d + 1) * ML_HEADS + h
                p, half = h // 2, h % 2
                bcum_c = cum_c[:, jf:jf + 1]
                bcum_r = cum_r[jf:jf + 1, :]
                i_r = val_r[ji:ji + 1, :]
                btot = tot_r[jf:jf + 1, :]
                w_end = btot - bcum_r + i_r
                m_w = jnp.max(w_end, axis=1, keepdims=True)
                dm = jnp.where(causal, bcum_c - bcum_r + i_r, -jnp.inf)
                m_loc = jnp.max(dm, axis=1, keepdims=True)
                q_h = q_ref[0, rs, p * LANE:(p + 1) * LANE] * half_mask[half]
                kt_pair = kt_ref[0, p * LANE:(p + 1) * LANE, rs]
                kt_h = kt_ref[0, h * ML_DQK:(h + 1) * ML_DQK, rs]
                v_ext = jnp.concatenate([v_ref[0, rs, h * ML_DV:(h + 1) * ML_DV], e0], axis=1)
                s = _dot(q_h, kt_pair) * jnp.exp(dm - m_loc)
                sv = _dot(s.astype(BF16), v_ext)
                kw = (kt_h.astype(F32) * jnp.exp(w_end - m_w)).astype(BF16)
                dc = _dot(kw, v_ext)
                local[ch, h] = (q_h, bcum_c, btot, m_w, m_loc, sv, dc)
        for cc in range(TM // L):
            ch = cc if d == 0 else TM // L - 1 - cc
            rs = slice(ch * L, (ch + 1) * L)
            for h in range(ML_HEADS):
                p, half = h // 2, h % 2
                idx = d * ML_HEADS + h
                q_h, bcum_c, btot, m_w, m_loc, sv, dc = local[ch, h]
                m_old = m_ref[idx, 0:1, 0:1]
                c_pair = c_ref[d, p]
                inter = bcum_c + m_old
                m_t = jnp.maximum(inter, m_loc)
                tot = jnp.exp(inter - m_t) * _dot(q_h, c_pair.astype(BF16)) + jnp.exp(m_loc - m_t) * sv
                den = tot[:, ML_DV:ML_DV + 1]
                h_ref[0, rs, h * ML_DV:(h + 1) * ML_DV] = tot[:, 0:ML_DV] / jnp.maximum(jnp.abs(den), jnp.exp(-m_t))
                m_new = jnp.maximum(btot + m_old, m_w)
                hs = slice(half * ML_DQK, (half + 1) * ML_DQK)
                c_ref[d, p, hs, :] = jnp.exp(btot + m_old - m_new) * c_pair[hs, :] + jnp.exp(m_w - m_new) * dc
                m_ref[idx] = jnp.broadcast_to(m_new, (8, LANE))


def _mlstm(q, kt, v, gcol, grow, brow, bcol, n_lat_tiles):
    B, S, _ = q.shape
    nblk = S // TM
    fwd = lambda j: jnp.where(j == 0, n_lat_tiles, j - 1)
    bwd = lambda j: jnp.where(j == 0, n_lat_tiles, n_lat_tiles - j)

    def specs(order):
        return [pl.BlockSpec((1, TM, 256), lambda b, j: (b, order(j), 0)),
                pl.BlockSpec((1, 256, TM), lambda b, j: (b, 0, order(j))),
                pl.BlockSpec((1, TM, 512), lambda b, j: (b, order(j), 0)),
                pl.BlockSpec((1, TM, LANE), lambda b, j: (b, order(j), 0)),
                pl.BlockSpec((1, 16, TM), lambda b, j: (b, 0, order(j)))]

    return pl.pallas_call(
        _mlstm_kernel,
        out_shape=[jax.ShapeDtypeStruct((B, S, 512), F32)] * 2,
        grid=(B, nblk),
        in_specs=specs(fwd) + specs(bwd) + [_const_spec(brow.shape), _const_spec(bcol.shape)],
        out_specs=[pl.BlockSpec((1, TM, 512), lambda b, j: (b, fwd(j), 0)),
                   pl.BlockSpec((1, TM, 512), lambda b, j: (b, bwd(j), 0))],
        scratch_shapes=[pltpu.VMEM((2, ML_HEADS // 2, 2 * ML_DQK, 2 * ML_DV), F32),
                        pltpu.VMEM((2 * ML_HEADS, 8, LANE), F32)],
        compiler_params=_params("arbitrary", "arbitrary"), name="mlstm",
    )(q, kt, v, gcol, grow, q, kt, v, gcol, grow, brow, bcol)


def _softmax_pv_t(parts):
    ss = []
    for k, q, _, bias in parts:
        s = _dot_nt(k, q)
        ss.append(s if bias is None else s + bias)
    m = functools.reduce(jnp.maximum, [jnp.max(s, axis=0, keepdims=True) for s in ss])
    ps = [jnp.exp2(s - m) for s in ss]
    l = sum(jnp.sum(p, axis=0, keepdims=True) for p in ps)
    o = sum(_dot(vt, p.astype(BF16)) for (_, _, vt, _), p in zip(parts, ps))
    return o / l


def _mla_kernel(bounded_ref, q_ref, k_ref, vt_ref, o_ref, s_scr, p_scr, acc_scr, *, chunks):
    @pl.when(bounded_ref[0] != 0)
    def _():
        _mla_bounded(q_ref, k_ref, vt_ref, o_ref, p_scr, acc_scr, chunks)

    @pl.when(bounded_ref[0] == 0)
    def _():
        _mla_online(q_ref, k_ref, vt_ref, o_ref, s_scr, p_scr, acc_scr, chunks)


def _mla_bounded(q_ref, k_ref, vt_ref, o_ref, p_scr, acc_scr, chunks):
    n_heads = q_ref.shape[1]
    items = [(hh, ci) for hh in range(n_heads) for ci in range(len(chunks))]

    def probs(n):
        hh, ci = items[n]
        off, size = chunks[ci]
        p = jnp.exp2(_dot_nt(k_ref[0, hh, off:off + size, :], q_ref[0, hh]))
        p_scr[n % 2, 0:size, :] = p.astype(BF16)
        return jnp.sum(p, axis=0, keepdims=True)

    lsum = probs(0)
    l = None
    for n, (hh, ci) in enumerate(items):
        off, size = chunks[ci]
        l = lsum if ci == 0 else l + lsum
        if n + 1 < len(items):
            lsum = probs(n + 1)
        pv = _dot(vt_ref[0, hh * MLA_V:(hh + 1) * MLA_V, off:off + size], p_scr[n % 2, 0:size, :])
        if ci == 0:
            acc_scr[hh] = pv
        else:
            acc_scr[hh] += pv
        if ci == len(chunks) - 1:
            o_ref[0, hh * MLA_V:(hh + 1) * MLA_V, :] = (acc_scr[hh] / l).astype(BF16)


def _mla_online(q_ref, k_ref, vt_ref, o_ref, s_scr, p_scr, acc_scr, chunks):
    n_heads, tq = q_ref.shape[1], q_ref.shape[2]
    items = [(hh, ci) for hh in range(n_heads) for ci in range(len(chunks))]

    def scores(n):
        hh, ci = items[n]
        off, size = chunks[ci]
        s_scr[n % 2, 0:size, :] = _dot_nt(k_ref[0, hh, off:off + size, :], q_ref[0, hh])

    scores(0)
    m = l = None
    for n, (hh, ci) in enumerate(items):
        off, size = chunks[ci]
        slot = n % 2
        if n + 1 < len(items):
            scores(n + 1)
        if ci == 0:
            m = [jnp.full((1, LANE), -jnp.inf, F32)] * (tq // LANE)
            l = [jnp.zeros((1, LANE), F32)] * (tq // LANE)
        alphas = []
        for st in range(tq // LANE):
            cs = slice(st * LANE, (st + 1) * LANE)
            m_new = jnp.maximum(m[st], jnp.max(s_scr[slot, 0:size, cs], axis=0, keepdims=True))
            alpha = jnp.exp2(m[st] - m_new)
            p = jnp.exp2(s_scr[slot, 0:size, cs] - m_new)
            l[st] = alpha * l[st] + jnp.sum(p, axis=0, keepdims=True)
            m[st] = m_new
            p_scr[slot, 0:size, cs] = p.astype(BF16)
            alphas.append(alpha)
        pv = _dot(vt_ref[0, hh * MLA_V:(hh + 1) * MLA_V, off:off + size], p_scr[slot, 0:size, :])
        if ci == 0:
            acc_scr[hh] = pv
        else:
            acc_scr[hh] = jnp.concatenate(alphas, axis=1) * acc_scr[hh] + pv
        if ci == len(chunks) - 1:
            o_ref[0, hh * MLA_V:(hh + 1) * MLA_V, :] = (acc_scr[hh] / jnp.concatenate(l, axis=1)).astype(BF16)


def _mla(bounded, qm, km, vmt, n_lat):
    B, H, S, _ = qm.shape
    kv, hp = 512, 2
    chunks = tuple((i * kv, kv) for i in range(n_lat // kv)) + ((n_lat, S - n_lat),)
    return pl.pallas_call(
        functools.partial(_mla_kernel, chunks=chunks),
        out_shape=jax.ShapeDtypeStruct((B, H * MLA_V, n_lat), BF16),
        grid=(B, H // hp, n_lat // MLA_TQ),
        in_specs=[pl.BlockSpec(memory_space=pltpu.SMEM),
                  pl.BlockSpec((1, hp, MLA_TQ, LANE), lambda b, h, i: (b, h, i, 0)),
                  pl.BlockSpec((1, hp, S, LANE), lambda b, h, i: (b, h, 0, 0)),
                  pl.BlockSpec((1, hp * MLA_V, S), lambda b, h, i: (b, h, 0))],
        out_specs=pl.BlockSpec((1, hp * MLA_V, MLA_TQ), lambda b, h, i: (b, h, i)),
        scratch_shapes=[pltpu.VMEM((2, kv, MLA_TQ), F32), pltpu.VMEM((2, kv, MLA_TQ), BF16),
                        pltpu.VMEM((hp, MLA_V, MLA_TQ), F32)],
        compiler_params=_params("arbitrary", "arbitrary", "arbitrary"), name="mla_attn",
    )(bounded, qm, km, vmt)


def _natten_kernel(bounded_ref, q_ref, k_ref, vt_ref, bias_ref, o_ref, s_scr, p_scr, *, n_lat, n_ctx, rows):
    blk = pl.program_id(1)
    band = NA_BAND_ROWS * GRID_W
    nk = band + n_ctx
    tq = q_ref.shape[2]
    u0 = jnp.clip(blk * NA_Q_ROWS - NA_WIN_R // 2, 0, rows - NA_BAND_ROWS)
    koff = pl.multiple_of(u0 * GRID_W, 2 * LANE)
    lane = lax.broadcasted_iota(jnp.int32, (1, LANE), 1)
    half_mask = [(lane < NA_DH).astype(F32).astype(BF16), (lane >= NA_DH).astype(F32).astype(BF16)]

    def scores(h):
        qh = q_ref[0, h // 2] * half_mask[h % 2]
        return (_dot_nt(k_ref[0, h // 2, pl.ds(koff, band), :], qh) + bias_ref[h, 0],
                _dot_nt(k_ref[0, h // 2, n_lat:n_lat + n_ctx, :], qh))

    def finish(h, l):
        slot = h % 2
        hs = slice(h * NA_DH, (h + 1) * NA_DH)
        o = (_dot(vt_ref[0, hs, pl.ds(koff, band)], p_scr[slot, 0:band, :])
             + _dot(vt_ref[0, hs, n_lat:n_lat + n_ctx], p_scr[slot, band:nk, :]))
        o_ref[0, hs, :] = (o / l).astype(BF16)

    @pl.when(bounded_ref[0] != 0)
    def _():
        def probs(h):
            pb, pc = (jnp.exp2(s) for s in scores(h))
            p_scr[h % 2, 0:band, :] = pb.astype(BF16)
            p_scr[h % 2, band:nk, :] = pc.astype(BF16)
            return jnp.sum(pb, axis=0, keepdims=True) + jnp.sum(pc, axis=0, keepdims=True)

        l = probs(0)
        for h in range(NA_HEADS):
            l_next = probs(h + 1) if h + 1 < NA_HEADS else None
            finish(h, l)
            l = l_next

    @pl.when(bounded_ref[0] == 0)
    def _():
        def stage(h):
            s_scr[h % 2, 0:band, :], s_scr[h % 2, band:nk, :] = scores(h)

        stage(0)
        for h in range(NA_HEADS):
            slot = h % 2
            if h + 1 < NA_HEADS:
                stage(h + 1)
            ls = []
            for st in range(tq // LANE):
                cs = slice(st * LANE, (st + 1) * LANE)
                m = jnp.max(s_scr[slot, :, cs], axis=0, keepdims=True)
                p = jnp.exp2(s_scr[slot, :, cs] - m)
                ls.append(jnp.sum(p, axis=0, keepdims=True))
                p_scr[slot, :, cs] = p.astype(BF16)
            finish(h, jnp.concatenate(ls, axis=1))


def _natten(bounded, qn, kn, vnt, bias, n_lat):
    B, P, S, _ = qn.shape
    rows = n_lat // GRID_W
    nblk = rows // NA_Q_ROWS
    tq = NA_Q_ROWS * GRID_W
    band = NA_BAND_ROWS * GRID_W
    nk = band + S - n_lat
    btype = lambda i: jnp.where(i == 0, 0, jnp.where(i == nblk - 1, 2, 1))
    return pl.pallas_call(
        functools.partial(_natten_kernel, n_lat=n_lat, n_ctx=S - n_lat, rows=rows),
        out_shape=jax.ShapeDtypeStruct((B, NA_HEADS * NA_DH, n_lat), BF16),
        grid=(B, nblk),
        in_specs=[pl.BlockSpec(memory_space=pltpu.SMEM),
                  pl.BlockSpec((1, P, tq, LANE), lambda b, i: (b, 0, i, 0)),
                  pl.BlockSpec((1, P, S, LANE), lambda b, i: (b, 0, 0, 0)),
                  pl.BlockSpec((1, NA_HEADS * NA_DH, S), lambda b, i: (b, 0, 0)),
                  pl.BlockSpec((NA_HEADS, 1, band, tq), lambda b, i: (0, btype(i), 0, 0))],
        out_specs=pl.BlockSpec((1, NA_HEADS * NA_DH, tq), lambda b, i: (b, 0, i)),
        scratch_shapes=[pltpu.VMEM((2, nk, tq), F32), pltpu.VMEM((2, nk, tq), BF16)],
        compiler_params=_params("arbitrary", "arbitrary"), name="natten",
    )(bounded, qn, kn, vnt, bias)


def _na_bias_kernel(rpb_ref, o_ref, bc_ref, *, rows):
    l, h = pl.program_id(0), pl.program_id(1)
    tq = NA_Q_ROWS * GRID_W
    kc = lax.broadcasted_iota(jnp.int32, (GRID_W, tq), 0)
    ql = lax.broadcasted_iota(jnp.int32, (GRID_W, tq), 1)
    c = ql & (GRID_W - 1)
    qi = ql >> 6
    dc = kc - c + (NA_WIN_C - 1)
    cs = jnp.clip(c - NA_WIN_C // 2, 0, GRID_W - NA_WIN_C)
    col_ok = (kc >= cs) & (kc < cs + NA_WIN_C)
    n_dr, n_dc = 2 * NA_WIN_R - 1, 2 * NA_WIN_C - 1
    base = (l * NA_HEADS + h) * (n_dr * n_dc)
    for dr in range(n_dr):
        acc = jnp.zeros((GRID_W, tq), F32)
        for j in range(n_dc):
            acc = jnp.where(dc == j, rpb_ref[base + dr * n_dc + j], acc)
        bc_ref[dr] = jnp.where(col_ok, acc * LOG2E, NEG_INF)
    nblk = rows // NA_Q_ROWS
    for t, blk in enumerate((0, 1, nblk - 1)):
        r0 = blk * NA_Q_ROWS
        u0 = min(max(r0 - NA_WIN_R // 2, 0), rows - NA_BAND_ROWS)
        for j in range(NA_BAND_ROWS):
            tile = jnp.full((GRID_W, tq), NEG_INF, F32)
            for i in range(NA_Q_ROWS):
                r, kr = r0 + i, u0 + j
                rs = min(max(r - NA_WIN_R // 2, 0), rows - NA_WIN_R)
                if rs <= kr < rs + NA_WIN_R:
                    tile = jnp.where(qi == i, bc_ref[kr - r + NA_WIN_R - 1], tile)
            o_ref[0, 0, t, j * GRID_W:(j + 1) * GRID_W, :] = tile


def _na_bias(rpb, rows):
    L = rpb.shape[0]
    tq = NA_Q_ROWS * GRID_W
    band = NA_BAND_ROWS * GRID_W
    return pl.pallas_call(
        functools.partial(_na_bias_kernel, rows=rows),
        out_shape=jax.ShapeDtypeStruct((L, NA_HEADS, 3, band, tq), F32),
        grid=(L, NA_HEADS),
        in_specs=[pl.BlockSpec(memory_space=pltpu.SMEM)],
        out_specs=pl.BlockSpec((1, 1, 3, band, tq), lambda l, h: (l, h, 0, 0, 0)),
        scratch_shapes=[pltpu.VMEM((2 * NA_WIN_R - 1, GRID_W, tq), F32)],
        compiler_params=_params("arbitrary", "arbitrary"), name="na_bias",
    )(rpb.reshape(-1))


def _ctx_attn_kernel(qm_ref, km_ref, vmt_ref, qn_ref, kn_ref, vnt_ref, mo_ref, no_ref):
    lane = lax.broadcasted_iota(jnp.int32, (1, LANE), 1)
    for hh in range(2):
        hs = slice(hh * MLA_V, (hh + 1) * MLA_V)
        mo_ref[0, hs, :] = _softmax_pv_t([(km_ref[0, hh], qm_ref[0, hh], vmt_ref[0, hs, :], None)]).astype(BF16)
        qh = qn_ref[0, 0] * ((lane < NA_DH) if hh == 0 else (lane >= NA_DH)).astype(F32).astype(BF16)
        no_ref[0, hs, :] = _softmax_pv_t([(kn_ref[0, 0], qh, vnt_ref[0, hs, :], None)]).astype(BF16)


def _ctx_attn(qm, km, vmt, qn, kn, vnt, n_lat):
    B, _, S, _ = qm.shape
    n_ctx = S - n_lat
    t = n_lat // n_ctx
    head = lambda b, p: (b, p, t, 0)
    rowsT = lambda b, p: (b, p, t)
    out = lambda b, p: (b, p, 0)
    return pl.pallas_call(
        _ctx_attn_kernel,
        out_shape=[jax.ShapeDtypeStruct((B, MLA_HEADS * MLA_V, n_ctx), BF16),
                   jax.ShapeDtypeStruct((B, NA_HEADS * NA_DH, n_ctx), BF16)],
        grid=(B, NA_HEADS // 2),
        in_specs=[pl.BlockSpec((1, 2, n_ctx, LANE), head), pl.BlockSpec((1, 2, n_ctx, LANE), head),
                  pl.BlockSpec((1, 2 * MLA_V, n_ctx), rowsT),
                  pl.BlockSpec((1, 1, n_ctx, LANE), head), pl.BlockSpec((1, 1, n_ctx, LANE), head),
                  pl.BlockSpec((1, 2 * NA_DH, n_ctx), rowsT)],
        out_specs=[pl.BlockSpec((1, 2 * MLA_V, n_ctx), out), pl.BlockSpec((1, 2 * NA_DH, n_ctx), out)],
        compiler_params=_params("arbitrary", "arbitrary"), name="ctx_attn",
    )(qm, km, vmt, qn, kn, vnt)


def _merge_kernel(x_ref, xc_ref, mod_ref, g1_ref, hf_ref, hb_ref, o_ref, mo_ref, no_ref, moc_ref, noc_ref, wg_ref, gout_ref,
                  wml_ref, wmla_ref, wna_ref, wout_ref, x1_ref, *, n_lat_tiles):
    is_ctx = pl.program_id(1) >= n_lat_tiles
    x = jnp.where(is_ctx, xc_ref[0], x_ref[0])
    m = mod_ref[0]
    hb = (_rms(x, g1_ref[...], D_MODEL) * (1.0 + m[1:2]) + m[0:1]).astype(BF16)
    gates = jax.nn.sigmoid(_dot(hb, wg_ref[...]))
    hs = hf_ref[0] + hb_ref[0]
    og = jax.nn.sigmoid(o_ref[0])
    gout = gout_ref[...]
    hn = jnp.concatenate(
        [_rms(hs[:, h * ML_DV:(h + 1) * ML_DV], gout[:, h * ML_DV:(h + 1) * ML_DV], ML_DV) for h in range(ML_HEADS)],
        axis=1)
    y_a = _dot((hn * og).astype(BF16), wml_ref[...])
    y_b = _dot_tn(jnp.where(is_ctx, moc_ref[0], mo_ref[0]), wmla_ref[...])
    y_c = _dot_tn(jnp.where(is_ctx, noc_ref[0], no_ref[0]), wna_ref[...])
    mg = gates[:, 0:D_MODEL] * y_a + gates[:, D_MODEL:2 * D_MODEL] * y_b + gates[:, 2 * D_MODEL:] * y_c
    x1_ref[0] = x + m[2:3] * _dot(mg.astype(BF16), wout_ref[...])


def _merge(x, xc, ctx_block, mods, g1, hf, hb, o, mla_o, na_o, mla_oc, na_oc, w, nt, n_lat_tiles):
    B = x.shape[0]
    tok = lambda b, t: (b, t, 0)
    lat_t = lambda b, t: (b, 0, jnp.minimum(t, n_lat_tiles - 1))
    ctx_t = lambda b, t: (b, 0, 0)
    in_specs = _stream_specs(ctx_block, n_lat_tiles)
    in_specs += [pl.BlockSpec((1, 8, D_MODEL), lambda b, t: (jnp.where(t >= n_lat_tiles, B, b), 0, 0)),
                _const_spec(g1.shape),
                pl.BlockSpec((1, TM, 512), tok), pl.BlockSpec((1, TM, 512), tok), pl.BlockSpec((1, TM, 512), tok),
                pl.BlockSpec((1, 512, TM), lat_t), pl.BlockSpec((1, 512, TM), lat_t),
                pl.BlockSpec((1, 512, TM), ctx_t), pl.BlockSpec((1, 512, TM), ctx_t)]
    names = ("wg", "gout", "wml", "wmla", "wna", "wout")
    in_specs += [_const_spec(w[k].shape) for k in names]
    return pl.pallas_call(
        functools.partial(_merge_kernel, n_lat_tiles=n_lat_tiles),
        out_shape=jax.ShapeDtypeStruct((B, nt * TM, D_MODEL), F32),
        grid=(B, nt), in_specs=in_specs, out_specs=pl.BlockSpec((1, TM, D_MODEL), tok),
        compiler_params=_params("arbitrary", "arbitrary"), name="merge",
    )(x, xc, mods, g1, hf, hb, o, mla_o, na_o, mla_oc, na_oc, *[w[k] for k in names])


def _mlp_kernel(x_ref, mod_ref, g2_ref, w1_ref, w2_ref, o_ref):
    x = x_ref[0]
    m = mod_ref[0]
    hb = (_rms(x, g2_ref[...], D_MODEL) * (1.0 + m[4:5]) + m[3:4]).astype(BF16)
    acc = jnp.zeros((TM, D_MODEL), F32)
    fc = 1024
    for c in range(D_FF // fc):
        u = jnp.maximum(_dot(hb, w1_ref[:, c * fc:(c + 1) * fc]), 0.0)
        acc = acc + _dot((u * u).astype(BF16), w2_ref[c * fc:(c + 1) * fc, :])
    o_ref[0] = x + m[5:6] * acc


def _mlp(x1, mods, g2, w1, w2, nt, n_lat_tiles):
    B = x1.shape[0]
    tok = lambda b, t: (b, t, 0)
    return pl.pallas_call(
        _mlp_kernel,
        out_shape=jax.ShapeDtypeStruct((B, nt * TM, D_MODEL), F32),
        grid=(B, nt),
        in_specs=[pl.BlockSpec((1, TM, D_MODEL), tok),
                  pl.BlockSpec((1, 8, D_MODEL), lambda b, t: (jnp.where(t >= n_lat_tiles, B, b), 0, 0)),
                  _const_spec(g2.shape), _const_spec(w1.shape), _const_spec(w2.shape)],
        out_specs=pl.BlockSpec((1, TM, D_MODEL), tok),
        compiler_params=_params("arbitrary", "arbitrary"), name="mlp",
    )(x1, mods, g2, w1, w2)


def _rope_tables(n_lat, n_ctx):
    nf = MLA_ROPE // 4
    inv = jnp.power(ROPE_THETA, -jnp.arange(nf, dtype=F32) / nf)
    pos = jnp.arange(n_lat)
    ang_r = (pos // GRID_W).astype(F32)[:, None] * inv
    ang_c = (pos % GRID_W).astype(F32)[:, None] * inv
    z = jnp.zeros((n_lat, nf), F32)
    one = lambda n: jnp.ones((n_lat, n), F32)
    zero = lambda n: jnp.zeros((n_lat, n), F32)
    del z
    cr, sr, cc, sc = jnp.cos(ang_r), jnp.sin(ang_r), jnp.cos(ang_c), jnp.sin(ang_c)
    cos = jnp.concatenate([one(MLA_NOPE), cr, cr, cc, cc, one(LANE - MLA_DQK)], axis=1)
    sin = jnp.concatenate([zero(MLA_NOPE), -sr, sr, -sc, sc, zero(LANE - MLA_DQK)], axis=1)
    ident = (jnp.ones((n_ctx, LANE), F32), jnp.zeros((n_ctx, LANE), F32))
    return tuple(jnp.concatenate([t, i], axis=0) for t, i in zip((cos, sin), ident))


def _rope_partner(a):
    half = MLA_ROPE // 4
    lane = np.arange(LANE)
    r = lane - MLA_NOPE
    is_rope = (r >= 0) & (r < MLA_ROPE)
    src = np.where((r // half) % 2 == 0, lane + half, lane - half)
    return jnp.where(is_rope, a[..., np.where(is_rope, src, lane)], 0)


def _block_diag_ones(block, size=2 * LANE):
    idx = np.arange(size) // block
    return jnp.asarray(idx[:, None] == idx[None, :], BF16)


def _layer_weights(l, w_in, ml_i_bias, ml_f_bias, ml_g_out, ml_w_o, mla_g_cq, mla_w_uq, mla_g_ckv, mla_w_ukv,
                   mla_g_q, mla_g_k, mla_w_o, na_g_q, na_g_k, na_w_o, w_out, w_ff1, w_ff2):
    offs = np.cumsum(IN_SIZES)[:-1].tolist()
    mlq, mlk, mlv, mlo, mlg, dq, dkv, kr, naq, nak, nav, gates = jnp.split(w_in[l], offs, axis=-1)
    padc = lambda a, n: jnp.pad(a, ((0, 0), (0, n - a.shape[1])))
    kr_slab = jnp.pad(kr, ((0, 0), (MLA_NOPE, LANE - MLA_DQK)))
    w_uq = jnp.pad(mla_w_uq[l].reshape(MLA_Q_RANK, MLA_HEADS, MLA_DQK), ((0, 0), (0, 0), (0, LANE - MLA_DQK)))
    w_ukv = mla_w_ukv[l].reshape(MLA_KV_RANK, MLA_HEADS, MLA_NOPE + MLA_V)
    w_uk = jnp.pad(w_ukv[..., :MLA_NOPE], ((0, 0), (0, 0), (0, LANE - MLA_NOPE)))
    w_uv = w_ukv[..., MLA_NOPE:].reshape(MLA_KV_RANK, MLA_HEADS * MLA_V)
    row = lambda a: a.reshape(1, -1).astype(F32)
    bias16 = jnp.stack([ml_i_bias[l, 0], ml_f_bias[l, 0], ml_i_bias[l, 1], ml_f_bias[l, 1]]).reshape(-1).astype(F32)
    return {
        "wa": jnp.concatenate([mlq * (ML_DQK ** -0.5), mlv, mlo, padc(mlg, LANE)], axis=1).astype(BF16),
        "wat": jnp.concatenate([mlk.T, mlg.T], axis=0).astype(BF16),
        "wb": jnp.concatenate([dq, dkv, kr_slab, _rope_partner(kr_slab)], axis=1).astype(BF16),
        "wc": jnp.concatenate([naq, nak], axis=1).astype(BF16),
        "wct": nav.T.astype(BF16),
        "wuq": w_uq.reshape(MLA_Q_RANK, MLA_HEADS * LANE).astype(BF16),
        "wuqs": _rope_partner(w_uq).reshape(MLA_Q_RANK, MLA_HEADS * LANE).astype(BF16),
        "wuk": w_uk.reshape(MLA_KV_RANK, MLA_HEADS * LANE).astype(BF16),
        "wuvt": w_uv.T.astype(BF16),
        "gcq": row(mla_g_cq[l]), "gckv": row(mla_g_ckv[l]),
        "gq": padc(row(mla_g_q[l]) * (MLA_DQK ** -0.5 * LOG2E), LANE), "gk": padc(row(mla_g_k[l]), LANE),
        "ngq": row(jnp.tile(na_g_q[l], 2)) * (NA_DH ** -0.5 * LOG2E), "ngk": row(jnp.tile(na_g_k[l], 2)),
        "ng": jnp.concatenate([row(jnp.tile(na_g_q[l], NA_HEADS)) * (NA_DH ** -0.5 * LOG2E),
                               row(jnp.tile(na_g_k[l], NA_HEADS))], axis=1),
        "bd128": _block_diag_ones(LANE), "bd64": _block_diag_ones(NA_DH),
        "brow": padc(bias16.reshape(1, 16), LANE), "bcol": bias16.reshape(16, 1),
        "wg": gates.astype(BF16), "gout": row(ml_g_out[l]),
        "wml": ml_w_o[l].astype(BF16), "wmla": mla_w_o[l].astype(BF16), "wna": na_w_o[l].astype(BF16),
        "wout": w_out[l].astype(BF16), "w1": w_ff1[l].astype(BF16), "w2": w_ff2[l].astype(BF16),
    }


def kernel(x, c, ctx, c_ctx, w_mod, b_mod, g_norm1, g_norm2, w_in, ml_i_bias, ml_f_bias, ml_g_out, ml_w_o, mla_g_cq, mla_w_uq, mla_g_ckv, mla_w_ukv, mla_g_q, mla_g_k, mla_w_o, na_g_q, na_g_k, na_rpb, na_w_o, w_out, w_ff1, w_ff2):
    B, T, D = x.shape
    C = ctx.shape[1]
    depth = w_in.shape[0]
    assert D == D_MODEL and C == TM and T % MLA_TQ == 0 and B < 16
    rows = T // GRID_W
    assert rows % NA_Q_ROWS == 0 and rows >= NA_BAND_ROWS
    n_lat_tiles = T // TM

    cc = jnp.zeros((16, D), F32).at[:B].set(c).at[B].set(c_ctx)
    mod = _modulation(cc, w_mod, b_mod).reshape(depth, 16, 6, D)
    mod = jnp.pad(mod, ((0, 0), (0, 0), (0, 2), (0, 0)))
    tabs = _rope_tables(T, C)
    na_bias = _na_bias(na_rpb.astype(F32), rows)
    stream = (x, ctx, 0)

    for l in range(depth):
        last = l == depth - 1
        w = _layer_weights(l, w_in, ml_i_bias, ml_f_bias, ml_g_out, ml_w_o, mla_g_cq, mla_w_uq, mla_g_ckv,
                           mla_w_ukv, mla_g_q, mla_g_k, mla_w_o, na_g_q, na_g_k, na_w_o, w_out, w_ff1, w_ff2)
        g1 = g_norm1[l].reshape(1, D).astype(F32)
        g2 = g_norm2[l].reshape(1, D).astype(F32)
        cos, sin = tabs
        gain_tabs = (cos * w["gq"], sin * _rope_partner(w["gq"]), cos * w["gk"], sin * _rope_partner(w["gk"]))
        q, kt, v, o, gcol, grow, qm, km, vmt, qn, kn, vnt = _inproj(*stream, mod[l], g1, w, gain_tabs, n_lat_tiles)
        hf, hb = _mlstm(q, kt, v, gcol, grow, w["brow"], w["bcol"], n_lat_tiles)
        mla_bound = MLA_DQK * jnp.max(jnp.abs(w["gq"])) * jnp.max(jnp.abs(w["gk"]))
        na_bound = (NA_DH * jnp.max(jnp.abs(w["ngq"])) * jnp.max(jnp.abs(w["ngk"]))
                    + LOG2E * jnp.max(jnp.abs(na_rpb[l].astype(F32))))
        mla_o = _mla((mla_bound <= SCORE_BOUND_LIMIT).astype(jnp.int32).reshape(1), qm, km, vmt, T)
        na_o = _natten((na_bound <= SCORE_BOUND_LIMIT).astype(jnp.int32).reshape(1), qn, kn, vnt, na_bias[l], T)
        mla_oc, na_oc = (mla_o, na_o) if last else _ctx_attn(qm, km, vmt, qn, kn, vnt, T)
        nt = n_lat_tiles if last else n_lat_tiles + 1
        x1 = _merge(*stream, mod[l], g1, hf, hb, o, mla_o, na_o, mla_oc, na_oc, w, nt, n_lat_tiles)
        xs = _mlp(x1, mod[l], g2, w["w1"], w["w2"], nt, n_lat_tiles)
        stream = (xs, xs, n_lat_tiles)
    return xs
```

```python
import functools

import numpy as np
import jax
import jax.numpy as jnp
from jax import lax
from jax.experimental import pallas as pl
from jax.experimental.pallas import tpu as pltpu

F32 = jnp.float32
BF16 = jnp.bfloat16

D_MODEL = 1024
GRID_W = 64
ML_HEADS, ML_DQK, ML_DV = 4, 64, 128
ML_CHUNK = 128
ML_SAMPLES = 1
MLA_HEADS, MLA_Q_RANK, MLA_KV_RANK, MLA_NOPE, MLA_ROPE, MLA_V = 8, 384, 256, 64, 32, 64
MLA_DQK = MLA_NOPE + MLA_ROPE
NA_HEADS, NA_DH, NA_WIN_R, NA_WIN_C = 8, 64, 8, 16
NA_Q_ROWS = 4
NA_BAND_ROWS = NA_Q_ROWS + NA_WIN_R
D_FF = 4 * D_MODEL
ROPE_THETA = 10000.0
EPS = 1e-6
LOG2E = 1.4426950408889634
SCORE_BOUND_LIMIT = 60.0
NEG_INF = -1e30
IN_SIZES = (256, 256, 512, 512, 16, MLA_Q_RANK, MLA_KV_RANK, MLA_ROPE, 512, 512, 512, 3 * D_MODEL)
LANE = 128
TM = 256
MLA_TQ = 1024
VMEM_LIMIT = 56 * 1024 * 1024


def _dot(a, b):
    return jnp.dot(a, b, preferred_element_type=F32)


def _dot_nt(a, b):
    return lax.dot_general(a, b, (((1,), (1,)), ((), ())), preferred_element_type=F32)


def _dot_tn(a, b):
    return lax.dot_general(a, b, (((0,), (0,)), ((), ())), preferred_element_type=F32)


def _rms(x, g, n):
    ms = jnp.sum(x * x, axis=-1, keepdims=True) * (1.0 / n)
    return x * lax.rsqrt(ms + EPS) * g


def _split3(x):
    hi = x.astype(BF16)
    r = x - hi.astype(F32)
    mid = r.astype(BF16)
    lo = (r - mid.astype(F32)).astype(BF16)
    return hi, mid, lo


def _params(*sem):
    return pltpu.CompilerParams(dimension_semantics=sem, vmem_limit_bytes=VMEM_LIMIT)


def _const_spec(shape):
    nd = len(shape)
    return pl.BlockSpec(shape, lambda *_: (0,) * nd, pipeline_mode=pl.Buffered(1))


def _mod_kernel(c_ref, w_ref, b_ref, o_ref):
    c = c_ref[...]
    s = c * jax.nn.sigmoid(c)
    o_ref[0] = _dot(s, w_ref[0]) + b_ref[0]


def _modulation(cc, w_mod, b_mod):
    L = w_mod.shape[0]
    tn = 1024
    return pl.pallas_call(
        _mod_kernel,
        out_shape=jax.ShapeDtypeStruct((L, 16, 6 * D_MODEL), F32),
        grid=(L, 6 * D_MODEL // tn),
        in_specs=[pl.BlockSpec((16, D_MODEL), lambda l, n: (0, 0)),
                  pl.BlockSpec((1, D_MODEL, tn), lambda l, n: (l, 0, n)),
                  pl.BlockSpec((1, 1, tn), lambda l, n: (l, 0, n))],
        out_specs=pl.BlockSpec((1, 16, tn), lambda l, n: (l, 0, n)),
        compiler_params=_params("arbitrary", "arbitrary"),
        name="modulation",
    )(cc, w_mod, b_mod.reshape(L, 1, 6 * D_MODEL))


def _head_inv_rms(y, bd, n):
    y2 = (y * y).astype(BF16)
    w = bd.shape[0]
    ss = jnp.concatenate([_dot(y2[:, j:j + w], bd) for j in range(0, y.shape[1], w)], axis=1)
    return lax.rsqrt(ss * (1.0 / n) + EPS)


def _inproj_kernel(x_ref, xc_ref, mod_ref, g1_ref, wa_ref, wat_ref, wb_ref, wc_ref, wct_ref, wuq_ref, wuqs_ref, wuk_ref,
                   wuvt_ref, gcq_ref, gckv_ref, ng_ref, bd128_ref, bd64_ref, cgq_ref, sgq_ref, cgk_ref, sgk_ref,
                   q_ref, kt_ref, v_ref, o_ref, gcol_ref, grow_ref, qm_ref, km_ref, vmt_ref, qn_ref, kn_ref, vnt_ref,
                   *, n_lat_tiles):
    x = jnp.where(pl.program_id(1) >= n_lat_tiles, xc_ref[0], x_ref[0])
    m = mod_ref[0]
    hb = (_rms(x, g1_ref[...], D_MODEL) * (1.0 + m[1:2]) + m[0:1]).astype(BF16)

    a = _dot(hb, wa_ref[...])
    q_ref[0] = a[:, 0:256].astype(BF16)
    v_ref[0] = a[:, 256:768].astype(BF16)
    o_ref[0] = a[:, 768:1280]
    gcol_ref[0] = a[:, 1280:1408]
    at = _dot_nt(wat_ref[...], hb)
    kt_ref[0] = at[0:256].astype(BF16)
    grow_ref[0] = at[256:272]

    bs = _dot(hb, wb_ref[...])
    dqn = _rms(bs[:, 0:MLA_Q_RANK], gcq_ref[...], MLA_Q_RANK).astype(BF16)
    dkvn = _rms(bs[:, MLA_Q_RANK:MLA_Q_RANK + MLA_KV_RANK], gckv_ref[...], MLA_KV_RANK).astype(BF16)
    kr = bs[:, MLA_Q_RANK + MLA_KV_RANK:MLA_Q_RANK + MLA_KV_RANK + LANE]
    kr_sw = bs[:, MLA_Q_RANK + MLA_KV_RANK + LANE:]
    qh_all = _dot(dqn, wuq_ref[...])
    qsw_all = _dot(dqn, wuqs_ref[...])
    kh_all = _dot(dkvn, wuk_ref[...]) + jnp.concatenate([kr] * MLA_HEADS, axis=1)
    inv_q = _head_inv_rms(qh_all, bd128_ref[...], MLA_DQK)
    inv_k = _head_inv_rms(kh_all, bd128_ref[...], MLA_DQK)
    cgq, sgq, cgk, sgk = cgq_ref[...], sgq_ref[...], cgk_ref[...], sgk_ref[...]
    for h in range(MLA_HEADS):
        sl = slice(h * LANE, (h + 1) * LANE)
        qm_ref[0, h] = ((qh_all[:, sl] * cgq + qsw_all[:, sl] * sgq) * inv_q[:, sl]).astype(BF16)
        km_ref[0, h] = ((kh_all[:, sl] * cgk + kr_sw * sgk) * inv_k[:, sl]).astype(BF16)
    vmt_ref[0] = _dot_nt(wuvt_ref[...], dkvn).astype(BF16)

    c = _dot(hb, wc_ref[...])
    cn = c * _head_inv_rms(c, bd64_ref[...], NA_DH) * ng_ref[...]
    for p in range(NA_HEADS // 2):
        qn_ref[0, p] = cn[:, p * LANE:(p + 1) * LANE].astype(BF16)
        kn_ref[0, p] = cn[:, NA_HEADS * NA_DH + p * LANE:NA_HEADS * NA_DH + (p + 1) * LANE].astype(BF16)
    vnt_ref[0] = _dot_nt(wct_ref[...], hb).astype(BF16)


def _stream_specs(ctx_block, n_lat_tiles):
    return [pl.BlockSpec((1, TM, D_MODEL), lambda b, t: (b, jnp.minimum(t, n_lat_tiles - 1), 0)),
            pl.BlockSpec((1, TM, D_MODEL), lambda b, t: (b, ctx_block, 0))]


def _inproj(x, xc, ctx_block, mods, g1, w, tabs, n_lat_tiles):
    B = x.shape[0]
    nt = n_lat_tiles + 1
    S = nt * TM
    tok = lambda b, t: (b, t, 0)
    tokT = lambda b, t: (b, 0, t)
    head = lambda b, t: (b, 0, t, 0)
    in_specs = _stream_specs(ctx_block, n_lat_tiles)
    in_specs += [pl.BlockSpec((1, 8, D_MODEL), lambda b, t: (jnp.where(t >= n_lat_tiles, B, b), 0, 0)),
                 _const_spec(g1.shape)]
    consts = ("wa", "wat", "wb", "wc", "wct", "wuq", "wuqs", "wuk", "wuvt", "gcq", "gckv", "ng", "bd128", "bd64")
    in_specs += [_const_spec(w[k].shape) for k in consts]
    in_specs += [pl.BlockSpec((TM, LANE), lambda b, t: (t, 0))] * 4
    out_shape = [jax.ShapeDtypeStruct((B, S, 256), BF16),
                 jax.ShapeDtypeStruct((B, 256, S), BF16),
                 jax.ShapeDtypeStruct((B, S, 512), BF16),
                 jax.ShapeDtypeStruct((B, S, 512), F32),
                 jax.ShapeDtypeStruct((B, S, LANE), F32),
                 jax.ShapeDtypeStruct((B, 16, S), F32),
                 jax.ShapeDtypeStruct((B, MLA_HEADS, S, LANE), BF16),
                 jax.ShapeDtypeStruct((B, MLA_HEADS, S, LANE), BF16),
                 jax.ShapeDtypeStruct((B, MLA_HEADS * MLA_V, S), BF16),
                 jax.ShapeDtypeStruct((B, NA_HEADS // 2, S, LANE), BF16),
                 jax.ShapeDtypeStruct((B, NA_HEADS // 2, S, LANE), BF16),
                 jax.ShapeDtypeStruct((B, NA_HEADS * NA_DH, S), BF16)]
    out_specs = [pl.BlockSpec((1, TM, 256), tok), pl.BlockSpec((1, 256, TM), tokT),
                 pl.BlockSpec((1, TM, 512), tok), pl.BlockSpec((1, TM, 512), tok),
                 pl.BlockSpec((1, TM, LANE), tok), pl.BlockSpec((1, 16, TM), tokT),
                 pl.BlockSpec((1, MLA_HEADS, TM, LANE), head), pl.BlockSpec((1, MLA_HEADS, TM, LANE), head),
                 pl.BlockSpec((1, MLA_HEADS * MLA_V, TM), tokT),
                 pl.BlockSpec((1, NA_HEADS // 2, TM, LANE), head), pl.BlockSpec((1, NA_HEADS // 2, TM, LANE), head),
                 pl.BlockSpec((1, NA_HEADS * NA_DH, TM), tokT)]
    return pl.pallas_call(
        functools.partial(_inproj_kernel, n_lat_tiles=n_lat_tiles),
        out_shape=out_shape, grid=(B, nt), in_specs=in_specs, out_specs=out_specs,
        compiler_params=_params("arbitrary", "arbitrary"), name="inproj",
    )(x, xc, mods, g1, *[w[k] for k in consts], *tabs)


def _log_sigmoid(x):
    return jnp.minimum(x, 0.0) - jnp.log1p(jnp.exp(-jnp.abs(x)))


def _mlstm_kernel(qf_ref, ktf_ref, vf_ref, gcf_ref, grf_ref, qb_ref, ktb_ref, vb_ref, gcb_ref, grb_ref,
                  brow_ref, bcol_ref, hf_ref, hb_ref, c_ref, m_ref):
    @pl.when(pl.program_id(1) == 0)
    def _():
        c_ref[...] = jnp.zeros_like(c_ref)
        m_ref[...] = jnp.zeros_like(m_ref)

    dirs = ((qf_ref, ktf_ref, vf_ref, gcf_ref, grf_ref, hf_ref), (qb_ref, ktb_ref, vb_ref, gcb_ref, grb_ref, hb_ref))
    samples = [_mlstm_sample(bb, dirs, brow_ref, bcol_ref, c_ref, m_ref) for bb in range(qf_ref.shape[0])]
    for _ in zip(*samples):
        pass


def _mlstm_sample(bb, dirs, brow_ref, bcol_ref, c_ref, m_ref):
    L = ML_CHUNK
    row = lax.broadcasted_iota(jnp.int32, (L, L), 0)
    col = lax.broadcasted_iota(jnp.int32, (L, L), 1)
    lane = lax.broadcasted_iota(jnp.int32, (1, LANE), 1)
    isf_row = (lane & ML_HEADS) != 0
    sub = lax.broadcasted_iota(jnp.int32, (16, 1), 0)
    isf_col = (sub & ML_HEADS) != 0
    half_mask = [(lane < ML_DQK).astype(F32).astype(BF16), (lane >= ML_DQK).astype(F32).astype(BF16)]
    e0 = jnp.broadcast_to((lane == 0).astype(F32).astype(BF16), (L, LANE))
    for d, (q_ref, kt_ref, v_ref, gc_ref, gr_ref, h_ref) in enumerate(dirs):
        causal = (col <= row) if d == 0 else (col >= row)
        tri = causal.astype(F32).astype(BF16)
        tri_t = ((row <= col) if d == 0 else (row >= col)).astype(F32).astype(BF16)
        local = {}
        for ch in range(TM // L):
            rs = slice(ch * L, (ch + 1) * L)
            pre_c = gc_ref[bb, rs, :] + brow_ref[...]
            val_c = jnp.where(isf_row, _log_sigmoid(pre_c), pre_c)
            pre_r = gr_ref[bb, :, rs] + bcol_ref[...]
            val_r = jnp.where(isf_col, _log_sigmoid(pre_r), pre_r)
            cum_c = sum(_dot(tri, t) for t in _split3(val_c))
            cum_r = sum(_dot(t, tri_t) for t in _split3(val_r))
            tot_r = jnp.sum(val_r, axis=1, keepdims=True)
            for h in range(ML_HEADS):
                ji, jf = (2 * d) * ML_HEADS + h, (2 * d + 1) * ML_HEADS + h
                p, half = h // 2, h % 2
                bcum_c = cum_c[:, jf:jf + 1]
                bcum_r = cum_r[jf:jf + 1, :]
                i_r = val_r[ji:ji + 1, :]
                btot = tot_r[jf:jf + 1, :]
                w_end = btot - bcum_r + i_r
                m_w = jnp.max(w_end, axis=1, keepdims=True)
                dm = jnp.where(causal, bcum_c - bcum_r + i_r, -jnp.inf)
                m_loc = jnp.max(dm, axis=1, keepdims=True)
                q_h = q_ref[bb, rs, p * LANE:(p + 1) * LANE] * half_mask[half]
                kt_pair = kt_ref[bb, p * LANE:(p + 1) * LANE, rs]
                kt_h = kt_ref[bb, h * ML_DQK:(h + 1) * ML_DQK, rs]
                v_ext = jnp.concatenate([v_ref[bb, rs, h * ML_DV:(h + 1) * ML_DV], e0], axis=1)
                s = _dot(q_h, kt_pair) * jnp.exp(dm - m_loc)
                sv = _dot(s.astype(BF16), v_ext)
                kw = (kt_h.astype(F32) * jnp.exp(w_end - m_w)).astype(BF16)
                dc = _dot(kw, v_ext)
                local[ch, h] = (q_h, bcum_c, btot, m_w, m_loc, sv, dc)
                yield
        for cc in range(TM // L):
            ch = cc if d == 0 else TM // L - 1 - cc
            rs = slice(ch * L, (ch + 1) * L)
            for h in range(ML_HEADS):
                p, half = h // 2, h % 2
                idx = d * ML_HEADS + h
                q_h, bcum_c, btot, m_w, m_loc, sv, dc = local[ch, h]
                m_old = m_ref[bb, idx, 0:1, 0:1]
                c_pair = c_ref[bb, d, p]
                inter = bcum_c + m_old
                m_t = jnp.maximum(inter, m_loc)
                tot = jnp.exp(inter - m_t) * _dot(q_h, c_pair.astype(BF16)) + jnp.exp(m_loc - m_t) * sv
                den = tot[:, ML_DV:ML_DV + 1]
                h_ref[bb, rs, h * ML_DV:(h + 1) * ML_DV] = tot[:, 0:ML_DV] / jnp.maximum(jnp.abs(den), jnp.exp(-m_t))
                m_new = jnp.maximum(btot + m_old, m_w)
                hs = slice(half * ML_DQK, (half + 1) * ML_DQK)
                c_ref[bb, d, p, hs, :] = jnp.exp(btot + m_old - m_new) * c_pair[hs, :] + jnp.exp(m_w - m_new) * dc
                m_ref[bb, idx] = jnp.broadcast_to(m_new, (8, LANE))
                yield


def _mlstm(q, kt, v, gcol, grow, brow, bcol, n_lat_tiles):
    B, S, _ = q.shape
    nblk = S // TM
    fwd = lambda j: jnp.where(j == 0, n_lat_tiles, j - 1)
    bwd = lambda j: jnp.where(j == 0, n_lat_tiles, n_lat_tiles - j)

    nb = ML_SAMPLES if B % ML_SAMPLES == 0 else 1

    def specs(order):
        return [pl.BlockSpec((nb, TM, 256), lambda b, j: (b, order(j), 0)),
                pl.BlockSpec((nb, 256, TM), lambda b, j: (b, 0, order(j))),
                pl.BlockSpec((nb, TM, 512), lambda b, j: (b, order(j), 0)),
                pl.BlockSpec((nb, TM, LANE), lambda b, j: (b, order(j), 0)),
                pl.BlockSpec((nb, 16, TM), lambda b, j: (b, 0, order(j)))]

    return pl.pallas_call(
        _mlstm_kernel,
        out_shape=[jax.ShapeDtypeStruct((B, S, 512), F32)] * 2,
        grid=(B // nb, nblk),
        in_specs=specs(fwd) + specs(bwd) + [_const_spec(brow.shape), _const_spec(bcol.shape)],
        out_specs=[pl.BlockSpec((nb, TM, 512), lambda b, j: (b, fwd(j), 0)),
                   pl.BlockSpec((nb, TM, 512), lambda b, j: (b, bwd(j), 0))],
        scratch_shapes=[pltpu.VMEM((nb, 2, ML_HEADS // 2, 2 * ML_DQK, 2 * ML_DV), F32),
                        pltpu.VMEM((nb, 2 * ML_HEADS, 8, LANE), F32)],
        compiler_params=_params("arbitrary", "arbitrary"), name="mlstm",
    )(q, kt, v, gcol, grow, q, kt, v, gcol, grow, brow, bcol)


def _softmax_pv_t(parts):
    ss = []
    for k, q, _, bias in parts:
        s = _dot_nt(k, q)
        ss.append(s if bias is None else s + bias)
    m = functools.reduce(jnp.maximum, [jnp.max(s, axis=0, keepdims=True) for s in ss])
    ps = [jnp.exp2(s - m) for s in ss]
    l = sum(jnp.sum(p, axis=0, keepdims=True) for p in ps)
    o = sum(_dot(vt, p.astype(BF16)) for (_, _, vt, _), p in zip(parts, ps))
    return o / l


def _mla_kernel(bounded_ref, q_ref, k_ref, vt_ref, o_ref, s_scr, p_scr, acc_scr, *, chunks):
    @pl.when(bounded_ref[0] != 0)
    def _():
        _mla_bounded(q_ref, k_ref, vt_ref, o_ref, p_scr, acc_scr, chunks)

    @pl.when(bounded_ref[0] == 0)
    def _():
        _mla_online(q_ref, k_ref, vt_ref, o_ref, s_scr, p_scr, acc_scr, chunks)


def _mla_bounded(q_ref, k_ref, vt_ref, o_ref, p_scr, acc_scr, chunks):
    n_heads = q_ref.shape[1]
    items = [(hh, ci) for hh in range(n_heads) for ci in range(len(chunks))]

    def probs(n):
        hh, ci = items[n]
        off, size = chunks[ci]
        p = jnp.exp2(_dot_nt(k_ref[0, hh, off:off + size, :], q_ref[0, hh]))
        p_scr[n % 2, 0:size, :] = p.astype(BF16)
        return jnp.sum(p, axis=0, keepdims=True)

    lsum = probs(0)
    l = None
    for n, (hh, ci) in enumerate(items):
        off, size = chunks[ci]
        l = lsum if ci == 0 else l + lsum
        if n + 1 < len(items):
            lsum = probs(n + 1)
        pv = _dot(vt_ref[0, hh * MLA_V:(hh + 1) * MLA_V, off:off + size], p_scr[n % 2, 0:size, :])
        if ci == 0:
            acc_scr[hh] = pv
        else:
            acc_scr[hh] += pv
        if ci == len(chunks) - 1:
            o_ref[0, hh * MLA_V:(hh + 1) * MLA_V, :] = (acc_scr[hh] / l).astype(BF16)


def _mla_online(q_ref, k_ref, vt_ref, o_ref, s_scr, p_scr, acc_scr, chunks):
    n_heads, tq = q_ref.shape[1], q_ref.shape[2]
    items = [(hh, ci) for hh in range(n_heads) for ci in range(len(chunks))]

    def scores(n):
        hh, ci = items[n]
        off, size = chunks[ci]
        s_scr[n % 2, 0:size, :] = _dot_nt(k_ref[0, hh, off:off + size, :], q_ref[0, hh])

    scores(0)
    m = l = None
    for n, (hh, ci) in enumerate(items):
        off, size = chunks[ci]
        slot = n % 2
        if n + 1 < len(items):
            scores(n + 1)
        if ci == 0:
            m = [jnp.full((1, LANE), -jnp.inf, F32)] * (tq // LANE)
            l = [jnp.zeros((1, LANE), F32)] * (tq // LANE)
        alphas = []
        for st in range(tq // LANE):
            cs = slice(st * LANE, (st + 1) * LANE)
            m_new = jnp.maximum(m[st], jnp.max(s_scr[slot, 0:size, cs], axis=0, keepdims=True))
            alpha = jnp.exp2(m[st] - m_new)
            p = jnp.exp2(s_scr[slot, 0:size, cs] - m_new)
            l[st] = alpha * l[st] + jnp.sum(p, axis=0, keepdims=True)
            m[st] = m_new
            p_scr[slot, 0:size, cs] = p.astype(BF16)
            alphas.append(alpha)
        pv = _dot(vt_ref[0, hh * MLA_V:(hh + 1) * MLA_V, off:off + size], p_scr[slot, 0:size, :])
        if ci == 0:
            acc_scr[hh] = pv
        else:
            acc_scr[hh] = jnp.concatenate(alphas, axis=1) * acc_scr[hh] + pv
        if ci == len(chunks) - 1:
            o_ref[0, hh * MLA_V:(hh + 1) * MLA_V, :] = (acc_scr[hh] / jnp.concatenate(l, axis=1)).astype(BF16)


def _mla(bounded, qm, km, vmt, n_lat):
    B, H, S, _ = qm.shape
    kv, hp = 512, 2
    chunks = tuple((i * kv, kv) for i in range(n_lat // kv)) + ((n_lat, S - n_lat),)
    return pl.pallas_call(
        functools.partial(_mla_kernel, chunks=chunks),
        out_shape=jax.ShapeDtypeStruct((B, H * MLA_V, n_lat), BF16),
        grid=(B, H // hp, n_lat // MLA_TQ),
        in_specs=[pl.BlockSpec(memory_space=pltpu.SMEM),
                  pl.BlockSpec((1, hp, MLA_TQ, LANE), lambda b, h, i: (b, h, i, 0)),
                  pl.BlockSpec((1, hp, S, LANE), lambda b, h, i: (b, h, 0, 0)),
                  pl.BlockSpec((1, hp * MLA_V, S), lambda b, h, i: (b, h, 0))],
        out_specs=pl.BlockSpec((1, hp * MLA_V, MLA_TQ), lambda b, h, i: (b, h, i)),
        scratch_shapes=[pltpu.VMEM((2, kv, MLA_TQ), F32), pltpu.VMEM((2, kv, MLA_TQ), BF16),
                        pltpu.VMEM((hp, MLA_V, MLA_TQ), F32)],
        compiler_params=_params("arbitrary", "arbitrary", "arbitrary"), name="mla_attn",
    )(bounded, qm, km, vmt)


def _natten_kernel(bounded_ref, q_ref, k_ref, vt_ref, bias_ref, o_ref, s_scr, p_scr, *, n_lat, n_ctx, rows):
    blk = pl.program_id(1)
    band = NA_BAND_ROWS * GRID_W
    nk = band + n_ctx
    tq = q_ref.shape[2]
    u0 = jnp.clip(blk * NA_Q_ROWS - NA_WIN_R // 2, 0, rows - NA_BAND_ROWS)
    koff = pl.multiple_of(u0 * GRID_W, 2 * LANE)
    lane = lax.broadcasted_iota(jnp.int32, (1, LANE), 1)
    half_mask = [(lane < NA_DH).astype(F32).astype(BF16), (lane >= NA_DH).astype(F32).astype(BF16)]

    def scores(h):
        qh = q_ref[0, h // 2] * half_mask[h % 2]
        return (_dot_nt(k_ref[0, h // 2, pl.ds(koff, band), :], qh) + bias_ref[h, 0],
                _dot_nt(k_ref[0, h // 2, n_lat:n_lat + n_ctx, :], qh))

    def finish(h, l):
        slot = h % 2
        hs = slice(h * NA_DH, (h + 1) * NA_DH)
        o = (_dot(vt_ref[0, hs, pl.ds(koff, band)], p_scr[slot, 0:band, :])
             + _dot(vt_ref[0, hs, n_lat:n_lat + n_ctx], p_scr[slot, band:nk, :]))
        o_ref[0, hs, :] = (o / l).astype(BF16)

    @pl.when(bounded_ref[0] != 0)
    def _():
        def probs(h):
            pb, pc = (jnp.exp2(s) for s in scores(h))
            p_scr[h % 2, 0:band, :] = pb.astype(BF16)
            p_scr[h % 2, band:nk, :] = pc.astype(BF16)
            return jnp.sum(pb, axis=0, keepdims=True) + jnp.sum(pc, axis=0, keepdims=True)

        l = probs(0)
        for h in range(NA_HEADS):
            l_next = probs(h + 1) if h + 1 < NA_HEADS else None
            finish(h, l)
            l = l_next

    @pl.when(bounded_ref[0] == 0)
    def _():
        def stage(h):
            s_scr[h % 2, 0:band, :], s_scr[h % 2, band:nk, :] = scores(h)

        stage(0)
        for h in range(NA_HEADS):
            slot = h % 2
            if h + 1 < NA_HEADS:
                stage(h + 1)
            ls = []
            for st in range(tq // LANE):
                cs = slice(st * LANE, (st + 1) * LANE)
                m = jnp.max(s_scr[slot, :, cs], axis=0, keepdims=True)
                p = jnp.exp2(s_scr[slot, :, cs] - m)
                ls.append(jnp.sum(p, axis=0, keepdims=True))
                p_scr[slot, :, cs] = p.astype(BF16)
            finish(h, jnp.concatenate(ls, axis=1))


def _natten(bounded, qn, kn, vnt, bias, n_lat):
    B, P, S, _ = qn.shape
    rows = n_lat // GRID_W
    nblk = rows // NA_Q_ROWS
    tq = NA_Q_ROWS * GRID_W
    band = NA_BAND_ROWS * GRID_W
    nk = band + S - n_lat
    btype = lambda i: jnp.where(i == 0, 0, jnp.where(i == nblk - 1, 2, 1))
    return pl.pallas_call(
        functools.partial(_natten_kernel, n_lat=n_lat, n_ctx=S - n_lat, rows=rows),
        out_shape=jax.ShapeDtypeStruct((B, NA_HEADS * NA_DH, n_lat), BF16),
        grid=(B, nblk),
        in_specs=[pl.BlockSpec(memory_space=pltpu.SMEM),
                  pl.BlockSpec((1, P, tq, LANE), lambda b, i: (b, 0, i, 0)),
                  pl.BlockSpec((1, P, S, LANE), lambda b, i: (b, 0, 0, 0)),
                  pl.BlockSpec((1, NA_HEADS * NA_DH, S), lambda b, i: (b, 0, 0)),
                  pl.BlockSpec((NA_HEADS, 1, band, tq), lambda b, i: (0, btype(i), 0, 0))],
        out_specs=pl.BlockSpec((1, NA_HEADS * NA_DH, tq), lambda b, i: (b, 0, i)),
        scratch_shapes=[pltpu.VMEM((2, nk, tq), F32), pltpu.VMEM((2, nk, tq), BF16)],
        compiler_params=_params("arbitrary", "arbitrary"), name="natten",
    )(bounded, qn, kn, vnt, bias)


def _na_bias_kernel(rpb_ref, o_ref, bc_ref, *, rows):
    l, h = pl.program_id(0), pl.program_id(1)
    tq = NA_Q_ROWS * GRID_W
    kc = lax.broadcasted_iota(jnp.int32, (GRID_W, tq), 0)
    ql = lax.broadcasted_iota(jnp.int32, (GRID_W, tq), 1)
    c = ql & (GRID_W - 1)
    qi = ql >> 6
    dc = kc - c + (NA_WIN_C - 1)
    cs = jnp.clip(c - NA_WIN_C // 2, 0, GRID_W - NA_WIN_C)
    col_ok = (kc >= cs) & (kc < cs + NA_WIN_C)
    n_dr, n_dc = 2 * NA_WIN_R - 1, 2 * NA_WIN_C - 1
    base = (l * NA_HEADS + h) * (n_dr * n_dc)
    for dr in range(n_dr):
        acc = jnp.zeros((GRID_W, tq), F32)
        for j in range(n_dc):
            acc = jnp.where(dc == j, rpb_ref[base + dr * n_dc + j], acc)
        bc_ref[dr] = jnp.where(col_ok, acc * LOG2E, NEG_INF)
    nblk = rows // NA_Q_ROWS
    for t, blk in enumerate((0, 1, nblk - 1)):
        r0 = blk * NA_Q_ROWS
        u0 = min(max(r0 - NA_WIN_R // 2, 0), rows - NA_BAND_ROWS)
        for j in range(NA_BAND_ROWS):
            tile = jnp.full((GRID_W, tq), NEG_INF, F32)
            for i in range(NA_Q_ROWS):
                r, kr = r0 + i, u0 + j
                rs = min(max(r - NA_WIN_R // 2, 0), rows - NA_WIN_R)
                if rs <= kr < rs + NA_WIN_R:
                    tile = jnp.where(qi == i, bc_ref[kr - r + NA_WIN_R - 1], tile)
            o_ref[0, 0, t, j * GRID_W:(j + 1) * GRID_W, :] = tile


def _na_bias(rpb, rows):
    L = rpb.shape[0]
    tq = NA_Q_ROWS * GRID_W
    band = NA_BAND_ROWS * GRID_W
    return pl.pallas_call(
        functools.partial(_na_bias_kernel, rows=rows),
        out_shape=jax.ShapeDtypeStruct((L, NA_HEADS, 3, band, tq), F32),
        grid=(L, NA_HEADS),
        in_specs=[pl.BlockSpec(memory_space=pltpu.SMEM)],
        out_specs=pl.BlockSpec((1, 1, 3, band, tq), lambda l, h: (l, h, 0, 0, 0)),
        scratch_shapes=[pltpu.VMEM((2 * NA_WIN_R - 1, GRID_W, tq), F32)],
        compiler_params=_params("arbitrary", "arbitrary"), name="na_bias",
    )(rpb.reshape(-1))


def _ctx_attn_kernel(qm_ref, km_ref, vmt_ref, qn_ref, kn_ref, vnt_ref, mo_ref, no_ref):
    lane = lax.broadcasted_iota(jnp.int32, (1, LANE), 1)
    for hh in range(2):
        hs = slice(hh * MLA_V, (hh + 1) * MLA_V)
        mo_ref[0, hs, :] = _softmax_pv_t([(km_ref[0, hh], qm_ref[0, hh], vmt_ref[0, hs, :], None)]).astype(BF16)
        qh = qn_ref[0, 0] * ((lane < NA_DH) if hh == 0 else (lane >= NA_DH)).astype(F32).astype(BF16)
        no_ref[0, hs, :] = _softmax_pv_t([(kn_ref[0, 0], qh, vnt_ref[0, hs, :], None)]).astype(BF16)


def _ctx_attn(qm, km, vmt, qn, kn, vnt, n_lat):
    B, _, S, _ = qm.shape
    n_ctx = S - n_lat
    t = n_lat // n_ctx
    head = lambda b, p: (b, p, t, 0)
    rowsT = lambda b, p: (b, p, t)
    out = lambda b, p: (b, p, 0)
    return pl.pallas_call(
        _ctx_attn_kernel,
        out_shape=[jax.ShapeDtypeStruct((B, MLA_HEADS * MLA_V, n_ctx), BF16),
                   jax.ShapeDtypeStruct((B, NA_HEADS * NA_DH, n_ctx), BF16)],
        grid=(B, NA_HEADS // 2),
        in_specs=[pl.BlockSpec((1, 2, n_ctx, LANE), head), pl.BlockSpec((1, 2, n_ctx, LANE), head),
                  pl.BlockSpec((1, 2 * MLA_V, n_ctx), rowsT),
                  pl.BlockSpec((1, 1, n_ctx, LANE), head), pl.BlockSpec((1, 1, n_ctx, LANE), head),
                  pl.BlockSpec((1, 2 * NA_DH, n_ctx), rowsT)],
        out_specs=[pl.BlockSpec((1, 2 * MLA_V, n_ctx), out), pl.BlockSpec((1, 2 * NA_DH, n_ctx), out)],
        compiler_params=_params("arbitrary", "arbitrary"), name="ctx_attn",
    )(qm, km, vmt, qn, kn, vnt)


def _block_out_kernel(x_ref, xc_ref, mod_ref, g1_ref, g2_ref, hf_ref, hb_ref, o_ref, mo_ref, no_ref, moc_ref, noc_ref,
                      wg_ref, gout_ref, wml_ref, wmla_ref, wna_ref, wout_ref, w1_ref, w2_ref, out_ref, *, n_lat_tiles):
    is_ctx = pl.program_id(1) >= n_lat_tiles
    x = jnp.where(is_ctx, xc_ref[0], x_ref[0])
    m = mod_ref[0]
    hb = (_rms(x, g1_ref[...], D_MODEL) * (1.0 + m[1:2]) + m[0:1]).astype(BF16)
    gates = jax.nn.sigmoid(_dot(hb, wg_ref[...]))
    hs = hf_ref[0] + hb_ref[0]
    og = jax.nn.sigmoid(o_ref[0])
    gout = gout_ref[...]
    hn = jnp.concatenate(
        [_rms(hs[:, h * ML_DV:(h + 1) * ML_DV], gout[:, h * ML_DV:(h + 1) * ML_DV], ML_DV) for h in range(ML_HEADS)],
        axis=1)
    y_a = _dot((hn * og).astype(BF16), wml_ref[...])
    y_b = _dot_tn(jnp.where(is_ctx, moc_ref[0], mo_ref[0]), wmla_ref[...])
    y_c = _dot_tn(jnp.where(is_ctx, noc_ref[0], no_ref[0]), wna_ref[...])
    mg = gates[:, 0:D_MODEL] * y_a + gates[:, D_MODEL:2 * D_MODEL] * y_b + gates[:, 2 * D_MODEL:] * y_c
    x1 = x + m[2:3] * _dot(mg.astype(BF16), wout_ref[...])

    h2 = (_rms(x1, g2_ref[...], D_MODEL) * (1.0 + m[4:5]) + m[3:4]).astype(BF16)
    acc = jnp.zeros((TM, D_MODEL), F32)
    fc = 1024
    for c in range(D_FF // fc):
        u = jnp.maximum(_dot(h2, w1_ref[:, c * fc:(c + 1) * fc]), 0.0)
        acc = acc + _dot((u * u).astype(BF16), w2_ref[c * fc:(c + 1) * fc, :])
    out_ref[0] = x1 + m[5:6] * acc


def _block_out(x, xc, ctx_block, mods, g1, g2, hf, hb, o, mla_o, na_o, mla_oc, na_oc, w, nt, n_lat_tiles):
    B = x.shape[0]
    tok = lambda b, t: (b, t, 0)
    lat_t = lambda b, t: (b, 0, jnp.minimum(t, n_lat_tiles - 1))
    ctx_t = lambda b, t: (b, 0, 0)
    in_specs = _stream_specs(ctx_block, n_lat_tiles)
    in_specs += [pl.BlockSpec((1, 8, D_MODEL), lambda b, t: (jnp.where(t >= n_lat_tiles, B, b), 0, 0)),
                 _const_spec(g1.shape), _const_spec(g2.shape),
                 pl.BlockSpec((1, TM, 512), tok), pl.BlockSpec((1, TM, 512), tok), pl.BlockSpec((1, TM, 512), tok),
                 pl.BlockSpec((1, 512, TM), lat_t), pl.BlockSpec((1, 512, TM), lat_t),
                 pl.BlockSpec((1, 512, TM), ctx_t), pl.BlockSpec((1, 512, TM), ctx_t)]
    names = ("wg", "gout", "wml", "wmla", "wna", "wout", "w1", "w2")
    in_specs += [_const_spec(w[k].shape) for k in names]
    return pl.pallas_call(
        functools.partial(_block_out_kernel, n_lat_tiles=n_lat_tiles),
        out_shape=jax.ShapeDtypeStruct((B, nt * TM, D_MODEL), F32),
        grid=(B, nt), in_specs=in_specs, out_specs=pl.BlockSpec((1, TM, D_MODEL), tok),
        compiler_params=_params("arbitrary", "arbitrary"), name="block_out",
    )(x, xc, mods, g1, g2, hf, hb, o, mla_o, na_o, mla_oc, na_oc, *[w[k] for k in names])


def _rope_tables(n_lat, n_ctx):
    nf = MLA_ROPE // 4
    inv = jnp.power(ROPE_THETA, -jnp.arange(nf, dtype=F32) / nf)
    pos = jnp.arange(n_lat)
    ang_r = (pos // GRID_W).astype(F32)[:, None] * inv
    ang_c = (pos % GRID_W).astype(F32)[:, None] * inv
    one = lambda n: jnp.ones((n_lat, n), F32)
    zero = lambda n: jnp.zeros((n_lat, n), F32)
    cr, sr, cc, sc = jnp.cos(ang_r), jnp.sin(ang_r), jnp.cos(ang_c), jnp.sin(ang_c)
    cos = jnp.concatenate([one(MLA_NOPE), cr, cr, cc, cc, one(LANE - MLA_DQK)], axis=1)
    sin = jnp.concatenate([zero(MLA_NOPE), -sr, sr, -sc, sc, zero(LANE - MLA_DQK)], axis=1)
    ident = (jnp.ones((n_ctx, LANE), F32), jnp.zeros((n_ctx, LANE), F32))
    return tuple(jnp.concatenate([t, i], axis=0) for t, i in zip((cos, sin), ident))


def _rope_partner(a):
    half = MLA_ROPE // 4
    lane = np.arange(LANE)
    r = lane - MLA_NOPE
    is_rope = (r >= 0) & (r < MLA_ROPE)
    src = np.where((r // half) % 2 == 0, lane + half, lane - half)
    return jnp.where(is_rope, a[..., np.where(is_rope, src, lane)], 0)


def _block_diag_ones(block, size=2 * LANE):
    idx = np.arange(size) // block
    return jnp.asarray(idx[:, None] == idx[None, :], BF16)


def _layer_weights(l, w_in, ml_i_bias, ml_f_bias, ml_g_out, ml_w_o, mla_g_cq, mla_w_uq, mla_g_ckv, mla_w_ukv,
                   mla_g_q, mla_g_k, mla_w_o, na_g_q, na_g_k, na_w_o, w_out, w_ff1, w_ff2):
    offs = np.cumsum(IN_SIZES)[:-1].tolist()
    mlq, mlk, mlv, mlo, mlg, dq, dkv, kr, naq, nak, nav, gates = jnp.split(w_in[l], offs, axis=-1)
    padc = lambda a, n: jnp.pad(a, ((0, 0), (0, n - a.shape[1])))
    kr_slab = jnp.pad(kr, ((0, 0), (MLA_NOPE, LANE - MLA_DQK)))
    w_uq = jnp.pad(mla_w_uq[l].reshape(MLA_Q_RANK, MLA_HEADS, MLA_DQK), ((0, 0), (0, 0), (0, LANE - MLA_DQK)))
    w_ukv = mla_w_ukv[l].reshape(MLA_KV_RANK, MLA_HEADS, MLA_NOPE + MLA_V)
    w_uk = jnp.pad(w_ukv[..., :MLA_NOPE], ((0, 0), (0, 0), (0, LANE - MLA_NOPE)))
    w_uv = w_ukv[..., MLA_NOPE:].reshape(MLA_KV_RANK, MLA_HEADS * MLA_V)
    row = lambda a: a.reshape(1, -1).astype(F32)
    bias16 = jnp.stack([ml_i_bias[l, 0], ml_f_bias[l, 0], ml_i_bias[l, 1], ml_f_bias[l, 1]]).reshape(-1).astype(F32)
    return {
        "wa": jnp.concatenate([mlq * (ML_DQK ** -0.5), mlv, mlo, padc(mlg, LANE)], axis=1).astype(BF16),
        "wat": jnp.concatenate([mlk.T, mlg.T], axis=0).astype(BF16),
        "wb": jnp.concatenate([dq, dkv, kr_slab, _rope_partner(kr_slab)], axis=1).astype(BF16),
        "wc": jnp.concatenate([naq, nak], axis=1).astype(BF16),
        "wct": nav.T.astype(BF16),
        "wuq": w_uq.reshape(MLA_Q_RANK, MLA_HEADS * LANE).astype(BF16),
        "wuqs": _rope_partner(w_uq).reshape(MLA_Q_RANK, MLA_HEADS * LANE).astype(BF16),
        "wuk": w_uk.reshape(MLA_KV_RANK, MLA_HEADS * LANE).astype(BF16),
        "wuvt": w_uv.T.astype(BF16),
        "gcq": row(mla_g_cq[l]), "gckv": row(mla_g_ckv[l]),
        "gq": padc(row(mla_g_q[l]) * (MLA_DQK ** -0.5 * LOG2E), LANE), "gk": padc(row(mla_g_k[l]), LANE),
        "ngq": row(jnp.tile(na_g_q[l], 2)) * (NA_DH ** -0.5 * LOG2E), "ngk": row(jnp.tile(na_g_k[l], 2)),
        "ng": jnp.concatenate([row(jnp.tile(na_g_q[l], NA_HEADS)) * (NA_DH ** -0.5 * LOG2E),
                               row(jnp.tile(na_g_k[l], NA_HEADS))], axis=1),
        "bd128": _block_diag_ones(LANE), "bd64": _block_diag_ones(NA_DH),
        "brow": padc(bias16.reshape(1, 16), LANE), "bcol": bias16.reshape(16, 1),
        "wg": gates.astype(BF16), "gout": row(ml_g_out[l]),
        "wml": ml_w_o[l].astype(BF16), "wmla": mla_w_o[l].astype(BF16), "wna": na_w_o[l].astype(BF16),
        "wout": w_out[l].astype(BF16), "w1": w_ff1[l].astype(BF16), "w2": w_ff2[l].astype(BF16),
    }


def kernel(x, c, ctx, c_ctx, w_mod, b_mod, g_norm1, g_norm2, w_in, ml_i_bias, ml_f_bias, ml_g_out, ml_w_o, mla_g_cq, mla_w_uq, mla_g_ckv, mla_w_ukv, mla_g_q, mla_g_k, mla_w_o, na_g_q, na_g_k, na_rpb, na_w_o, w_out, w_ff1, w_ff2):
    B, T, D = x.shape
    C = ctx.shape[1]
    depth = w_in.shape[0]
    assert D == D_MODEL and C == TM and T % MLA_TQ == 0 and B < 16
    rows = T // GRID_W
    assert rows % NA_Q_ROWS == 0 and rows >= NA_BAND_ROWS
    n_lat_tiles = T // TM

    cc = jnp.zeros((16, D), F32).at[:B].set(c).at[B].set(c_ctx)
    mod = _modulation(cc, w_mod, b_mod).reshape(depth, 16, 6, D)
    mod = jnp.pad(mod, ((0, 0), (0, 0), (0, 2), (0, 0)))
    tabs = _rope_tables(T, C)
    na_bias = _na_bias(na_rpb.astype(F32), rows)
    stream = (x, ctx, 0)

    for l in range(depth):
        last = l == depth - 1
        w = _layer_weights(l, w_in, ml_i_bias, ml_f_bias, ml_g_out, ml_w_o, mla_g_cq, mla_w_uq, mla_g_ckv,
                           mla_w_ukv, mla_g_q, mla_g_k, mla_w_o, na_g_q, na_g_k, na_w_o, w_out, w_ff1, w_ff2)
        g1 = g_norm1[l].reshape(1, D).astype(F32)
        g2 = g_norm2[l].reshape(1, D).astype(F32)
        cos, sin = tabs
        gain_tabs = (cos * w["gq"], sin * _rope_partner(w["gq"]), cos * w["gk"], sin * _rope_partner(w["gk"]))
        q, kt, v, o, gcol, grow, qm, km, vmt, qn, kn, vnt = _inproj(*stream, mod[l], g1, w, gain_tabs, n_lat_tiles)
        hf, hb = _mlstm(q, kt, v, gcol, grow, w["brow"], w["bcol"], n_lat_tiles)
        mla_bound = MLA_DQK * jnp.max(jnp.abs(w["gq"])) * jnp.max(jnp.abs(w["gk"]))
        na_bound = (NA_DH * jnp.max(jnp.abs(w["ngq"])) * jnp.max(jnp.abs(w["ngk"]))
                    + LOG2E * jnp.max(jnp.abs(na_rpb[l].astype(F32))))
        mla_o = _mla((mla_bound <= SCORE_BOUND_LIMIT).astype(jnp.int32).reshape(1), qm, km, vmt, T)
        na_o = _natten((na_bound <= SCORE_BOUND_LIMIT).astype(jnp.int32).reshape(1), qn, kn, vnt, na_bias[l], T)
        mla_oc, na_oc = (mla_o, na_o) if last else _ctx_attn(qm, km, vmt, qn, kn, vnt, T)
        nt = n_lat_tiles if last else n_lat_tiles + 1
        xs = _block_out(*stream, mod[l], g1, g2, hf, hb, o, mla_o, na_o, mla_oc, na_oc, w, nt, n_lat_tiles)
        stream = (xs, xs, n_lat_tiles)
    return xs
```

```python
import functools

import numpy as np
import jax
import jax.numpy as jnp
from jax import lax
from jax.experimental import pallas as pl
from jax.experimental.pallas import tpu as pltpu

F32 = jnp.float32
BF16 = jnp.bfloat16

D_MODEL = 1024
GRID_W = 64
ML_HEADS, ML_DQK, ML_DV = 4, 64, 128
ML_CHUNK = 128
ML_SAMPLES = 1
MLA_HEADS, MLA_Q_RANK, MLA_KV_RANK, MLA_NOPE, MLA_ROPE, MLA_V = 8, 384, 256, 64, 32, 64
MLA_DQK = MLA_NOPE + MLA_ROPE
NA_HEADS, NA_DH, NA_WIN_R, NA_WIN_C = 8, 64, 8, 16
NA_Q_ROWS = 4
NA_BAND_ROWS = NA_Q_ROWS + NA_WIN_R
D_FF = 4 * D_MODEL
ROPE_THETA = 10000.0
EPS = 1e-6
LOG2E = 1.4426950408889634
SCORE_BOUND_LIMIT = 60.0
NEG_INF = -1e30
IN_SIZES = (256, 256, 512, 512, 16, MLA_Q_RANK, MLA_KV_RANK, MLA_ROPE, 512, 512, 512, 3 * D_MODEL)
LANE = 128
TM = 256
MLA_TQ = 1024
VMEM_LIMIT = 56 * 1024 * 1024


def _dot(a, b):
    return jnp.dot(a, b, preferred_element_type=F32)


def _dot_nt(a, b):
    return lax.dot_general(a, b, (((1,), (1,)), ((), ())), preferred_element_type=F32)


def _dot_tn(a, b):
    return lax.dot_general(a, b, (((0,), (0,)), ((), ())), preferred_element_type=F32)


def _rms(x, g, n):
    ms = jnp.sum(x * x, axis=-1, keepdims=True) * (1.0 / n)
    return x * lax.rsqrt(ms + EPS) * g


def _split3(x):
    hi = x.astype(BF16)
    r = x - hi.astype(F32)
    mid = r.astype(BF16)
    lo = (r - mid.astype(F32)).astype(BF16)
    return hi, mid, lo


def _params(*sem):
    return pltpu.CompilerParams(dimension_semantics=sem, vmem_limit_bytes=VMEM_LIMIT)


def _const_spec(shape):
    nd = len(shape)
    return pl.BlockSpec(shape, lambda *_: (0,) * nd, pipeline_mode=pl.Buffered(1))


def _mod_kernel(c_ref, w_ref, b_ref, o_ref):
    c = c_ref[...]
    s = c * jax.nn.sigmoid(c)
    o_ref[0] = _dot(s, w_ref[0]) + b_ref[0]


def _modulation(cc, w_mod, b_mod):
    L = w_mod.shape[0]
    tn = 1024
    return pl.pallas_call(
        _mod_kernel,
        out_shape=jax.ShapeDtypeStruct((L, 16, 6 * D_MODEL), F32),
        grid=(L, 6 * D_MODEL // tn),
        in_specs=[pl.BlockSpec((16, D_MODEL), lambda l, n: (0, 0)),
                  pl.BlockSpec((1, D_MODEL, tn), lambda l, n: (l, 0, n)),
                  pl.BlockSpec((1, 1, tn), lambda l, n: (l, 0, n))],
        out_specs=pl.BlockSpec((1, 16, tn), lambda l, n: (l, 0, n)),
        compiler_params=_params("arbitrary", "arbitrary"),
        name="modulation",
    )(cc, w_mod, b_mod.reshape(L, 1, 6 * D_MODEL))


def _inproj_kernel(x_ref, xc_ref, mod_ref, g1_ref, wa_ref, wat_ref, wb_ref, wc_ref, wct_ref, wuq_ref, wuqs_ref, wuk_ref,
                   wuvt_ref, gcq_ref, gckv_ref, ng_ref, cgq_ref, sgq_ref, cgk_ref, sgk_ref,
                   q_ref, kt_ref, v_ref, o_ref, gcol_ref, grow_ref, qm_ref, km_ref, vmt_ref, qn_ref, kn_ref, vnt_ref,
                   *, n_lat_tiles):
    x = jnp.where(pl.program_id(1) >= n_lat_tiles, xc_ref[0], x_ref[0])
    m = mod_ref[0]
    hb = (_rms(x, g1_ref[...], D_MODEL) * (1.0 + m[1:2]) + m[0:1]).astype(BF16)

    a = _dot(hb, wa_ref[...])
    q_ref[0] = a[:, 0:256].astype(BF16)
    v_ref[0] = a[:, 256:768].astype(BF16)
    o_ref[0] = a[:, 768:1280]
    gcol_ref[0] = a[:, 1280:1408]
    at = _dot_nt(wat_ref[...], hb)
    kt_ref[0] = at[0:256].astype(BF16)
    grow_ref[0] = at[256:272]

    bs = _dot(hb, wb_ref[...])
    dqn = _rms(bs[:, 0:MLA_Q_RANK], gcq_ref[...], MLA_Q_RANK).astype(BF16)
    dkvn = _rms(bs[:, MLA_Q_RANK:MLA_Q_RANK + MLA_KV_RANK], gckv_ref[...], MLA_KV_RANK).astype(BF16)
    kr = bs[:, MLA_Q_RANK + MLA_KV_RANK:MLA_Q_RANK + MLA_KV_RANK + LANE]
    kr_sw = bs[:, MLA_Q_RANK + MLA_KV_RANK + LANE:]
    qh_all = _dot(dqn, wuq_ref[...])
    qsw_all = _dot(dqn, wuqs_ref[...])
    kh_all = _dot(dkvn, wuk_ref[...]) + jnp.concatenate([kr] * MLA_HEADS, axis=1)
    cgq, sgq, cgk, sgk = cgq_ref[...], sgq_ref[...], cgk_ref[...], sgk_ref[...]
    inv_rms = lambda y: lax.rsqrt(jnp.sum(y * y, axis=-1, keepdims=True) * (1.0 / MLA_DQK) + EPS)
    for h in range(MLA_HEADS):
        sl = slice(h * LANE, (h + 1) * LANE)
        qh, kh = qh_all[:, sl], kh_all[:, sl]
        qm_ref[0, h] = ((qh * cgq + qsw_all[:, sl] * sgq) * inv_rms(qh)).astype(BF16)
        km_ref[0, h] = ((kh * cgk + kr_sw * sgk) * inv_rms(kh)).astype(BF16)
    vmt_ref[0] = _dot_nt(wuvt_ref[...], dkvn).astype(BF16)

    c = _dot(hb, wc_ref[...])
    lo = lax.broadcasted_iota(jnp.int32, (1, LANE), 1) < NA_DH
    ng = ng_ref[...]
    for p in range(NA_HEADS // 2):
        for base, out_ref in ((0, qn_ref), (NA_HEADS * NA_DH, kn_ref)):
            sl = slice(base + p * LANE, base + (p + 1) * LANE)
            s = c[:, sl]
            sq = s * s
            s_lo = jnp.sum(jnp.where(lo, sq, 0.0), axis=-1, keepdims=True)
            s_hi = jnp.sum(jnp.where(lo, 0.0, sq), axis=-1, keepdims=True)
            ms = jnp.where(lo, s_lo, s_hi) * (1.0 / NA_DH)
            out_ref[0, p] = (s * lax.rsqrt(ms + EPS) * ng[:, sl]).astype(BF16)
    vnt_ref[0] = _dot_nt(wct_ref[...], hb).astype(BF16)


def _stream_specs(ctx_block, n_lat_tiles):
    return [pl.BlockSpec((1, TM, D_MODEL), lambda b, t: (b, jnp.minimum(t, n_lat_tiles - 1), 0)),
            pl.BlockSpec((1, TM, D_MODEL), lambda b, t: (b, ctx_block, 0))]


def _inproj(x, xc, ctx_block, mods, g1, w, tabs, n_lat_tiles):
    B = x.shape[0]
    nt = n_lat_tiles + 1
    S = nt * TM
    tok = lambda b, t: (b, t, 0)
    tokT = lambda b, t: (b, 0, t)
    head = lambda b, t: (b, 0, t, 0)
    in_specs = _stream_specs(ctx_block, n_lat_tiles)
    in_specs += [pl.BlockSpec((1, 8, D_MODEL), lambda b, t: (jnp.where(t >= n_lat_tiles, B, b), 0, 0)),
                 _const_spec(g1.shape)]
    consts = ("wa", "wat", "wb", "wc", "wct", "wuq", "wuqs", "wuk", "wuvt", "gcq", "gckv", "ng")
    in_specs += [_const_spec(w[k].shape) for k in consts]
    in_specs += [pl.BlockSpec((TM, LANE), lambda b, t: (t, 0))] * 4
    out_shape = [jax.ShapeDtypeStruct((B, S, 256), BF16),
                 jax.ShapeDtypeStruct((B, 256, S), BF16),
                 jax.ShapeDtypeStruct((B, S, 512), BF16),
                 jax.ShapeDtypeStruct((B, S, 512), F32),
                 jax.ShapeDtypeStruct((B, S, LANE), F32),
                 jax.ShapeDtypeStruct((B, 16, S), F32),
                 jax.ShapeDtypeStruct((B, MLA_HEADS, S, LANE), BF16),
                 jax.ShapeDtypeStruct((B, MLA_HEADS, S, LANE), BF16),
                 jax.ShapeDtypeStruct((B, MLA_HEADS * MLA_V, S), BF16),
                 jax.ShapeDtypeStruct((B, NA_HEADS // 2, S, LANE), BF16),
                 jax.ShapeDtypeStruct((B, NA_HEADS // 2, S, LANE), BF16),
                 jax.ShapeDtypeStruct((B, NA_HEADS * NA_DH, S), BF16)]
    out_specs = [pl.BlockSpec((1, TM, 256), tok), pl.BlockSpec((1, 256, TM), tokT),
                 pl.BlockSpec((1, TM, 512), tok), pl.BlockSpec((1, TM, 512), tok),
                 pl.BlockSpec((1, TM, LANE), tok), pl.BlockSpec((1, 16, TM), tokT),
                 pl.BlockSpec((1, MLA_HEADS, TM, LANE), head), pl.BlockSpec((1, MLA_HEADS, TM, LANE), head),
                 pl.BlockSpec((1, MLA_HEADS * MLA_V, TM), tokT),
                 pl.BlockSpec((1, NA_HEADS // 2, TM, LANE), head), pl.BlockSpec((1, NA_HEADS // 2, TM, LANE), head),
                 pl.BlockSpec((1, NA_HEADS * NA_DH, TM), tokT)]
    return pl.pallas_call(
        functools.partial(_inproj_kernel, n_lat_tiles=n_lat_tiles),
        out_shape=out_shape, grid=(B, nt), in_specs=in_specs, out_specs=out_specs,
        compiler_params=_params("arbitrary", "arbitrary"), name="inproj",
    )(x, xc, mods, g1, *[w[k] for k in consts], *tabs)


def _log_sigmoid(x):
    return jnp.minimum(x, 0.0) - jnp.log1p(jnp.exp(-jnp.abs(x)))


def _mlstm_kernel(qf_ref, ktf_ref, vf_ref, gcf_ref, grf_ref, qb_ref, ktb_ref, vb_ref, gcb_ref, grb_ref,
                  brow_ref, bcol_ref, hf_ref, hb_ref, c_ref, m_ref):
    @pl.when(pl.program_id(1) == 0)
    def _():
        c_ref[...] = jnp.zeros_like(c_ref)
        m_ref[...] = jnp.zeros_like(m_ref)

    dirs = ((qf_ref, ktf_ref, vf_ref, gcf_ref, grf_ref, hf_ref), (qb_ref, ktb_ref, vb_ref, gcb_ref, grb_ref, hb_ref))
    samples = [_mlstm_sample(bb, dirs, brow_ref, bcol_ref, c_ref, m_ref) for bb in range(qf_ref.shape[0])]
    for _ in zip(*samples):
        pass


def _mlstm_sample(bb, dirs, brow_ref, bcol_ref, c_ref, m_ref):
    L = ML_CHUNK
    row = lax.broadcasted_iota(jnp.int32, (L, L), 0)
    col = lax.broadcasted_iota(jnp.int32, (L, L), 1)
    lane = lax.broadcasted_iota(jnp.int32, (1, LANE), 1)
    isf_row = (lane & ML_HEADS) != 0
    sub = lax.broadcasted_iota(jnp.int32, (16, 1), 0)
    isf_col = (sub & ML_HEADS) != 0
    half_mask = [(lane < ML_DQK).astype(F32).astype(BF16), (lane >= ML_DQK).astype(F32).astype(BF16)]
    e0 = jnp.broadcast_to((lane == 0).astype(F32).astype(BF16), (L, LANE))
    for d, (q_ref, kt_ref, v_ref, gc_ref, gr_ref, h_ref) in enumerate(dirs):
        causal = (col <= row) if d == 0 else (col >= row)
        tri = causal.astype(F32).astype(BF16)
        tri_t = ((row <= col) if d == 0 else (row >= col)).astype(F32).astype(BF16)
        local = {}
        for ch in range(TM // L):
            rs = slice(ch * L, (ch + 1) * L)
            pre_c = gc_ref[bb, rs, :] + brow_ref[...]
            val_c = jnp.where(isf_row, _log_sigmoid(pre_c), pre_c)
            pre_r = gr_ref[bb, :, rs] + bcol_ref[...]
            val_r = jnp.where(isf_col, _log_sigmoid(pre_r), pre_r)
            cum_c = sum(_dot(tri, t) for t in _split3(val_c))
            cum_r = sum(_dot(t, tri_t) for t in _split3(val_r))
            tot_r = jnp.sum(val_r, axis=1, keepdims=True)
            for h in range(ML_HEADS):
                ji, jf = (2 * d) * ML_HEADS + h, (2 * d + 1) * ML_HEADS + h
                p, half = h // 2, h % 2
                bcum_c = cum_c[:, jf:jf + 1]
                bcum_r = cum_r[jf:jf + 1, :]
                i_r = val_r[ji:ji + 1, :]
                btot = tot_r[jf:jf + 1, :]
                w_end = btot - bcum_r + i_r
                m_w = jnp.max(w_end, axis=1, keepdims=True)
                dm = jnp.where(causal, bcum_c - bcum_r + i_r, -jnp.inf)
                m_loc = jnp.max(dm, axis=1, keepdims=True)
                q_h = q_ref[bb, rs, p * LANE:(p + 1) * LANE] * half_mask[half]
                kt_pair = kt_ref[bb, p * LANE:(p + 1) * LANE, rs]
                kt_h = kt_ref[bb, h * ML_DQK:(h + 1) * ML_DQK, rs]
                v_ext = jnp.concatenate([v_ref[bb, rs, h * ML_DV:(h + 1) * ML_DV], e0], axis=1)
                s = _dot(q_h, kt_pair) * jnp.exp(dm - m_loc)
                sv = _dot(s.astype(BF16), v_ext)
                kw = (kt_h.astype(F32) * jnp.exp(w_end - m_w)).astype(BF16)
                dc = _dot(kw, v_ext)
                local[ch, h] = (q_h, bcum_c, btot, m_w, m_loc, sv, dc)
                yield
        for cc in range(TM // L):
            ch = cc if d == 0 else TM // L - 1 - cc
            rs = slice(ch * L, (ch + 1) * L)
            for h in range(ML_HEADS):
                p, half = h // 2, h % 2
                idx = d * ML_HEADS + h
                q_h, bcum_c, btot, m_w, m_loc, sv, dc = local[ch, h]
                m_old = m_ref[bb, idx, 0:1, 0:1]
                c_pair = c_ref[bb, d, p]
                inter = bcum_c + m_old
                m_t = jnp.maximum(inter, m_loc)
                tot = jnp.exp(inter - m_t) * _dot(q_h, c_pair.astype(BF16)) + jnp.exp(m_loc - m_t) * sv
                den = tot[:, ML_DV:ML_DV + 1]
                h_ref[bb, rs, h * ML_DV:(h + 1) * ML_DV] = tot[:, 0:ML_DV] / jnp.maximum(jnp.abs(den), jnp.exp(-m_t))
                m_new = jnp.maximum(btot + m_old, m_w)
                hs = slice(half * ML_DQK, (half + 1) * ML_DQK)
                c_ref[bb, d, p, hs, :] = jnp.exp(btot + m_old - m_new) * c_pair[hs, :] + jnp.exp(m_w - m_new) * dc
                m_ref[bb, idx] = jnp.broadcast_to(m_new, (8, LANE))
                yield


def _mlstm(q, kt, v, gcol, grow, brow, bcol, n_lat_tiles):
    B, S, _ = q.shape
    nblk = S // TM
    fwd = lambda j: jnp.where(j == 0, n_lat_tiles, j - 1)
    bwd = lambda j: jnp.where(j == 0, n_lat_tiles, n_lat_tiles - j)

    nb = ML_SAMPLES if B % ML_SAMPLES == 0 else 1

    def specs(order):
        return [pl.BlockSpec((nb, TM, 256), lambda b, j: (b, order(j), 0)),
                pl.BlockSpec((nb, 256, TM), lambda b, j: (b, 0, order(j))),
                pl.BlockSpec((nb, TM, 512), lambda b, j: (b, order(j), 0)),
                pl.BlockSpec((nb, TM, LANE), lambda b, j: (b, order(j), 0)),
                pl.BlockSpec((nb, 16, TM), lambda b, j: (b, 0, order(j)))]

    return pl.pallas_call(
        _mlstm_kernel,
        out_shape=[jax.ShapeDtypeStruct((B, S, 512), F32)] * 2,
        grid=(B // nb, nblk),
        in_specs=specs(fwd) + specs(bwd) + [_const_spec(brow.shape), _const_spec(bcol.shape)],
        out_specs=[pl.BlockSpec((nb, TM, 512), lambda b, j: (b, fwd(j), 0)),
                   pl.BlockSpec((nb, TM, 512), lambda b, j: (b, bwd(j), 0))],
        scratch_shapes=[pltpu.VMEM((nb, 2, ML_HEADS // 2, 2 * ML_DQK, 2 * ML_DV), F32),
                        pltpu.VMEM((nb, 2 * ML_HEADS, 8, LANE), F32)],
        compiler_params=_params("arbitrary", "arbitrary"), name="mlstm",
    )(q, kt, v, gcol, grow, q, kt, v, gcol, grow, brow, bcol)


def _softmax_pv_t(parts):
    ss = []
    for k, q, _, bias in parts:
        s = _dot_nt(k, q)
        ss.append(s if bias is None else s + bias)
    m = functools.reduce(jnp.maximum, [jnp.max(s, axis=0, keepdims=True) for s in ss])
    ps = [jnp.exp2(s - m) for s in ss]
    l = sum(jnp.sum(p, axis=0, keepdims=True) for p in ps)
    o = sum(_dot(vt, p.astype(BF16)) for (_, _, vt, _), p in zip(parts, ps))
    return o / l


def _mla_kernel(bounded_ref, q_ref, k_ref, vt_ref, o_ref, s_scr, p_scr, acc_scr, *, chunks):
    @pl.when(bounded_ref[0] != 0)
    def _():
        _mla_bounded(q_ref, k_ref, vt_ref, o_ref, p_scr, acc_scr, chunks)

    @pl.when(bounded_ref[0] == 0)
    def _():
        _mla_online(q_ref, k_ref, vt_ref, o_ref, s_scr, p_scr, acc_scr, chunks)


def _mla_bounded(q_ref, k_ref, vt_ref, o_ref, p_scr, acc_scr, chunks):
    n_heads = q_ref.shape[1]
    items = [(hh, ci) for hh in range(n_heads) for ci in range(len(chunks))]

    def probs(n):
        hh, ci = items[n]
        off, size = chunks[ci]
        p = jnp.exp2(_dot_nt(k_ref[0, hh, off:off + size, :], q_ref[0, hh]))
        p_scr[n % 2, 0:size, :] = p.astype(BF16)
        return jnp.sum(p, axis=0, keepdims=True)

    lsum = probs(0)
    l = None
    for n, (hh, ci) in enumerate(items):
        off, size = chunks[ci]
        l = lsum if ci == 0 else l + lsum
        if n + 1 < len(items):
            lsum = probs(n + 1)
        pv = _dot(vt_ref[0, hh * MLA_V:(hh + 1) * MLA_V, off:off + size], p_scr[n % 2, 0:size, :])
        if ci == 0:
            acc_scr[hh] = pv
        else:
            acc_scr[hh] += pv
        if ci == len(chunks) - 1:
            o_ref[0, hh * MLA_V:(hh + 1) * MLA_V, :] = (acc_scr[hh] / l).astype(BF16)


def _mla_online(q_ref, k_ref, vt_ref, o_ref, s_scr, p_scr, acc_scr, chunks):
    n_heads, tq = q_ref.shape[1], q_ref.shape[2]
    items = [(hh, ci) for hh in range(n_heads) for ci in range(len(chunks))]

    def scores(n):
        hh, ci = items[n]
        off, size = chunks[ci]
        s_scr[n % 2, 0:size, :] = _dot_nt(k_ref[0, hh, off:off + size, :], q_ref[0, hh])

    scores(0)
    m = l = None
    for n, (hh, ci) in enumerate(items):
        off, size = chunks[ci]
        slot = n % 2
        if n + 1 < len(items):
            scores(n + 1)
        if ci == 0:
            m = [jnp.full((1, LANE), -jnp.inf, F32)] * (tq // LANE)
            l = [jnp.zeros((1, LANE), F32)] * (tq // LANE)
        alphas = []
        for st in range(tq // LANE):
            cs = slice(st * LANE, (st + 1) * LANE)
            m_new = jnp.maximum(m[st], jnp.max(s_scr[slot, 0:size, cs], axis=0, keepdims=True))
            alpha = jnp.exp2(m[st] - m_new)
            p = jnp.exp2(s_scr[slot, 0:size, cs] - m_new)
            l[st] = alpha * l[st] + jnp.sum(p, axis=0, keepdims=True)
            m[st] = m_new
            p_scr[slot, 0:size, cs] = p.astype(BF16)
            alphas.append(alpha)
        pv = _dot(vt_ref[0, hh * MLA_V:(hh + 1) * MLA_V, off:off + size], p_scr[slot, 0:size, :])
        if ci == 0:
            acc_scr[hh] = pv
        else:
            acc_scr[hh] = jnp.concatenate(alphas, axis=1) * acc_scr[hh] + pv
        if ci == len(chunks) - 1:
            o_ref[0, hh * MLA_V:(hh + 1) * MLA_V, :] = (acc_scr[hh] / jnp.concatenate(l, axis=1)).astype(BF16)


def _mla(bounded, qm, km, vmt, n_lat):
    B, H, S, _ = qm.shape
    kv, hp = 512, 2
    chunks = tuple((i * kv, kv) for i in range(n_lat // kv)) + ((n_lat, S - n_lat),)
    return pl.pallas_call(
        functools.partial(_mla_kernel, chunks=chunks),
        out_shape=jax.ShapeDtypeStruct((B, H * MLA_V, n_lat), BF16),
        grid=(B, H // hp, n_lat // MLA_TQ),
        in_specs=[pl.BlockSpec(memory_space=pltpu.SMEM),
                  pl.BlockSpec((1, hp, MLA_TQ, LANE), lambda b, h, i: (b, h, i, 0)),
                  pl.BlockSpec((1, hp, S, LANE), lambda b, h, i: (b, h, 0, 0)),
                  pl.BlockSpec((1, hp * MLA_V, S), lambda b, h, i: (b, h, 0))],
        out_specs=pl.BlockSpec((1, hp * MLA_V, MLA_TQ), lambda b, h, i: (b, h, i)),
        scratch_shapes=[pltpu.VMEM((2, kv, MLA_TQ), F32), pltpu.VMEM((2, kv, MLA_TQ), BF16),
                        pltpu.VMEM((hp, MLA_V, MLA_TQ), F32)],
        compiler_params=_params("arbitrary", "arbitrary", "arbitrary"), name="mla_attn",
    )(bounded, qm, km, vmt)


def _natten_kernel(bounded_ref, q_ref, k_ref, vt_ref, bias_ref, o_ref, s_scr, p_scr, *, n_lat, n_ctx, rows):
    blk = pl.program_id(1)
    band = NA_BAND_ROWS * GRID_W
    nk = band + n_ctx
    tq = q_ref.shape[2]
    u0 = jnp.clip(blk * NA_Q_ROWS - NA_WIN_R // 2, 0, rows - NA_BAND_ROWS)
    koff = pl.multiple_of(u0 * GRID_W, 2 * LANE)
    lane = lax.broadcasted_iota(jnp.int32, (1, LANE), 1)
    half_mask = [(lane < NA_DH).astype(F32).astype(BF16), (lane >= NA_DH).astype(F32).astype(BF16)]

    def scores(h):
        qh = q_ref[0, h // 2] * half_mask[h % 2]
        return (_dot_nt(k_ref[0, h // 2, pl.ds(koff, band), :], qh) + bias_ref[h, 0],
                _dot_nt(k_ref[0, h // 2, n_lat:n_lat + n_ctx, :], qh))

    def finish(h, l):
        slot = h % 2
        hs = slice(h * NA_DH, (h + 1) * NA_DH)
        o = (_dot(vt_ref[0, hs, pl.ds(koff, band)], p_scr[slot, 0:band, :])
             + _dot(vt_ref[0, hs, n_lat:n_lat + n_ctx], p_scr[slot, band:nk, :]))
        o_ref[0, hs, :] = (o / l).astype(BF16)

    @pl.when(bounded_ref[0] != 0)
    def _():
        def probs(h):
            pb, pc = (jnp.exp2(s) for s in scores(h))
            p_scr[h % 2, 0:band, :] = pb.astype(BF16)
            p_scr[h % 2, band:nk, :] = pc.astype(BF16)
            return jnp.sum(pb, axis=0, keepdims=True) + jnp.sum(pc, axis=0, keepdims=True)

        l = probs(0)
        for h in range(NA_HEADS):
            l_next = probs(h + 1) if h + 1 < NA_HEADS else None
            finish(h, l)
            l = l_next

    @pl.when(bounded_ref[0] == 0)
    def _():
        def stage(h):
            s_scr[h % 2, 0:band, :], s_scr[h % 2, band:nk, :] = scores(h)

        stage(0)
        for h in range(NA_HEADS):
            slot = h % 2
            if h + 1 < NA_HEADS:
                stage(h + 1)
            ls = []
            for st in range(tq // LANE):
                cs = slice(st * LANE, (st + 1) * LANE)
                m = jnp.max(s_scr[slot, :, cs], axis=0, keepdims=True)
                p = jnp.exp2(s_scr[slot, :, cs] - m)
                ls.append(jnp.sum(p, axis=0, keepdims=True))
                p_scr[slot, :, cs] = p.astype(BF16)
            finish(h, jnp.concatenate(ls, axis=1))


def _natten(bounded, qn, kn, vnt, bias, n_lat):
    B, P, S, _ = qn.shape
    rows = n_lat // GRID_W
    nblk = rows // NA_Q_ROWS
    tq = NA_Q_ROWS * GRID_W
    band = NA_BAND_ROWS * GRID_W
    nk = band + S - n_lat
    btype = lambda i: jnp.where(i == 0, 0, jnp.where(i == nblk - 1, 2, 1))
    return pl.pallas_call(
        functools.partial(_natten_kernel, n_lat=n_lat, n_ctx=S - n_lat, rows=rows),
        out_shape=jax.ShapeDtypeStruct((B, NA_HEADS * NA_DH, n_lat), BF16),
        grid=(B, nblk),
        in_specs=[pl.BlockSpec(memory_space=pltpu.SMEM),
                  pl.BlockSpec((1, P, tq, LANE), lambda b, i: (b, 0, i, 0)),
                  pl.BlockSpec((1, P, S, LANE), lambda b, i: (b, 0, 0, 0)),
                  pl.BlockSpec((1, NA_HEADS * NA_DH, S), lambda b, i: (b, 0, 0)),
                  pl.BlockSpec((NA_HEADS, 1, band, tq), lambda b, i: (0, btype(i), 0, 0))],
        out_specs=pl.BlockSpec((1, NA_HEADS * NA_DH, tq), lambda b, i: (b, 0, i)),
        scratch_shapes=[pltpu.VMEM((2, nk, tq), F32), pltpu.VMEM((2, nk, tq), BF16)],
        compiler_params=_params("arbitrary", "arbitrary"), name="natten",
    )(bounded, qn, kn, vnt, bias)


def _na_bias_kernel(rpb_ref, o_ref, bc_ref, *, rows):
    l, h = pl.program_id(0), pl.program_id(1)
    tq = NA_Q_ROWS * GRID_W
    kc = lax.broadcasted_iota(jnp.int32, (GRID_W, tq), 0)
    ql = lax.broadcasted_iota(jnp.int32, (GRID_W, tq), 1)
    c = ql & (GRID_W - 1)
    qi = ql >> 6
    dc = kc - c + (NA_WIN_C - 1)
    cs = jnp.clip(c - NA_WIN_C // 2, 0, GRID_W - NA_WIN_C)
    col_ok = (kc >= cs) & (kc < cs + NA_WIN_C)
    n_dr, n_dc = 2 * NA_WIN_R - 1, 2 * NA_WIN_C - 1
    base = (l * NA_HEADS + h) * (n_dr * n_dc)
    for dr in range(n_dr):
        acc = jnp.zeros((GRID_W, tq), F32)
        for j in range(n_dc):
            acc = jnp.where(dc == j, rpb_ref[base + dr * n_dc + j], acc)
        bc_ref[dr] = jnp.where(col_ok, acc * LOG2E, NEG_INF)
    nblk = rows // NA_Q_ROWS
    for t, blk in enumerate((0, 1, nblk - 1)):
        r0 = blk * NA_Q_ROWS
        u0 = min(max(r0 - NA_WIN_R // 2, 0), rows - NA_BAND_ROWS)
        for j in range(NA_BAND_ROWS):
            tile = jnp.full((GRID_W, tq), NEG_INF, F32)
            for i in range(NA_Q_ROWS):
                r, kr = r0 + i, u0 + j
                rs = min(max(r - NA_WIN_R // 2, 0), rows - NA_WIN_R)
                if rs <= kr < rs + NA_WIN_R:
                    tile = jnp.where(qi == i, bc_ref[kr - r + NA_WIN_R - 1], tile)
            o_ref[0, 0, t, j * GRID_W:(j + 1) * GRID_W, :] = tile


def _na_bias(rpb, rows):
    L = rpb.shape[0]
    tq = NA_Q_ROWS * GRID_W
    band = NA_BAND_ROWS * GRID_W
    return pl.pallas_call(
        functools.partial(_na_bias_kernel, rows=rows),
        out_shape=jax.ShapeDtypeStruct((L, NA_HEADS, 3, band, tq), F32),
        grid=(L, NA_HEADS),
        in_specs=[pl.BlockSpec(memory_space=pltpu.SMEM)],
        out_specs=pl.BlockSpec((1, 1, 3, band, tq), lambda l, h: (l, h, 0, 0, 0)),
        scratch_shapes=[pltpu.VMEM((2 * NA_WIN_R - 1, GRID_W, tq), F32)],
        compiler_params=_params("arbitrary", "arbitrary"), name="na_bias",
    )(rpb.reshape(-1))


def _ctx_attn_kernel(qm_ref, km_ref, vmt_ref, qn_ref, kn_ref, vnt_ref, mo_ref, no_ref):
    lane = lax.broadcasted_iota(jnp.int32, (1, LANE), 1)
    for hh in range(2):
        hs = slice(hh * MLA_V, (hh + 1) * MLA_V)
        mo_ref[0, hs, :] = _softmax_pv_t([(km_ref[0, hh], qm_ref[0, hh], vmt_ref[0, hs, :], None)]).astype(BF16)
        qh = qn_ref[0, 0] * ((lane < NA_DH) if hh == 0 else (lane >= NA_DH)).astype(F32).astype(BF16)
        no_ref[0, hs, :] = _softmax_pv_t([(kn_ref[0, 0], qh, vnt_ref[0, hs, :], None)]).astype(BF16)


def _ctx_attn(qm, km, vmt, qn, kn, vnt, n_lat):
    B, _, S, _ = qm.shape
    n_ctx = S - n_lat
    t = n_lat // n_ctx
    head = lambda b, p: (b, p, t, 0)
    rowsT = lambda b, p: (b, p, t)
    out = lambda b, p: (b, p, 0)
    return pl.pallas_call(
        _ctx_attn_kernel,
        out_shape=[jax.ShapeDtypeStruct((B, MLA_HEADS * MLA_V, n_ctx), BF16),
                   jax.ShapeDtypeStruct((B, NA_HEADS * NA_DH, n_ctx), BF16)],
        grid=(B, NA_HEADS // 2),
        in_specs=[pl.BlockSpec((1, 2, n_ctx, LANE), head), pl.BlockSpec((1, 2, n_ctx, LANE), head),
                  pl.BlockSpec((1, 2 * MLA_V, n_ctx), rowsT),
                  pl.BlockSpec((1, 1, n_ctx, LANE), head), pl.BlockSpec((1, 1, n_ctx, LANE), head),
                  pl.BlockSpec((1, 2 * NA_DH, n_ctx), rowsT)],
        out_specs=[pl.BlockSpec((1, 2 * MLA_V, n_ctx), out), pl.BlockSpec((1, 2 * NA_DH, n_ctx), out)],
        compiler_params=_params("arbitrary", "arbitrary"), name="ctx_attn",
    )(qm, km, vmt, qn, kn, vnt)


def _block_out_kernel(x_ref, xc_ref, mod_ref, g1_ref, g2_ref, hf_ref, hb_ref, o_ref, mo_ref, no_ref, moc_ref, noc_ref,
                      wg_ref, gout_ref, wml_ref, wmla_ref, wna_ref, wout_ref, w1_ref, w2_ref, out_ref, *, n_lat_tiles):
    is_ctx = pl.program_id(1) >= n_lat_tiles
    x = jnp.where(is_ctx, xc_ref[0], x_ref[0])
    m = mod_ref[0]
    hb = (_rms(x, g1_ref[...], D_MODEL) * (1.0 + m[1:2]) + m[0:1]).astype(BF16)
    gates = jax.nn.sigmoid(_dot(hb, wg_ref[...]))
    hs = hf_ref[0] + hb_ref[0]
    og = jax.nn.sigmoid(o_ref[0])
    gout = gout_ref[...]
    hn = jnp.concatenate(
        [_rms(hs[:, h * ML_DV:(h + 1) * ML_DV], gout[:, h * ML_DV:(h + 1) * ML_DV], ML_DV) for h in range(ML_HEADS)],
        axis=1)
    y_a = _dot((hn * og).astype(BF16), wml_ref[...])
    y_b = _dot_tn(jnp.where(is_ctx, moc_ref[0], mo_ref[0]), wmla_ref[...])
    y_c = _dot_tn(jnp.where(is_ctx, noc_ref[0], no_ref[0]), wna_ref[...])
    mg = gates[:, 0:D_MODEL] * y_a + gates[:, D_MODEL:2 * D_MODEL] * y_b + gates[:, 2 * D_MODEL:] * y_c
    x1 = x + m[2:3] * _dot(mg.astype(BF16), wout_ref[...])

    h2 = (_rms(x1, g2_ref[...], D_MODEL) * (1.0 + m[4:5]) + m[3:4]).astype(BF16)
    acc = jnp.zeros((TM, D_MODEL), F32)
    fc = 1024
    for c in range(D_FF // fc):
        u = jnp.maximum(_dot(h2, w1_ref[:, c * fc:(c + 1) * fc]), 0.0)
        acc = acc + _dot((u * u).astype(BF16), w2_ref[c * fc:(c + 1) * fc, :])
    out_ref[0] = x1 + m[5:6] * acc


def _block_out(x, xc, ctx_block, mods, g1, g2, hf, hb, o, mla_o, na_o, mla_oc, na_oc, w, nt, n_lat_tiles):
    B = x.shape[0]
    tok = lambda b, t: (b, t, 0)
    lat_t = lambda b, t: (b, 0, jnp.minimum(t, n_lat_tiles - 1))
    ctx_t = lambda b, t: (b, 0, 0)
    in_specs = _stream_specs(ctx_block, n_lat_tiles)
    in_specs += [pl.BlockSpec((1, 8, D_MODEL), lambda b, t: (jnp.where(t >= n_lat_tiles, B, b), 0, 0)),
                 _const_spec(g1.shape), _const_spec(g2.shape),
                 pl.BlockSpec((1, TM, 512), tok), pl.BlockSpec((1, TM, 512), tok), pl.BlockSpec((1, TM, 512), tok),
                 pl.BlockSpec((1, 512, TM), lat_t), pl.BlockSpec((1, 512, TM), lat_t),
                 pl.BlockSpec((1, 512, TM), ctx_t), pl.BlockSpec((1, 512, TM), ctx_t)]
    names = ("wg", "gout", "wml", "wmla", "wna", "wout", "w1", "w2")
    in_specs += [_const_spec(w[k].shape) for k in names]
    return pl.pallas_call(
        functools.partial(_block_out_kernel, n_lat_tiles=n_lat_tiles),
        out_shape=jax.ShapeDtypeStruct((B, nt * TM, D_MODEL), F32),
        grid=(B, nt), in_specs=in_specs, out_specs=pl.BlockSpec((1, TM, D_MODEL), tok),
        compiler_params=_params("arbitrary", "arbitrary"), name="block_out",
    )(x, xc, mods, g1, g2, hf, hb, o, mla_o, na_o, mla_oc, na_oc, *[w[k] for k in names])


def _rope_tables(n_lat, n_ctx):
    nf = MLA_ROPE // 4
    inv = jnp.power(ROPE_THETA, -jnp.arange(nf, dtype=F32) / nf)
    pos = jnp.arange(n_lat)
    ang_r = (pos // GRID_W).astype(F32)[:, None] * inv
    ang_c = (pos % GRID_W).astype(F32)[:, None] * inv
    one = lambda n: jnp.ones((n_lat, n), F32)
    zero = lambda n: jnp.zeros((n_lat, n), F32)
    cr, sr, cc, sc = jnp.cos(ang_r), jnp.sin(ang_r), jnp.cos(ang_c), jnp.sin(ang_c)
    cos = jnp.concatenate([one(MLA_NOPE), cr, cr, cc, cc, one(LANE - MLA_DQK)], axis=1)
    sin = jnp.concatenate([zero(MLA_NOPE), -sr, sr, -sc, sc, zero(LANE - MLA_DQK)], axis=1)
    ident = (jnp.ones((n_ctx, LANE), F32), jnp.zeros((n_ctx, LANE), F32))
    return tuple(jnp.concatenate([t, i], axis=0) for t, i in zip((cos, sin), ident))


def _rope_partner(a):
    half = MLA_ROPE // 4
    lane = np.arange(LANE)
    r = lane - MLA_NOPE
    is_rope = (r >= 0) & (r < MLA_ROPE)
    src = np.where((r // half) % 2 == 0, lane + half, lane - half)
    return jnp.where(is_rope, a[..., np.where(is_rope, src, lane)], 0)


def _layer_weights(l, w_in, ml_i_bias, ml_f_bias, ml_g_out, ml_w_o, mla_g_cq, mla_w_uq, mla_g_ckv, mla_w_ukv,
                   mla_g_q, mla_g_k, mla_w_o, na_g_q, na_g_k, na_w_o, w_out, w_ff1, w_ff2):
    offs = np.cumsum(IN_SIZES)[:-1].tolist()
    mlq, mlk, mlv, mlo, mlg, dq, dkv, kr, naq, nak, nav, gates = jnp.split(w_in[l], offs, axis=-1)
    padc = lambda a, n: jnp.pad(a, ((0, 0), (0, n - a.shape[1])))
    kr_slab = jnp.pad(kr, ((0, 0), (MLA_NOPE, LANE - MLA_DQK)))
    w_uq = jnp.pad(mla_w_uq[l].reshape(MLA_Q_RANK, MLA_HEADS, MLA_DQK), ((0, 0), (0, 0), (0, LANE - MLA_DQK)))
    w_ukv = mla_w_ukv[l].reshape(MLA_KV_RANK, MLA_HEADS, MLA_NOPE + MLA_V)
    w_uk = jnp.pad(w_ukv[..., :MLA_NOPE], ((0, 0), (0, 0), (0, LANE - MLA_NOPE)))
    w_uv = w_ukv[..., MLA_NOPE:].reshape(MLA_KV_RANK, MLA_HEADS * MLA_V)
    row = lambda a: a.reshape(1, -1).astype(F32)
    bias16 = jnp.stack([ml_i_bias[l, 0], ml_f_bias[l, 0], ml_i_bias[l, 1], ml_f_bias[l, 1]]).reshape(-1).astype(F32)
    return {
        "wa": jnp.concatenate([mlq * (ML_DQK ** -0.5), mlv, mlo, padc(mlg, LANE)], axis=1).astype(BF16),
        "wat": jnp.concatenate([mlk.T, mlg.T], axis=0).astype(BF16),
        "wb": jnp.concatenate([dq, dkv, kr_slab, _rope_partner(kr_slab)], axis=1).astype(BF16),
        "wc": jnp.concatenate([naq, nak], axis=1).astype(BF16),
        "wct": nav.T.astype(BF16),
        "wuq": w_uq.reshape(MLA_Q_RANK, MLA_HEADS * LANE).astype(BF16),
        "wuqs": _rope_partner(w_uq).reshape(MLA_Q_RANK, MLA_HEADS * LANE).astype(BF16),
        "wuk": w_uk.reshape(MLA_KV_RANK, MLA_HEADS * LANE).astype(BF16),
        "wuvt": w_uv.T.astype(BF16),
        "gcq": row(mla_g_cq[l]), "gckv": row(mla_g_ckv[l]),
        "gq": padc(row(mla_g_q[l]) * (MLA_DQK ** -0.5 * LOG2E), LANE), "gk": padc(row(mla_g_k[l]), LANE),
        "ngq": row(jnp.tile(na_g_q[l], 2)) * (NA_DH ** -0.5 * LOG2E), "ngk": row(jnp.tile(na_g_k[l], 2)),
        "ng": jnp.concatenate([row(jnp.tile(na_g_q[l], NA_HEADS)) * (NA_DH ** -0.5 * LOG2E),
                               row(jnp.tile(na_g_k[l], NA_HEADS))], axis=1),
        "brow": padc(bias16.reshape(1, 16), LANE), "bcol": bias16.reshape(16, 1),
        "wg": gates.astype(BF16), "gout": row(ml_g_out[l]),
        "wml": ml_w_o[l].astype(BF16), "wmla": mla_w_o[l].astype(BF16), "wna": na_w_o[l].astype(BF16),
        "wout": w_out[l].astype(BF16), "w1": w_ff1[l].astype(BF16), "w2": w_ff2[l].astype(BF16),
    }


def kernel(x, c, ctx, c_ctx, w_mod, b_mod, g_norm1, g_norm2, w_in, ml_i_bias, ml_f_bias, ml_g_out, ml_w_o, mla_g_cq, mla_w_uq, mla_g_ckv, mla_w_ukv, mla_g_q, mla_g_k, mla_w_o, na_g_q, na_g_k, na_rpb, na_w_o, w_out, w_ff1, w_ff2):
    B, T, D = x.shape
    C = ctx.shape[1]
    depth = w_in.shape[0]
    assert D == D_MODEL and C == TM and T % MLA_TQ == 0 and B < 16
    rows = T // GRID_W
    assert rows % NA_Q_ROWS == 0 and rows >= NA_BAND_ROWS
    n_lat_tiles = T // TM

    cc = jnp.zeros((16, D), F32).at[:B].set(c).at[B].set(c_ctx)
    mod = _modulation(cc, w_mod, b_mod).reshape(depth, 16, 6, D)
    mod = jnp.pad(mod, ((0, 0), (0, 0), (0, 2), (0, 0)))
    tabs = _rope_tables(T, C)
    na_bias = _na_bias(na_rpb.astype(F32), rows)
    stream = (x, ctx, 0)

    for l in range(depth):
        last = l == depth - 1
        w = _layer_weights(l, w_in, ml_i_bias, ml_f_bias, ml_g_out, ml_w_o, mla_g_cq, mla_w_uq, mla_g_ckv,
                           mla_w_ukv, mla_g_q, mla_g_k, mla_w_o, na_g_q, na_g_k, na_w_o, w_out, w_ff1, w_ff2)
        g1 = g_norm1[l].reshape(1, D).astype(F32)
        g2 = g_norm2[l].reshape(1, D).astype(F32)
        cos, sin = tabs
        gain_tabs = (cos * w["gq"], sin * _rope_partner(w["gq"]), cos * w["gk"], sin * _rope_partner(w["gk"]))
        q, kt, v, o, gcol, grow, qm, km, vmt, qn, kn, vnt = _inproj(*stream, mod[l], g1, w, gain_tabs, n_lat_tiles)
        hf, hb = _mlstm(q, kt, v, gcol, grow, w["brow"], w["bcol"], n_lat_tiles)
        mla_bound = MLA_DQK * jnp.max(jnp.abs(w["gq"])) * jnp.max(jnp.abs(w["gk"]))
        na_bound = (NA_DH * jnp.max(jnp.abs(w["ngq"])) * jnp.max(jnp.abs(w["ngk"]))
                    + LOG2E * jnp.max(jnp.abs(na_rpb[l].astype(F32))))
        mla_o = _mla((mla_bound <= SCORE_BOUND_LIMIT).astype(jnp.int32).reshape(1), qm, km, vmt, T)
        na_o = _natten((na_bound <= SCORE_BOUND_LIMIT).astype(jnp.int32).reshape(1), qn, kn, vnt, na_bias[l], T)
        mla_oc, na_oc = (mla_o, na_o) if last else _ctx_attn(qm, km, vmt, qn, kn, vnt, T)
        nt = n_lat_tiles if last else n_lat_tiles + 1
        xs = _block_out(*stream, mod[l], g1, g2, hf, hb, o, mla_o, na_o, mla_oc, na_oc, w, nt, n_lat_tiles)
        stream = (xs, xs, n_lat_tiles)
    return xs
```

```python
import functools

import numpy as np
import jax
import jax.numpy as jnp
from jax import lax
from jax.experimental import pallas as pl
from jax.experimental.pallas import tpu as pltpu

F32 = jnp.float32
BF16 = jnp.bfloat16

D_MODEL = 1024
GRID_W = 64
ML_HEADS, ML_DQK, ML_DV = 4, 64, 128
ML_CHUNK = 128
ML_SAMPLES = 1
MLA_HEADS, MLA_Q_RANK, MLA_KV_RANK, MLA_NOPE, MLA_ROPE, MLA_V = 8, 384, 256, 64, 32, 64
MLA_DQK = MLA_NOPE + MLA_ROPE
NA_HEADS, NA_DH, NA_WIN_R, NA_WIN_C = 8, 64, 8, 16
NA_Q_ROWS = 4
NA_BAND_ROWS = NA_Q_ROWS + NA_WIN_R
D_FF = 4 * D_MODEL
ROPE_THETA = 10000.0
EPS = 1e-6
LOG2E = 1.4426950408889634
SCORE_BOUND_LIMIT = 60.0
NEG_INF = -1e30
IN_SIZES = (256, 256, 512, 512, 16, MLA_Q_RANK, MLA_KV_RANK, MLA_ROPE, 512, 512, 512, 3 * D_MODEL)
LANE = 128
TM = 256
MLA_TQ = 1024
VMEM_CAP_MIB = 56


def _dot(a, b):
    return jnp.dot(a, b, preferred_element_type=F32)


def _dot_nt(a, b):
    return lax.dot_general(a, b, (((1,), (1,)), ((), ())), preferred_element_type=F32)


def _dot_tn(a, b):
    return lax.dot_general(a, b, (((0,), (0,)), ((), ())), preferred_element_type=F32)


def _rms(x, g, n):
    ms = jnp.sum(x * x, axis=-1, keepdims=True) * (1.0 / n)
    return x * lax.rsqrt(ms + EPS) * g


def _split3(x):
    hi = x.astype(BF16)
    r = x - hi.astype(F32)
    mid = r.astype(BF16)
    lo = (r - mid.astype(F32)).astype(BF16)
    return hi, mid, lo


def _params(n_axes, vmem_mib):
    assert vmem_mib <= VMEM_CAP_MIB
    return pltpu.CompilerParams(dimension_semantics=("arbitrary",) * n_axes, vmem_limit_bytes=vmem_mib << 20)


def _const_spec(shape):
    nd = len(shape)
    return pl.BlockSpec(shape, lambda *_: (0,) * nd, pipeline_mode=pl.Buffered(1))


def _mod_kernel(c_ref, w_ref, b_ref, o_ref):
    c = c_ref[...]
    s = c * jax.nn.sigmoid(c)
    o_ref[0] = _dot(s, w_ref[0]) + b_ref[0]


def _modulation(cc, w_mod, b_mod):
    L = w_mod.shape[0]
    tn = 1024
    return pl.pallas_call(
        _mod_kernel,
        out_shape=jax.ShapeDtypeStruct((L, 16, 6 * D_MODEL), F32),
        grid=(L, 6 * D_MODEL // tn),
        in_specs=[pl.BlockSpec((16, D_MODEL), lambda l, n: (0, 0)),
                  pl.BlockSpec((1, D_MODEL, tn), lambda l, n: (l, 0, n)),
                  pl.BlockSpec((1, 1, tn), lambda l, n: (l, 0, n))],
        out_specs=pl.BlockSpec((1, 16, tn), lambda l, n: (l, 0, n)),
        compiler_params=_params(2, 16),
        name="modulation",
    )(cc, w_mod, b_mod.reshape(L, 1, 6 * D_MODEL))


def _inproj_kernel(x_ref, xc_ref, mod_ref, g1_ref, wa_ref, wat_ref, wb_ref, wc_ref, wct_ref, wuq_ref, wuqs_ref, wuk_ref,
                   wuvt_ref, gcq_ref, gckv_ref, ng_ref, cgq_ref, sgq_ref, cgk_ref, sgk_ref,
                   q_ref, kt_ref, v_ref, o_ref, gcol_ref, grow_ref, qm_ref, km_ref, vmt_ref, qn_ref, kn_ref, vnt_ref,
                   *, n_lat_tiles):
    x = jnp.where(pl.program_id(1) >= n_lat_tiles, xc_ref[0], x_ref[0])
    m = mod_ref[0]
    hb = (_rms(x, g1_ref[...], D_MODEL) * (1.0 + m[1:2]) + m[0:1]).astype(BF16)

    a = _dot(hb, wa_ref[...])
    q_ref[0] = a[:, 0:256].astype(BF16)
    v_ref[0] = a[:, 256:768].astype(BF16)
    o_ref[0] = a[:, 768:1280]
    gcol_ref[0] = a[:, 1280:1408]
    at = _dot_nt(wat_ref[...], hb)
    kt_ref[0] = at[0:256].astype(BF16)
    grow_ref[0] = at[256:272]

    bs = _dot(hb, wb_ref[...])
    dqn = _rms(bs[:, 0:MLA_Q_RANK], gcq_ref[...], MLA_Q_RANK).astype(BF16)
    dkvn = _rms(bs[:, MLA_Q_RANK:MLA_Q_RANK + MLA_KV_RANK], gckv_ref[...], MLA_KV_RANK).astype(BF16)
    kr = bs[:, MLA_Q_RANK + MLA_KV_RANK:MLA_Q_RANK + MLA_KV_RANK + LANE]
    kr_sw = bs[:, MLA_Q_RANK + MLA_KV_RANK + LANE:]
    qh_all = _dot(dqn, wuq_ref[...])
    qsw_all = _dot(dqn, wuqs_ref[...])
    kh_all = _dot(dkvn, wuk_ref[...]) + jnp.concatenate([kr] * MLA_HEADS, axis=1)
    cgq, sgq, cgk, sgk = cgq_ref[...], sgq_ref[...], cgk_ref[...], sgk_ref[...]
    inv_rms = lambda y: lax.rsqrt(jnp.sum(y * y, axis=-1, keepdims=True) * (1.0 / MLA_DQK) + EPS)
    for h in range(MLA_HEADS):
        sl = slice(h * LANE, (h + 1) * LANE)
        qh, kh = qh_all[:, sl], kh_all[:, sl]
        qm_ref[0, h] = ((qh * cgq + qsw_all[:, sl] * sgq) * inv_rms(qh)).astype(BF16)
        km_ref[0, h] = ((kh * cgk + kr_sw * sgk) * inv_rms(kh)).astype(BF16)
    vmt_ref[0] = _dot_nt(wuvt_ref[...], dkvn).astype(BF16)

    c = _dot(hb, wc_ref[...])
    lo = lax.broadcasted_iota(jnp.int32, (1, LANE), 1) < NA_DH
    ng = ng_ref[...]
    for p in range(NA_HEADS // 2):
        for base, out_ref in ((0, qn_ref), (NA_HEADS * NA_DH, kn_ref)):
            sl = slice(base + p * LANE, base + (p + 1) * LANE)
            s = c[:, sl]
            sq = s * s
            s_lo = jnp.sum(jnp.where(lo, sq, 0.0), axis=-1, keepdims=True)
            s_hi = jnp.sum(jnp.where(lo, 0.0, sq), axis=-1, keepdims=True)
            ms = jnp.where(lo, s_lo, s_hi) * (1.0 / NA_DH)
            out_ref[0, p] = (s * lax.rsqrt(ms + EPS) * ng[:, sl]).astype(BF16)
    vnt_ref[0] = _dot_nt(wct_ref[...], hb).astype(BF16)


def _stream_specs(ctx_block, n_lat_tiles):
    return [pl.BlockSpec((1, TM, D_MODEL), lambda b, t: (b, jnp.minimum(t, n_lat_tiles - 1), 0)),
            pl.BlockSpec((1, TM, D_MODEL), lambda b, t: (b, ctx_block, 0))]


def _inproj(x, xc, ctx_block, mods, g1, w, tabs, n_lat_tiles):
    B = x.shape[0]
    nt = n_lat_tiles + 1
    S = nt * TM
    tok = lambda b, t: (b, t, 0)
    tokT = lambda b, t: (b, 0, t)
    head = lambda b, t: (b, 0, t, 0)
    in_specs = _stream_specs(ctx_block, n_lat_tiles)
    in_specs += [pl.BlockSpec((1, 8, D_MODEL), lambda b, t: (jnp.where(t >= n_lat_tiles, B, b), 0, 0)),
                 _const_spec(g1.shape)]
    consts = ("wa", "wat", "wb", "wc", "wct", "wuq", "wuqs", "wuk", "wuvt", "gcq", "gckv", "ng")
    in_specs += [_const_spec(w[k].shape) for k in consts]
    in_specs += [pl.BlockSpec((TM, LANE), lambda b, t: (t, 0))] * 4
    out_shape = [jax.ShapeDtypeStruct((B, S, 256), BF16),
                 jax.ShapeDtypeStruct((B, 256, S), BF16),
                 jax.ShapeDtypeStruct((B, S, 512), BF16),
                 jax.ShapeDtypeStruct((B, S, 512), F32),
                 jax.ShapeDtypeStruct((B, S, LANE), F32),
                 jax.ShapeDtypeStruct((B, 16, S), F32),
                 jax.ShapeDtypeStruct((B, MLA_HEADS, S, LANE), BF16),
                 jax.ShapeDtypeStruct((B, MLA_HEADS, S, LANE), BF16),
                 jax.ShapeDtypeStruct((B, MLA_HEADS * MLA_V, S), BF16),
                 jax.ShapeDtypeStruct((B, NA_HEADS // 2, S, LANE), BF16),
                 jax.ShapeDtypeStruct((B, NA_HEADS // 2, S, LANE), BF16),
                 jax.ShapeDtypeStruct((B, NA_HEADS * NA_DH, S), BF16)]
    out_specs = [pl.BlockSpec((1, TM, 256), tok), pl.BlockSpec((1, 256, TM), tokT),
                 pl.BlockSpec((1, TM, 512), tok), pl.BlockSpec((1, TM, 512), tok),
                 pl.BlockSpec((1, TM, LANE), tok), pl.BlockSpec((1, 16, TM), tokT),
                 pl.BlockSpec((1, MLA_HEADS, TM, LANE), head), pl.BlockSpec((1, MLA_HEADS, TM, LANE), head),
                 pl.BlockSpec((1, MLA_HEADS * MLA_V, TM), tokT),
                 pl.BlockSpec((1, NA_HEADS // 2, TM, LANE), head), pl.BlockSpec((1, NA_HEADS // 2, TM, LANE), head),
                 pl.BlockSpec((1, NA_HEADS * NA_DH, TM), tokT)]
    return pl.pallas_call(
        functools.partial(_inproj_kernel, n_lat_tiles=n_lat_tiles),
        out_shape=out_shape, grid=(B, nt), in_specs=in_specs, out_specs=out_specs,
        compiler_params=_params(2, 40), name="inproj",
    )(x, xc, mods, g1, *[w[k] for k in consts], *tabs)


def _log_sigmoid(x):
    return jnp.minimum(x, 0.0) - jnp.log1p(jnp.exp(-jnp.abs(x)))


def _mlstm_kernel(qf_ref, ktf_ref, vf_ref, gcf_ref, grf_ref, qb_ref, ktb_ref, vb_ref, gcb_ref, grb_ref,
                  brow_ref, bcol_ref, hf_ref, hb_ref, c_ref, m_ref):
    @pl.when(pl.program_id(1) == 0)
    def _():
        c_ref[...] = jnp.zeros_like(c_ref)
        m_ref[...] = jnp.zeros_like(m_ref)

    dirs = ((qf_ref, ktf_ref, vf_ref, gcf_ref, grf_ref, hf_ref), (qb_ref, ktb_ref, vb_ref, gcb_ref, grb_ref, hb_ref))
    samples = [_mlstm_sample(bb, dirs, brow_ref, bcol_ref, c_ref, m_ref) for bb in range(qf_ref.shape[0])]
    for _ in zip(*samples):
        pass


def _mlstm_sample(bb, dirs, brow_ref, bcol_ref, c_ref, m_ref):
    L = ML_CHUNK
    row = lax.broadcasted_iota(jnp.int32, (L, L), 0)
    col = lax.broadcasted_iota(jnp.int32, (L, L), 1)
    lane = lax.broadcasted_iota(jnp.int32, (1, LANE), 1)
    isf_row = (lane & ML_HEADS) != 0
    sub = lax.broadcasted_iota(jnp.int32, (16, 1), 0)
    isf_col = (sub & ML_HEADS) != 0
    half_mask = [(lane < ML_DQK).astype(F32).astype(BF16), (lane >= ML_DQK).astype(F32).astype(BF16)]
    e0 = jnp.broadcast_to((lane == 0).astype(F32).astype(BF16), (L, LANE))
    for d, (q_ref, kt_ref, v_ref, gc_ref, gr_ref, h_ref) in enumerate(dirs):
        causal = (col <= row) if d == 0 else (col >= row)
        tri = causal.astype(F32).astype(BF16)
        tri_t = ((row <= col) if d == 0 else (row >= col)).astype(F32).astype(BF16)
        local = {}
        for ch in range(TM // L):
            rs = slice(ch * L, (ch + 1) * L)
            pre_c = gc_ref[bb, rs, :] + brow_ref[...]
            val_c = jnp.where(isf_row, _log_sigmoid(pre_c), pre_c)
            pre_r = gr_ref[bb, :, rs] + bcol_ref[...]
            val_r = jnp.where(isf_col, _log_sigmoid(pre_r), pre_r)
            cum_c = sum(_dot(tri, t) for t in _split3(val_c))
            cum_r = sum(_dot(t, tri_t) for t in _split3(val_r))
            tot_r = jnp.sum(val_r, axis=1, keepdims=True)
            for h in range(ML_HEADS):
                ji, jf = (2 * d) * ML_HEADS + h, (2 * d + 1) * ML_HEADS + h
                p, half = h // 2, h % 2
                bcum_c = cum_c[:, jf:jf + 1]
                bcum_r = cum_r[jf:jf + 1, :]
                i_r = val_r[ji:ji + 1, :]
                btot = tot_r[jf:jf + 1, :]
                w_end = btot - bcum_r + i_r
                m_w = jnp.max(w_end, axis=1, keepdims=True)
                dm = jnp.where(causal, bcum_c - bcum_r + i_r, -jnp.inf)
                m_loc = jnp.max(dm, axis=1, keepdims=True)
                q_h = q_ref[bb, rs, p * LANE:(p + 1) * LANE] * half_mask[half]
                kt_pair = kt_ref[bb, p * LANE:(p + 1) * LANE, rs]
                kt_h = kt_ref[bb, h * ML_DQK:(h + 1) * ML_DQK, rs]
                v_ext = jnp.concatenate([v_ref[bb, rs, h * ML_DV:(h + 1) * ML_DV], e0], axis=1)
                s = _dot(q_h, kt_pair) * jnp.exp(dm - m_loc)
                sv = _dot(s.astype(BF16), v_ext)
                kw = (kt_h.astype(F32) * jnp.exp(w_end - m_w)).astype(BF16)
                dc = _dot(kw, v_ext)
                local[ch, h] = (q_h, bcum_c, btot, m_w, m_loc, sv, dc)
                yield
        for cc in range(TM // L):
            ch = cc if d == 0 else TM // L - 1 - cc
            rs = slice(ch * L, (ch + 1) * L)
            for h in range(ML_HEADS):
                p, half = h // 2, h % 2
                idx = d * ML_HEADS + h
                q_h, bcum_c, btot, m_w, m_loc, sv, dc = local[ch, h]
                m_old = m_ref[bb, idx, 0:1, 0:1]
                c_pair = c_ref[bb, d, p]
                inter = bcum_c + m_old
                m_t = jnp.maximum(inter, m_loc)
                tot = jnp.exp(inter - m_t) * _dot(q_h, c_pair.astype(BF16)) + jnp.exp(m_loc - m_t) * sv
                den = tot[:, ML_DV:ML_DV + 1]
                h_ref[bb, rs, h * ML_DV:(h + 1) * ML_DV] = tot[:, 0:ML_DV] / jnp.maximum(jnp.abs(den), jnp.exp(-m_t))
                m_new = jnp.maximum(btot + m_old, m_w)
                hs = slice(half * ML_DQK, (half + 1) * ML_DQK)
                c_ref[bb, d, p, hs, :] = jnp.exp(btot + m_old - m_new) * c_pair[hs, :] + jnp.exp(m_w - m_new) * dc
                m_ref[bb, idx] = jnp.broadcast_to(m_new, (8, LANE))
                yield


def _mlstm(q, kt, v, gcol, grow, brow, bcol, n_lat_tiles):
    B, S, _ = q.shape
    nblk = S // TM
    fwd = lambda j: jnp.where(j == 0, n_lat_tiles, j - 1)
    bwd = lambda j: jnp.where(j == 0, n_lat_tiles, n_lat_tiles - j)

    nb = ML_SAMPLES if B % ML_SAMPLES == 0 else 1

    def specs(order):
        return [pl.BlockSpec((nb, TM, 256), lambda b, j: (b, order(j), 0)),
                pl.BlockSpec((nb, 256, TM), lambda b, j: (b, 0, order(j))),
                pl.BlockSpec((nb, TM, 512), lambda b, j: (b, order(j), 0)),
                pl.BlockSpec((nb, TM, LANE), lambda b, j: (b, order(j), 0)),
                pl.BlockSpec((nb, 16, TM), lambda b, j: (b, 0, order(j)))]

    return pl.pallas_call(
        _mlstm_kernel,
        out_shape=[jax.ShapeDtypeStruct((B, S, 512), F32)] * 2,
        grid=(B // nb, nblk),
        in_specs=specs(fwd) + specs(bwd) + [_const_spec(brow.shape), _const_spec(bcol.shape)],
        out_specs=[pl.BlockSpec((nb, TM, 512), lambda b, j: (b, fwd(j), 0)),
                   pl.BlockSpec((nb, TM, 512), lambda b, j: (b, bwd(j), 0))],
        scratch_shapes=[pltpu.VMEM((nb, 2, ML_HEADS // 2, 2 * ML_DQK, 2 * ML_DV), F32),
                        pltpu.VMEM((nb, 2 * ML_HEADS, 8, LANE), F32)],
        compiler_params=_params(2, 24), name="mlstm",
    )(q, kt, v, gcol, grow, q, kt, v, gcol, grow, brow, bcol)


def _softmax_pv_t(parts):
    ss = []
    for k, q, _, bias in parts:
        s = _dot_nt(k, q)
        ss.append(s if bias is None else s + bias)
    m = functools.reduce(jnp.maximum, [jnp.max(s, axis=0, keepdims=True) for s in ss])
    ps = [jnp.exp2(s - m) for s in ss]
    l = sum(jnp.sum(p, axis=0, keepdims=True) for p in ps)
    o = sum(_dot(vt, p.astype(BF16)) for (_, _, vt, _), p in zip(parts, ps))
    return o / l


def _mla_kernel(bounded_ref, q_ref, k_ref, vt_ref, o_ref, s_scr, p_scr, acc_scr, *, chunks):
    @pl.when(bounded_ref[0] != 0)
    def _():
        _mla_bounded(q_ref, k_ref, vt_ref, o_ref, p_scr, acc_scr, chunks)

    @pl.when(bounded_ref[0] == 0)
    def _():
        _mla_online(q_ref, k_ref, vt_ref, o_ref, s_scr, p_scr, acc_scr, chunks)


def _mla_bounded(q_ref, k_ref, vt_ref, o_ref, p_scr, acc_scr, chunks):
    n_heads = q_ref.shape[1]
    items = [(hh, ci) for hh in range(n_heads) for ci in range(len(chunks))]

    def probs(n):
        hh, ci = items[n]
        off, size = chunks[ci]
        p = jnp.exp2(_dot_nt(k_ref[0, hh, off:off + size, :], q_ref[0, hh]))
        p_scr[n % 2, 0:size, :] = p.astype(BF16)
        return jnp.sum(p, axis=0, keepdims=True)

    lsum = probs(0)
    l = None
    for n, (hh, ci) in enumerate(items):
        off, size = chunks[ci]
        l = lsum if ci == 0 else l + lsum
        if n + 1 < len(items):
            lsum = probs(n + 1)
        pv = _dot(vt_ref[0, hh * MLA_V:(hh + 1) * MLA_V, off:off + size], p_scr[n % 2, 0:size, :])
        if ci == 0:
            acc_scr[hh] = pv
        else:
            acc_scr[hh] += pv
        if ci == len(chunks) - 1:
            o_ref[0, hh * MLA_V:(hh + 1) * MLA_V, :] = (acc_scr[hh] / l).astype(BF16)


def _mla_online(q_ref, k_ref, vt_ref, o_ref, s_scr, p_scr, acc_scr, chunks):
    n_heads, tq = q_ref.shape[1], q_ref.shape[2]
    items = [(hh, ci) for hh in range(n_heads) for ci in range(len(chunks))]

    def scores(n):
        hh, ci = items[n]
        off, size = chunks[ci]
        s_scr[n % 2, 0:size, :] = _dot_nt(k_ref[0, hh, off:off + size, :], q_ref[0, hh])

    scores(0)
    m = l = None
    for n, (hh, ci) in enumerate(items):
        off, size = chunks[ci]
        slot = n % 2
        if n + 1 < len(items):
            scores(n + 1)
        if ci == 0:
            m = [jnp.full((1, LANE), -jnp.inf, F32)] * (tq // LANE)
            l = [jnp.zeros((1, LANE), F32)] * (tq // LANE)
        alphas = []
        for st in range(tq // LANE):
            cs = slice(st * LANE, (st + 1) * LANE)
            m_new = jnp.maximum(m[st], jnp.max(s_scr[slot, 0:size, cs], axis=0, keepdims=True))
            alpha = jnp.exp2(m[st] - m_new)
            p = jnp.exp2(s_scr[slot, 0:size, cs] - m_new)
            l[st] = alpha * l[st] + jnp.sum(p, axis=0, keepdims=True)
            m[st] = m_new
            p_scr[slot, 0:size, cs] = p.astype(BF16)
            alphas.append(alpha)
        pv = _dot(vt_ref[0, hh * MLA_V:(hh + 1) * MLA_V, off:off + size], p_scr[slot, 0:size, :])
        if ci == 0:
            acc_scr[hh] = pv
        else:
            acc_scr[hh] = jnp.concatenate(alphas, axis=1) * acc_scr[hh] + pv
        if ci == len(chunks) - 1:
            o_ref[0, hh * MLA_V:(hh + 1) * MLA_V, :] = (acc_scr[hh] / jnp.concatenate(l, axis=1)).astype(BF16)


def _mla(bounded, qm, km, vmt, n_lat):
    B, H, S, _ = qm.shape
    kv, hp = 512, 2
    chunks = tuple((i * kv, kv) for i in range(n_lat // kv)) + ((n_lat, S - n_lat),)
    return pl.pallas_call(
        functools.partial(_mla_kernel, chunks=chunks),
        out_shape=jax.ShapeDtypeStruct((B, H * MLA_V, n_lat), BF16),
        grid=(B, H // hp, n_lat // MLA_TQ),
        in_specs=[pl.BlockSpec(memory_space=pltpu.SMEM),
                  pl.BlockSpec((1, hp, MLA_TQ, LANE), lambda b, h, i: (b, h, i, 0)),
                  pl.BlockSpec((1, hp, S, LANE), lambda b, h, i: (b, h, 0, 0)),
                  pl.BlockSpec((1, hp * MLA_V, S), lambda b, h, i: (b, h, 0))],
        out_specs=pl.BlockSpec((1, hp * MLA_V, MLA_TQ), lambda b, h, i: (b, h, i)),
        scratch_shapes=[pltpu.VMEM((2, kv, MLA_TQ), F32), pltpu.VMEM((2, kv, MLA_TQ), BF16),
                        pltpu.VMEM((hp, MLA_V, MLA_TQ), F32)],
        compiler_params=_params(3, 32), name="mla_attn",
    )(bounded, qm, km, vmt)


def _natten_kernel(bounded_ref, q_ref, k_ref, vt_ref, bias_ref, o_ref, s_scr, p_scr, *, n_lat, n_ctx, rows):
    blk = pl.program_id(1)
    band = NA_BAND_ROWS * GRID_W
    nk = band + n_ctx
    tq = q_ref.shape[2]
    u0 = jnp.clip(blk * NA_Q_ROWS - NA_WIN_R // 2, 0, rows - NA_BAND_ROWS)
    koff = pl.multiple_of(u0 * GRID_W, 2 * LANE)
    lane = lax.broadcasted_iota(jnp.int32, (1, LANE), 1)
    half_mask = [(lane < NA_DH).astype(F32).astype(BF16), (lane >= NA_DH).astype(F32).astype(BF16)]

    def scores(h):
        qh = q_ref[0, h // 2] * half_mask[h % 2]
        return (_dot_nt(k_ref[0, h // 2, pl.ds(koff, band), :], qh) + bias_ref[h, 0],
                _dot_nt(k_ref[0, h // 2, n_lat:n_lat + n_ctx, :], qh))

    def finish(h, l):
        slot = h % 2
        hs = slice(h * NA_DH, (h + 1) * NA_DH)
        o = (_dot(vt_ref[0, hs, pl.ds(koff, band)], p_scr[slot, 0:band, :])
             + _dot(vt_ref[0, hs, n_lat:n_lat + n_ctx], p_scr[slot, band:nk, :]))
        o_ref[0, hs, :] = (o / l).astype(BF16)

    @pl.when(bounded_ref[0] != 0)
    def _():
        def probs(h):
            pb, pc = (jnp.exp2(s) for s in scores(h))
            p_scr[h % 2, 0:band, :] = pb.astype(BF16)
            p_scr[h % 2, band:nk, :] = pc.astype(BF16)
            return jnp.sum(pb, axis=0, keepdims=True) + jnp.sum(pc, axis=0, keepdims=True)

        l = probs(0)
        for h in range(NA_HEADS):
            l_next = probs(h + 1) if h + 1 < NA_HEADS else None
            finish(h, l)
            l = l_next

    @pl.when(bounded_ref[0] == 0)
    def _():
        def stage(h):
            s_scr[h % 2, 0:band, :], s_scr[h % 2, band:nk, :] = scores(h)

        stage(0)
        for h in range(NA_HEADS):
            slot = h % 2
            if h + 1 < NA_HEADS:
                stage(h + 1)
            ls = []
            for st in range(tq // LANE):
                cs = slice(st * LANE, (st + 1) * LANE)
                m = jnp.max(s_scr[slot, :, cs], axis=0, keepdims=True)
                p = jnp.exp2(s_scr[slot, :, cs] - m)
                ls.append(jnp.sum(p, axis=0, keepdims=True))
                p_scr[slot, :, cs] = p.astype(BF16)
            finish(h, jnp.concatenate(ls, axis=1))


def _natten(bounded, qn, kn, vnt, bias, n_lat):
    B, P, S, _ = qn.shape
    rows = n_lat // GRID_W
    nblk = rows // NA_Q_ROWS
    tq = NA_Q_ROWS * GRID_W
    band = NA_BAND_ROWS * GRID_W
    nk = band + S - n_lat
    btype = lambda i: jnp.where(i == 0, 0, jnp.where(i == nblk - 1, 2, 1))
    return pl.pallas_call(
        functools.partial(_natten_kernel, n_lat=n_lat, n_ctx=S - n_lat, rows=rows),
        out_shape=jax.ShapeDtypeStruct((B, NA_HEADS * NA_DH, n_lat), BF16),
        grid=(B, nblk),
        in_specs=[pl.BlockSpec(memory_space=pltpu.SMEM),
                  pl.BlockSpec((1, P, tq, LANE), lambda b, i: (b, 0, i, 0)),
                  pl.BlockSpec((1, P, S, LANE), lambda b, i: (b, 0, 0, 0)),
                  pl.BlockSpec((1, NA_HEADS * NA_DH, S), lambda b, i: (b, 0, 0)),
                  pl.BlockSpec((NA_HEADS, 1, band, tq), lambda b, i: (0, btype(i), 0, 0))],
        out_specs=pl.BlockSpec((1, NA_HEADS * NA_DH, tq), lambda b, i: (b, 0, i)),
        scratch_shapes=[pltpu.VMEM((2, nk, tq), F32), pltpu.VMEM((2, nk, tq), BF16)],
        compiler_params=_params(2, 48), name="natten",
    )(bounded, qn, kn, vnt, bias)


def _na_bias_kernel(rpb_ref, o_ref, bc_ref, *, rows):
    l, h = pl.program_id(0), pl.program_id(1)
    tq = NA_Q_ROWS * GRID_W
    kc = lax.broadcasted_iota(jnp.int32, (GRID_W, tq), 0)
    ql = lax.broadcasted_iota(jnp.int32, (GRID_W, tq), 1)
    c = ql & (GRID_W - 1)
    qi = ql >> 6
    dc = kc - c + (NA_WIN_C - 1)
    cs = jnp.clip(c - NA_WIN_C // 2, 0, GRID_W - NA_WIN_C)
    col_ok = (kc >= cs) & (kc < cs + NA_WIN_C)
    n_dr, n_dc = 2 * NA_WIN_R - 1, 2 * NA_WIN_C - 1
    base = (l * NA_HEADS + h) * (n_dr * n_dc)
    for dr in range(n_dr):
        acc = jnp.zeros((GRID_W, tq), F32)
        for j in range(n_dc):
            acc = jnp.where(dc == j, rpb_ref[base + dr * n_dc + j], acc)
        bc_ref[dr] = jnp.where(col_ok, acc * LOG2E, NEG_INF)
    nblk = rows // NA_Q_ROWS
    for t, blk in enumerate((0, 1, nblk - 1)):
        r0 = blk * NA_Q_ROWS
        u0 = min(max(r0 - NA_WIN_R // 2, 0), rows - NA_BAND_ROWS)
        for j in range(NA_BAND_ROWS):
            tile = jnp.full((GRID_W, tq), NEG_INF, F32)
            for i in range(NA_Q_ROWS):
                r, kr = r0 + i, u0 + j
                rs = min(max(r - NA_WIN_R // 2, 0), rows - NA_WIN_R)
                if rs <= kr < rs + NA_WIN_R:
                    tile = jnp.where(qi == i, bc_ref[kr - r + NA_WIN_R - 1], tile)
            o_ref[0, 0, t, j * GRID_W:(j + 1) * GRID_W, :] = tile


def _na_bias(rpb, rows):
    L = rpb.shape[0]
    tq = NA_Q_ROWS * GRID_W
    band = NA_BAND_ROWS * GRID_W
    return pl.pallas_call(
        functools.partial(_na_bias_kernel, rows=rows),
        out_shape=jax.ShapeDtypeStruct((L, NA_HEADS, 3, band, tq), F32),
        grid=(L, NA_HEADS),
        in_specs=[pl.BlockSpec(memory_space=pltpu.SMEM)],
        out_specs=pl.BlockSpec((1, 1, 3, band, tq), lambda l, h: (l, h, 0, 0, 0)),
        scratch_shapes=[pltpu.VMEM((2 * NA_WIN_R - 1, GRID_W, tq), F32)],
        compiler_params=_params(2, 16), name="na_bias",
    )(rpb.reshape(-1))


def _ctx_attn_kernel(qm_ref, km_ref, vmt_ref, qn_ref, kn_ref, vnt_ref, mo_ref, no_ref):
    lane = lax.broadcasted_iota(jnp.int32, (1, LANE), 1)
    for h in range(MLA_HEADS):
        hs = slice(h * MLA_V, (h + 1) * MLA_V)
        mo_ref[0, hs, :] = _softmax_pv_t([(km_ref[0, h], qm_ref[0, h], vmt_ref[0, hs, :], None)]).astype(BF16)
        qh = qn_ref[0, h // 2] * ((lane < NA_DH) if h % 2 == 0 else (lane >= NA_DH)).astype(F32).astype(BF16)
        no_ref[0, hs, :] = _softmax_pv_t([(kn_ref[0, h // 2], qh, vnt_ref[0, hs, :], None)]).astype(BF16)


def _ctx_attn(qm, km, vmt, qn, kn, vnt, n_lat):
    B, H, S, _ = qm.shape
    n_ctx = S - n_lat
    t = n_lat // n_ctx
    head = lambda b: (b, 0, t, 0)
    rows_t = lambda b: (b, 0, t)
    out = lambda b: (b, 0, 0)
    return pl.pallas_call(
        _ctx_attn_kernel,
        out_shape=[jax.ShapeDtypeStruct((B, H * MLA_V, n_ctx), BF16),
                   jax.ShapeDtypeStruct((B, NA_HEADS * NA_DH, n_ctx), BF16)],
        grid=(B,),
        in_specs=[pl.BlockSpec((1, H, n_ctx, LANE), head), pl.BlockSpec((1, H, n_ctx, LANE), head),
                  pl.BlockSpec((1, H * MLA_V, n_ctx), rows_t),
                  pl.BlockSpec((1, NA_HEADS // 2, n_ctx, LANE), head), pl.BlockSpec((1, NA_HEADS // 2, n_ctx, LANE), head),
                  pl.BlockSpec((1, NA_HEADS * NA_DH, n_ctx), rows_t)],
        out_specs=[pl.BlockSpec((1, H * MLA_V, n_ctx), out), pl.BlockSpec((1, NA_HEADS * NA_DH, n_ctx), out)],
        compiler_params=_params(1, 16), name="ctx_attn",
    )(qm, km, vmt, qn, kn, vnt)


def _block_out_kernel(x_ref, xc_ref, mod_ref, g1_ref, g2_ref, hf_ref, hb_ref, o_ref, mo_ref, no_ref, moc_ref, noc_ref,
                      wg_ref, gout_ref, wml_ref, wmla_ref, wna_ref, wout_ref, w1_ref, w2_ref, out_ref, *, n_lat_tiles):
    is_ctx = pl.program_id(1) >= n_lat_tiles
    x = jnp.where(is_ctx, xc_ref[0], x_ref[0])
    m = mod_ref[0]
    hb = (_rms(x, g1_ref[...], D_MODEL) * (1.0 + m[1:2]) + m[0:1]).astype(BF16)
    gates = jax.nn.sigmoid(_dot(hb, wg_ref[...]))
    hs = hf_ref[0] + hb_ref[0]
    og = jax.nn.sigmoid(o_ref[0])
    gout = gout_ref[...]
    hn = jnp.concatenate(
        [_rms(hs[:, h * ML_DV:(h + 1) * ML_DV], gout[:, h * ML_DV:(h + 1) * ML_DV], ML_DV) for h in range(ML_HEADS)],
        axis=1)
    y_a = _dot((hn * og).astype(BF16), wml_ref[...])
    y_b = _dot_tn(jnp.where(is_ctx, moc_ref[0], mo_ref[0]), wmla_ref[...])
    y_c = _dot_tn(jnp.where(is_ctx, noc_ref[0], no_ref[0]), wna_ref[...])
    mg = gates[:, 0:D_MODEL] * y_a + gates[:, D_MODEL:2 * D_MODEL] * y_b + gates[:, 2 * D_MODEL:] * y_c
    x1 = x + m[2:3] * _dot(mg.astype(BF16), wout_ref[...])

    h2 = (_rms(x1, g2_ref[...], D_MODEL) * (1.0 + m[4:5]) + m[3:4]).astype(BF16)
    acc = jnp.zeros((TM, D_MODEL), F32)
    fc = 1024
    for c in range(D_FF // fc):
        u = jnp.maximum(_dot(h2, w1_ref[:, c * fc:(c + 1) * fc]), 0.0)
        acc = acc + _dot((u * u).astype(BF16), w2_ref[c * fc:(c + 1) * fc, :])
    out_ref[0] = x1 + m[5:6] * acc


def _block_out(x, xc, ctx_block, mods, g1, g2, hf, hb, o, mla_o, na_o, mla_oc, na_oc, w, nt, n_lat_tiles):
    B = x.shape[0]
    tok = lambda b, t: (b, t, 0)
    lat_t = lambda b, t: (b, 0, jnp.minimum(t, n_lat_tiles - 1))
    ctx_t = lambda b, t: (b, 0, 0)
    in_specs = _stream_specs(ctx_block, n_lat_tiles)
    in_specs += [pl.BlockSpec((1, 8, D_MODEL), lambda b, t: (jnp.where(t >= n_lat_tiles, B, b), 0, 0)),
                 _const_spec(g1.shape), _const_spec(g2.shape),
                 pl.BlockSpec((1, TM, 512), tok), pl.BlockSpec((1, TM, 512), tok), pl.BlockSpec((1, TM, 512), tok),
                 pl.BlockSpec((1, 512, TM), lat_t), pl.BlockSpec((1, 512, TM), lat_t),
                 pl.BlockSpec((1, 512, TM), ctx_t), pl.BlockSpec((1, 512, TM), ctx_t)]
    names = ("wg", "gout", "wml", "wmla", "wna", "wout", "w1", "w2")
    in_specs += [_const_spec(w[k].shape) for k in names]
    return pl.pallas_call(
        functools.partial(_block_out_kernel, n_lat_tiles=n_lat_tiles),
        out_shape=jax.ShapeDtypeStruct((B, nt * TM, D_MODEL), F32),
        grid=(B, nt), in_specs=in_specs, out_specs=pl.BlockSpec((1, TM, D_MODEL), tok),
        compiler_params=_params(2, VMEM_CAP_MIB), name="block_out",
    )(x, xc, mods, g1, g2, hf, hb, o, mla_o, na_o, mla_oc, na_oc, *[w[k] for k in names])


def _rope_tables(n_lat, n_ctx):
    nf = MLA_ROPE // 4
    inv = jnp.power(ROPE_THETA, -jnp.arange(nf, dtype=F32) / nf)
    pos = jnp.arange(n_lat)
    ang_r = (pos // GRID_W).astype(F32)[:, None] * inv
    ang_c = (pos % GRID_W).astype(F32)[:, None] * inv
    one = lambda n: jnp.ones((n_lat, n), F32)
    zero = lambda n: jnp.zeros((n_lat, n), F32)
    cr, sr, cc, sc = jnp.cos(ang_r), jnp.sin(ang_r), jnp.cos(ang_c), jnp.sin(ang_c)
    cos = jnp.concatenate([one(MLA_NOPE), cr, cr, cc, cc, one(LANE - MLA_DQK)], axis=1)
    sin = jnp.concatenate([zero(MLA_NOPE), -sr, sr, -sc, sc, zero(LANE - MLA_DQK)], axis=1)
    ident = (jnp.ones((n_ctx, LANE), F32), jnp.zeros((n_ctx, LANE), F32))
    return tuple(jnp.concatenate([t, i], axis=0) for t, i in zip((cos, sin), ident))


def _rope_partner(a):
    half = MLA_ROPE // 4
    lane = np.arange(LANE)
    r = lane - MLA_NOPE
    is_rope = (r >= 0) & (r < MLA_ROPE)
    src = np.where((r // half) % 2 == 0, lane + half, lane - half)
    return jnp.where(is_rope, a[..., np.where(is_rope, src, lane)], 0)


def _layer_weights(l, w_in, ml_i_bias, ml_f_bias, ml_g_out, ml_w_o, mla_g_cq, mla_w_uq, mla_g_ckv, mla_w_ukv,
                   mla_g_q, mla_g_k, mla_w_o, na_g_q, na_g_k, na_w_o, w_out, w_ff1, w_ff2):
    offs = np.cumsum(IN_SIZES)[:-1].tolist()
    mlq, mlk, mlv, mlo, mlg, dq, dkv, kr, naq, nak, nav, gates = jnp.split(w_in[l], offs, axis=-1)
    padc = lambda a, n: jnp.pad(a, ((0, 0), (0, n - a.shape[1])))
    kr_slab = jnp.pad(kr, ((0, 0), (MLA_NOPE, LANE - MLA_DQK)))
    w_uq = jnp.pad(mla_w_uq[l].reshape(MLA_Q_RANK, MLA_HEADS, MLA_DQK), ((0, 0), (0, 0), (0, LANE - MLA_DQK)))
    w_ukv = mla_w_ukv[l].reshape(MLA_KV_RANK, MLA_HEADS, MLA_NOPE + MLA_V)
    w_uk = jnp.pad(w_ukv[..., :MLA_NOPE], ((0, 0), (0, 0), (0, LANE - MLA_NOPE)))
    w_uv = w_ukv[..., MLA_NOPE:].reshape(MLA_KV_RANK, MLA_HEADS * MLA_V)
    row = lambda a: a.reshape(1, -1).astype(F32)
    bias16 = jnp.stack([ml_i_bias[l, 0], ml_f_bias[l, 0], ml_i_bias[l, 1], ml_f_bias[l, 1]]).reshape(-1).astype(F32)
    return {
        "wa": jnp.concatenate([mlq * (ML_DQK ** -0.5), mlv, mlo, padc(mlg, LANE)], axis=1).astype(BF16),
        "wat": jnp.concatenate([mlk.T, mlg.T], axis=0).astype(BF16),
        "wb": jnp.concatenate([dq, dkv, kr_slab, _rope_partner(kr_slab)], axis=1).astype(BF16),
        "wc": jnp.concatenate([naq, nak], axis=1).astype(BF16),
        "wct": nav.T.astype(BF16),
        "wuq": w_uq.reshape(MLA_Q_RANK, MLA_HEADS * LANE).astype(BF16),
        "wuqs": _rope_partner(w_uq).reshape(MLA_Q_RANK, MLA_HEADS * LANE).astype(BF16),
        "wuk": w_uk.reshape(MLA_KV_RANK, MLA_HEADS * LANE).astype(BF16),
        "wuvt": w_uv.T.astype(BF16),
        "gcq": row(mla_g_cq[l]), "gckv": row(mla_g_ckv[l]),
        "gq": padc(row(mla_g_q[l]) * (MLA_DQK ** -0.5 * LOG2E), LANE), "gk": padc(row(mla_g_k[l]), LANE),
        "ngq": row(jnp.tile(na_g_q[l], 2)) * (NA_DH ** -0.5 * LOG2E), "ngk": row(jnp.tile(na_g_k[l], 2)),
        "ng": jnp.concatenate([row(jnp.tile(na_g_q[l], NA_HEADS)) * (NA_DH ** -0.5 * LOG2E),
                               row(jnp.tile(na_g_k[l], NA_HEADS))], axis=1),
        "brow": padc(bias16.reshape(1, 16), LANE), "bcol": bias16.reshape(16, 1),
        "wg": gates.astype(BF16), "gout": row(ml_g_out[l]),
        "wml": ml_w_o[l].astype(BF16), "wmla": mla_w_o[l].astype(BF16), "wna": na_w_o[l].astype(BF16),
        "wout": w_out[l].astype(BF16), "w1": w_ff1[l].astype(BF16), "w2": w_ff2[l].astype(BF16),
    }


def kernel(x, c, ctx, c_ctx, w_mod, b_mod, g_norm1, g_norm2, w_in, ml_i_bias, ml_f_bias, ml_g_out, ml_w_o, mla_g_cq, mla_w_uq, mla_g_ckv, mla_w_ukv, mla_g_q, mla_g_k, mla_w_o, na_g_q, na_g_k, na_rpb, na_w_o, w_out, w_ff1, w_ff2):
    B, T, D = x.shape
    C = ctx.shape[1]
    depth = w_in.shape[0]
    assert D == D_MODEL and C == TM and T % MLA_TQ == 0 and B < 16
    rows = T // GRID_W
    assert rows % NA_Q_ROWS == 0 and rows >= NA_BAND_ROWS
    n_lat_tiles = T // TM

    cc = jnp.zeros((16, D), F32).at[:B].set(c).at[B].set(c_ctx)
    mod = _modulation(cc, w_mod, b_mod).reshape(depth, 16, 6, D)
    mod = jnp.pad(mod, ((0, 0), (0, 0), (0, 2), (0, 0)))
    tabs = _rope_tables(T, C)
    na_bias = _na_bias(na_rpb.astype(F32), rows)
    stream = (x, ctx, 0)

    for l in range(depth):
        last = l == depth - 1
        w = _layer_weights(l, w_in, ml_i_bias, ml_f_bias, ml_g_out, ml_w_o, mla_g_cq, mla_w_uq, mla_g_ckv,
                           mla_w_ukv, mla_g_q, mla_g_k, mla_w_o, na_g_q, na_g_k, na_w_o, w_out, w_ff1, w_ff2)
        g1 = g_norm1[l].reshape(1, D).astype(F32)
        g2 = g_norm2[l].reshape(1, D).astype(F32)
        cos, sin = tabs
        gain_tabs = (cos * w["gq"], sin * _rope_partner(w["gq"]), cos * w["gk"], sin * _rope_partner(w["gk"]))
        q, kt, v, o, gcol, grow, qm, km, vmt, qn, kn, vnt = _inproj(*stream, mod[l], g1, w, gain_tabs, n_lat_tiles)
        hf, hb = _mlstm(q, kt, v, gcol, grow, w["brow"], w["bcol"], n_lat_tiles)
        mla_bound = MLA_DQK * jnp.max(jnp.abs(w["gq"])) * jnp.max(jnp.abs(w["gk"]))
        na_bound = (NA_DH * jnp.max(jnp.abs(w["ngq"])) * jnp.max(jnp.abs(w["ngk"]))
                    + LOG2E * jnp.max(jnp.abs(na_rpb[l].astype(F32))))
        mla_o = _mla((mla_bound <= SCORE_BOUND_LIMIT).astype(jnp.int32).reshape(1), qm, km, vmt, T)
        na_o = _natten((na_bound <= SCORE_BOUND_LIMIT).astype(jnp.int32).reshape(1), qn, kn, vnt, na_bias[l], T)
        mla_oc, na_oc = (mla_o, na_o) if last else _ctx_attn(qm, km, vmt, qn, kn, vnt, T)
        nt = n_lat_tiles if last else n_lat_tiles + 1
        xs = _block_out(*stream, mod[l], g1, g2, hf, hb, o, mla_o, na_o, mla_oc, na_oc, w, nt, n_lat_tiles)
        stream = (xs, xs, n_lat_tiles)
    return xs
```

```python
import functools

import numpy as np
import jax
import jax.numpy as jnp
from jax import lax
from jax.experimental import pallas as pl
from jax.experimental.pallas import tpu as pltpu

F32 = jnp.float32
BF16 = jnp.bfloat16

D_MODEL = 1024
GRID_W = 64
ML_HEADS, ML_DQK, ML_DV = 4, 64, 128
ML_CHUNK = 128
ML_SAMPLES = 1
MLA_HEADS, MLA_Q_RANK, MLA_KV_RANK, MLA_NOPE, MLA_ROPE, MLA_V = 8, 384, 256, 64, 32, 64
MLA_DQK = MLA_NOPE + MLA_ROPE
NA_HEADS, NA_DH, NA_WIN_R, NA_WIN_C = 8, 64, 8, 16
NA_Q_ROWS = 4
NA_BAND_ROWS = NA_Q_ROWS + NA_WIN_R
D_FF = 4 * D_MODEL
ROPE_THETA = 10000.0
EPS = 1e-6
LOG2E = 1.4426950408889634
SCORE_BOUND_LIMIT = 60.0
NEG_INF = -1e30
IN_SIZES = (256, 256, 512, 512, 16, MLA_Q_RANK, MLA_KV_RANK, MLA_ROPE, 512, 512, 512, 3 * D_MODEL)
LANE = 128
TM = 256
MLA_TQ = 1024
VMEM_CAP_MIB = 56


def _dot(a, b):
    return jnp.dot(a, b, preferred_element_type=F32)


def _dot_nt(a, b):
    return lax.dot_general(a, b, (((1,), (1,)), ((), ())), preferred_element_type=F32)


def _dot_tn(a, b):
    return lax.dot_general(a, b, (((0,), (0,)), ((), ())), preferred_element_type=F32)


def _rms(x, g, n):
    ms = jnp.sum(x * x, axis=-1, keepdims=True) * (1.0 / n)
    return x * lax.rsqrt(ms + EPS) * g


def _split3(x):
    hi = x.astype(BF16)
    r = x - hi.astype(F32)
    mid = r.astype(BF16)
    lo = (r - mid.astype(F32)).astype(BF16)
    return hi, mid, lo


def _params(n_axes, vmem_mib=VMEM_CAP_MIB):
    assert vmem_mib <= VMEM_CAP_MIB
    return pltpu.CompilerParams(dimension_semantics=("arbitrary",) * n_axes, vmem_limit_bytes=vmem_mib << 20)


def _const_spec(shape):
    nd = len(shape)
    return pl.BlockSpec(shape, lambda *_: (0,) * nd, pipeline_mode=pl.Buffered(1))


def _mod_kernel(c_ref, w_ref, b_ref, o_ref):
    c = c_ref[...]
    s = c * jax.nn.sigmoid(c)
    o_ref[0] = _dot(s, w_ref[0]) + b_ref[0]


def _modulation(cc, w_mod, b_mod):
    L = w_mod.shape[0]
    tn = 1024
    return pl.pallas_call(
        _mod_kernel,
        out_shape=jax.ShapeDtypeStruct((L, 16, 6 * D_MODEL), F32),
        grid=(L, 6 * D_MODEL // tn),
        in_specs=[pl.BlockSpec((16, D_MODEL), lambda l, n: (0, 0)),
                  pl.BlockSpec((1, D_MODEL, tn), lambda l, n: (l, 0, n)),
                  pl.BlockSpec((1, 1, tn), lambda l, n: (l, 0, n))],
        out_specs=pl.BlockSpec((1, 16, tn), lambda l, n: (l, 0, n)),
        compiler_params=_params(2),
        name="modulation",
    )(cc, w_mod, b_mod.reshape(L, 1, 6 * D_MODEL))


def _inproj_kernel(x_ref, xc_ref, mod_ref, g1_ref, wa_ref, wat_ref, wb_ref, wc_ref, wct_ref, wuq_ref, wuqs_ref, wuk_ref,
                   wuvt_ref, gcq_ref, gckv_ref, ng_ref, cgq_ref, sgq_ref, cgk_ref, sgk_ref,
                   q_ref, kt_ref, v_ref, o_ref, gcol_ref, grow_ref, qm_ref, km_ref, vmt_ref, qn_ref, kn_ref, vnt_ref,
                   *, n_lat_tiles):
    x = jnp.where(pl.program_id(1) >= n_lat_tiles, xc_ref[0], x_ref[0])
    m = mod_ref[0]
    hb = (_rms(x, g1_ref[...], D_MODEL) * (1.0 + m[1:2]) + m[0:1]).astype(BF16)

    a = _dot(hb, wa_ref[...])
    q_ref[0] = a[:, 0:256].astype(BF16)
    v_ref[0] = a[:, 256:768].astype(BF16)
    o_ref[0] = a[:, 768:1280]
    gcol_ref[0] = a[:, 1280:1408]
    at = _dot_nt(wat_ref[...], hb)
    kt_ref[0] = at[0:256].astype(BF16)
    grow_ref[0] = at[256:272]

    bs = _dot(hb, wb_ref[...])
    dqn = _rms(bs[:, 0:MLA_Q_RANK], gcq_ref[...], MLA_Q_RANK).astype(BF16)
    dkvn = _rms(bs[:, MLA_Q_RANK:MLA_Q_RANK + MLA_KV_RANK], gckv_ref[...], MLA_KV_RANK).astype(BF16)
    kr = bs[:, MLA_Q_RANK + MLA_KV_RANK:MLA_Q_RANK + MLA_KV_RANK + LANE]
    kr_sw = bs[:, MLA_Q_RANK + MLA_KV_RANK + LANE:]
    qh_all = _dot(dqn, wuq_ref[...])
    qsw_all = _dot(dqn, wuqs_ref[...])
    kh_all = _dot(dkvn, wuk_ref[...]) + jnp.concatenate([kr] * MLA_HEADS, axis=1)
    cgq, sgq, cgk, sgk = cgq_ref[...], sgq_ref[...], cgk_ref[...], sgk_ref[...]
    inv_rms = lambda y: lax.rsqrt(jnp.sum(y * y, axis=-1, keepdims=True) * (1.0 / MLA_DQK) + EPS)
    for h in range(MLA_HEADS):
        sl = slice(h * LANE, (h + 1) * LANE)
        qh, kh = qh_all[:, sl], kh_all[:, sl]
        qm_ref[0, h] = ((qh * cgq + qsw_all[:, sl] * sgq) * inv_rms(qh)).astype(BF16)
        km_ref[0, h] = ((kh * cgk + kr_sw * sgk) * inv_rms(kh)).astype(BF16)
    vmt_ref[0] = _dot_nt(wuvt_ref[...], dkvn).astype(BF16)

    c = _dot(hb, wc_ref[...])
    lo = lax.broadcasted_iota(jnp.int32, (1, LANE), 1) < NA_DH
    ng = ng_ref[...]
    for p in range(NA_HEADS // 2):
        for base, out_ref in ((0, qn_ref), (NA_HEADS * NA_DH, kn_ref)):
            sl = slice(base + p * LANE, base + (p + 1) * LANE)
            s = c[:, sl]
            sq = s * s
            s_lo = jnp.sum(jnp.where(lo, sq, 0.0), axis=-1, keepdims=True)
            s_hi = jnp.sum(jnp.where(lo, 0.0, sq), axis=-1, keepdims=True)
            ms = jnp.where(lo, s_lo, s_hi) * (1.0 / NA_DH)
            out_ref[0, p] = (s * lax.rsqrt(ms + EPS) * ng[:, sl]).astype(BF16)
    vnt_ref[0] = _dot_nt(wct_ref[...], hb).astype(BF16)


def _stream_specs(ctx_block, n_lat_tiles):
    return [pl.BlockSpec((1, TM, D_MODEL), lambda b, t: (b, jnp.minimum(t, n_lat_tiles - 1), 0)),
            pl.BlockSpec((1, TM, D_MODEL), lambda b, t: (b, ctx_block, 0))]


def _inproj(x, xc, ctx_block, mods, g1, w, tabs, n_lat_tiles):
    B = x.shape[0]
    nt = n_lat_tiles + 1
    S = nt * TM
    tok = lambda b, t: (b, t, 0)
    tokT = lambda b, t: (b, 0, t)
    head = lambda b, t: (b, 0, t, 0)
    in_specs = _stream_specs(ctx_block, n_lat_tiles)
    in_specs += [pl.BlockSpec((1, 8, D_MODEL), lambda b, t: (jnp.where(t >= n_lat_tiles, B, b), 0, 0)),
                 _const_spec(g1.shape)]
    consts = ("wa", "wat", "wb", "wc", "wct", "wuq", "wuqs", "wuk", "wuvt", "gcq", "gckv", "ng")
    in_specs += [_const_spec(w[k].shape) for k in consts]
    in_specs += [pl.BlockSpec((TM, LANE), lambda b, t: (t, 0))] * 4
    out_shape = [jax.ShapeDtypeStruct((B, S, 256), BF16),
                 jax.ShapeDtypeStruct((B, 256, S), BF16),
                 jax.ShapeDtypeStruct((B, S, 512), BF16),
                 jax.ShapeDtypeStruct((B, S, 512), F32),
                 jax.ShapeDtypeStruct((B, S, LANE), F32),
                 jax.ShapeDtypeStruct((B, 16, S), F32),
                 jax.ShapeDtypeStruct((B, MLA_HEADS, S, LANE), BF16),
                 jax.ShapeDtypeStruct((B, MLA_HEADS, S, LANE), BF16),
                 jax.ShapeDtypeStruct((B, MLA_HEADS * MLA_V, S), BF16),
                 jax.ShapeDtypeStruct((B, NA_HEADS // 2, S, LANE), BF16),
                 jax.ShapeDtypeStruct((B, NA_HEADS // 2, S, LANE), BF16),
                 jax.ShapeDtypeStruct((B, NA_HEADS * NA_DH, S), BF16)]
    out_specs = [pl.BlockSpec((1, TM, 256), tok), pl.BlockSpec((1, 256, TM), tokT),
                 pl.BlockSpec((1, TM, 512), tok), pl.BlockSpec((1, TM, 512), tok),
                 pl.BlockSpec((1, TM, LANE), tok), pl.BlockSpec((1, 16, TM), tokT),
                 pl.BlockSpec((1, MLA_HEADS, TM, LANE), head), pl.BlockSpec((1, MLA_HEADS, TM, LANE), head),
                 pl.BlockSpec((1, MLA_HEADS * MLA_V, TM), tokT),
                 pl.BlockSpec((1, NA_HEADS // 2, TM, LANE), head), pl.BlockSpec((1, NA_HEADS // 2, TM, LANE), head),
                 pl.BlockSpec((1, NA_HEADS * NA_DH, TM), tokT)]
    return pl.pallas_call(
        functools.partial(_inproj_kernel, n_lat_tiles=n_lat_tiles),
        out_shape=out_shape, grid=(B, nt), in_specs=in_specs, out_specs=out_specs,
        compiler_params=_params(2, 40), name="inproj",
    )(x, xc, mods, g1, *[w[k] for k in consts], *tabs)


def _log_sigmoid(x):
    return jnp.minimum(x, 0.0) - jnp.log1p(jnp.exp(-jnp.abs(x)))


def _mlstm_kernel(qf_ref, ktf_ref, vf_ref, gcf_ref, grf_ref, qb_ref, ktb_ref, vb_ref, gcb_ref, grb_ref,
                  brow_ref, bcol_ref, hf_ref, hb_ref, c_ref, m_ref):
    @pl.when(pl.program_id(1) == 0)
    def _():
        c_ref[...] = jnp.zeros_like(c_ref)
        m_ref[...] = jnp.zeros_like(m_ref)

    dirs = ((qf_ref, ktf_ref, vf_ref, gcf_ref, grf_ref, hf_ref), (qb_ref, ktb_ref, vb_ref, gcb_ref, grb_ref, hb_ref))
    samples = [_mlstm_sample(bb, dirs, brow_ref, bcol_ref, c_ref, m_ref) for bb in range(qf_ref.shape[0])]
    for _ in zip(*samples):
        pass


def _mlstm_sample(bb, dirs, brow_ref, bcol_ref, c_ref, m_ref):
    L = ML_CHUNK
    row = lax.broadcasted_iota(jnp.int32, (L, L), 0)
    col = lax.broadcasted_iota(jnp.int32, (L, L), 1)
    lane = lax.broadcasted_iota(jnp.int32, (1, LANE), 1)
    isf_row = (lane & ML_HEADS) != 0
    sub = lax.broadcasted_iota(jnp.int32, (16, 1), 0)
    isf_col = (sub & ML_HEADS) != 0
    half_mask = [(lane < ML_DQK).astype(F32).astype(BF16), (lane >= ML_DQK).astype(F32).astype(BF16)]
    e0 = jnp.broadcast_to((lane == 0).astype(F32).astype(BF16), (L, LANE))
    for d, (q_ref, kt_ref, v_ref, gc_ref, gr_ref, h_ref) in enumerate(dirs):
        causal = (col <= row) if d == 0 else (col >= row)
        tri = causal.astype(F32).astype(BF16)
        tri_t = ((row <= col) if d == 0 else (row >= col)).astype(F32).astype(BF16)
        local = {}
        for ch in range(TM // L):
            rs = slice(ch * L, (ch + 1) * L)
            pre_c = gc_ref[bb, rs, :] + brow_ref[...]
            val_c = jnp.where(isf_row, _log_sigmoid(pre_c), pre_c)
            pre_r = gr_ref[bb, :, rs] + bcol_ref[...]
            val_r = jnp.where(isf_col, _log_sigmoid(pre_r), pre_r)
            cum_c = sum(_dot(tri, t) for t in _split3(val_c))
            cum_r = sum(_dot(t, tri_t) for t in _split3(val_r))
            tot_r = jnp.sum(val_r, axis=1, keepdims=True)
            for h in range(ML_HEADS):
                ji, jf = (2 * d) * ML_HEADS + h, (2 * d + 1) * ML_HEADS + h
                p, half = h // 2, h % 2
                bcum_c = cum_c[:, jf:jf + 1]
                bcum_r = cum_r[jf:jf + 1, :]
                i_r = val_r[ji:ji + 1, :]
                btot = tot_r[jf:jf + 1, :]
                w_end = btot - bcum_r + i_r
                m_w = jnp.max(w_end, axis=1, keepdims=True)
                dm = jnp.where(causal, bcum_c - bcum_r + i_r, -jnp.inf)
                m_loc = jnp.max(dm, axis=1, keepdims=True)
                q_h = q_ref[bb, rs, p * LANE:(p + 1) * LANE] * half_mask[half]
                kt_pair = kt_ref[bb, p * LANE:(p + 1) * LANE, rs]
                kt_h = kt_ref[bb, h * ML_DQK:(h + 1) * ML_DQK, rs]
                v_ext = jnp.concatenate([v_ref[bb, rs, h * ML_DV:(h + 1) * ML_DV], e0], axis=1)
                s = _dot(q_h, kt_pair) * jnp.exp(dm - m_loc)
                sv = _dot(s.astype(BF16), v_ext)
                kw = (kt_h.astype(F32) * jnp.exp(w_end - m_w)).astype(BF16)
                dc = _dot(kw, v_ext)
                local[ch, h] = (q_h, bcum_c, btot, m_w, m_loc, sv, dc)
                yield
        for cc in range(TM // L):
            ch = cc if d == 0 else TM // L - 1 - cc
            rs = slice(ch * L, (ch + 1) * L)
            for h in range(ML_HEADS):
                p, half = h // 2, h % 2
                idx = d * ML_HEADS + h
                q_h, bcum_c, btot, m_w, m_loc, sv, dc = local[ch, h]
                m_old = m_ref[bb, idx, 0:1, 0:1]
                c_pair = c_ref[bb, d, p]
                inter = bcum_c + m_old
                m_t = jnp.maximum(inter, m_loc)
                tot = jnp.exp(inter - m_t) * _dot(q_h, c_pair.astype(BF16)) + jnp.exp(m_loc - m_t) * sv
                den = tot[:, ML_DV:ML_DV + 1]
                h_ref[bb, rs, h * ML_DV:(h + 1) * ML_DV] = tot[:, 0:ML_DV] / jnp.maximum(jnp.abs(den), jnp.exp(-m_t))
                m_new = jnp.maximum(btot + m_old, m_w)
                hs = slice(half * ML_DQK, (half + 1) * ML_DQK)
                c_ref[bb, d, p, hs, :] = jnp.exp(btot + m_old - m_new) * c_pair[hs, :] + jnp.exp(m_w - m_new) * dc
                m_ref[bb, idx] = jnp.broadcast_to(m_new, (8, LANE))
                yield


def _mlstm(q, kt, v, gcol, grow, brow, bcol, n_lat_tiles):
    B, S, _ = q.shape
    nblk = S // TM
    fwd = lambda j: jnp.where(j == 0, n_lat_tiles, j - 1)
    bwd = lambda j: jnp.where(j == 0, n_lat_tiles, n_lat_tiles - j)

    nb = ML_SAMPLES if B % ML_SAMPLES == 0 else 1

    def specs(order):
        return [pl.BlockSpec((nb, TM, 256), lambda b, j: (b, order(j), 0)),
                pl.BlockSpec((nb, 256, TM), lambda b, j: (b, 0, order(j))),
                pl.BlockSpec((nb, TM, 512), lambda b, j: (b, order(j), 0)),
                pl.BlockSpec((nb, TM, LANE), lambda b, j: (b, order(j), 0)),
                pl.BlockSpec((nb, 16, TM), lambda b, j: (b, 0, order(j)))]

    return pl.pallas_call(
        _mlstm_kernel,
        out_shape=[jax.ShapeDtypeStruct((B, S, 512), F32)] * 2,
        grid=(B // nb, nblk),
        in_specs=specs(fwd) + specs(bwd) + [_const_spec(brow.shape), _const_spec(bcol.shape)],
        out_specs=[pl.BlockSpec((nb, TM, 512), lambda b, j: (b, fwd(j), 0)),
                   pl.BlockSpec((nb, TM, 512), lambda b, j: (b, bwd(j), 0))],
        scratch_shapes=[pltpu.VMEM((nb, 2, ML_HEADS // 2, 2 * ML_DQK, 2 * ML_DV), F32),
                        pltpu.VMEM((nb, 2 * ML_HEADS, 8, LANE), F32)],
        compiler_params=_params(2), name="mlstm",
    )(q, kt, v, gcol, grow, q, kt, v, gcol, grow, brow, bcol)


def _softmax_pv_t(parts):
    ss = []
    for k, q, _, bias in parts:
        s = _dot_nt(k, q)
        ss.append(s if bias is None else s + bias)
    m = functools.reduce(jnp.maximum, [jnp.max(s, axis=0, keepdims=True) for s in ss])
    ps = [jnp.exp2(s - m) for s in ss]
    l = sum(jnp.sum(p, axis=0, keepdims=True) for p in ps)
    o = sum(_dot(vt, p.astype(BF16)) for (_, _, vt, _), p in zip(parts, ps))
    return o / l


def _mla_kernel(bounded_ref, q_ref, k_ref, vt_ref, o_ref, s_scr, p_scr, acc_scr, *, chunks):
    @pl.when(bounded_ref[0] != 0)
    def _():
        _mla_bounded(q_ref, k_ref, vt_ref, o_ref, p_scr, acc_scr, chunks)

    @pl.when(bounded_ref[0] == 0)
    def _():
        _mla_online(q_ref, k_ref, vt_ref, o_ref, s_scr, p_scr, acc_scr, chunks)


def _mla_bounded(q_ref, k_ref, vt_ref, o_ref, p_scr, acc_scr, chunks):
    n_heads = q_ref.shape[1]
    items = [(hh, ci) for hh in range(n_heads) for ci in range(len(chunks))]

    def probs(n):
        hh, ci = items[n]
        off, size = chunks[ci]
        p = jnp.exp2(_dot_nt(k_ref[0, hh, off:off + size, :], q_ref[0, hh]))
        p_scr[n % 2, 0:size, :] = p.astype(BF16)
        return jnp.sum(p, axis=0, keepdims=True)

    lsum = probs(0)
    l = None
    for n, (hh, ci) in enumerate(items):
        off, size = chunks[ci]
        l = lsum if ci == 0 else l + lsum
        if n + 1 < len(items):
            lsum = probs(n + 1)
        pv = _dot(vt_ref[0, hh * MLA_V:(hh + 1) * MLA_V, off:off + size], p_scr[n % 2, 0:size, :])
        if ci == 0:
            acc_scr[hh] = pv
        else:
            acc_scr[hh] += pv
        if ci == len(chunks) - 1:
            o_ref[0, hh * MLA_V:(hh + 1) * MLA_V, :] = (acc_scr[hh] / l).astype(BF16)


def _mla_online(q_ref, k_ref, vt_ref, o_ref, s_scr, p_scr, acc_scr, chunks):
    n_heads, tq = q_ref.shape[1], q_ref.shape[2]
    items = [(hh, ci) for hh in range(n_heads) for ci in range(len(chunks))]

    def scores(n):
        hh, ci = items[n]
        off, size = chunks[ci]
        s_scr[n % 2, 0:size, :] = _dot_nt(k_ref[0, hh, off:off + size, :], q_ref[0, hh])

    scores(0)
    m = l = None
    for n, (hh, ci) in enumerate(items):
        off, size = chunks[ci]
        slot = n % 2
        if n + 1 < len(items):
            scores(n + 1)
        if ci == 0:
            m = [jnp.full((1, LANE), -jnp.inf, F32)] * (tq // LANE)
            l = [jnp.zeros((1, LANE), F32)] * (tq // LANE)
        alphas = []
        for st in range(tq // LANE):
            cs = slice(st * LANE, (st + 1) * LANE)
            m_new = jnp.maximum(m[st], jnp.max(s_scr[slot, 0:size, cs], axis=0, keepdims=True))
            alpha = jnp.exp2(m[st] - m_new)
            p = jnp.exp2(s_scr[slot, 0:size, cs] - m_new)
            l[st] = alpha * l[st] + jnp.sum(p, axis=0, keepdims=True)
            m[st] = m_new
            p_scr[slot, 0:size, cs] = p.astype(BF16)
            alphas.append(alpha)
        pv = _dot(vt_ref[0, hh * MLA_V:(hh + 1) * MLA_V, off:off + size], p_scr[slot, 0:size, :])
        if ci == 0:
            acc_scr[hh] = pv
        else:
            acc_scr[hh] = jnp.concatenate(alphas, axis=1) * acc_scr[hh] + pv
        if ci == len(chunks) - 1:
            o_ref[0, hh * MLA_V:(hh + 1) * MLA_V, :] = (acc_scr[hh] / jnp.concatenate(l, axis=1)).astype(BF16)


def _mla(bounded, qm, km, vmt, n_lat):
    B, H, S, _ = qm.shape
    kv, hp = 512, 2
    chunks = tuple((i * kv, kv) for i in range(n_lat // kv)) + ((n_lat, S - n_lat),)
    return pl.pallas_call(
        functools.partial(_mla_kernel, chunks=chunks),
        out_shape=jax.ShapeDtypeStruct((B, H * MLA_V, n_lat), BF16),
        grid=(B, H // hp, n_lat // MLA_TQ),
        in_specs=[pl.BlockSpec(memory_space=pltpu.SMEM),
                  pl.BlockSpec((1, hp, MLA_TQ, LANE), lambda b, h, i: (b, h, i, 0)),
                  pl.BlockSpec((1, hp, S, LANE), lambda b, h, i: (b, h, 0, 0)),
                  pl.BlockSpec((1, hp * MLA_V, S), lambda b, h, i: (b, h, 0))],
        out_specs=pl.BlockSpec((1, hp * MLA_V, MLA_TQ), lambda b, h, i: (b, h, i)),
        scratch_shapes=[pltpu.VMEM((2, kv, MLA_TQ), F32), pltpu.VMEM((2, kv, MLA_TQ), BF16),
                        pltpu.VMEM((hp, MLA_V, MLA_TQ), F32)],
        compiler_params=_params(3), name="mla_attn",
    )(bounded, qm, km, vmt)


def _natten_kernel(bounded_ref, q_ref, k_ref, vt_ref, bias_ref, o_ref, s_scr, p_scr, *, n_lat, n_ctx, rows):
    blk = pl.program_id(1)
    band = NA_BAND_ROWS * GRID_W
    nk = band + n_ctx
    tq = q_ref.shape[2]
    u0 = jnp.clip(blk * NA_Q_ROWS - NA_WIN_R // 2, 0, rows - NA_BAND_ROWS)
    koff = pl.multiple_of(u0 * GRID_W, 2 * LANE)
    lane = lax.broadcasted_iota(jnp.int32, (1, LANE), 1)
    half_mask = [(lane < NA_DH).astype(F32).astype(BF16), (lane >= NA_DH).astype(F32).astype(BF16)]

    def scores(h):
        qh = q_ref[0, h // 2] * half_mask[h % 2]
        return (_dot_nt(k_ref[0, h // 2, pl.ds(koff, band), :], qh) + bias_ref[h, 0],
                _dot_nt(k_ref[0, h // 2, n_lat:n_lat + n_ctx, :], qh))

    def finish(h, l):
        slot = h % 2
        hs = slice(h * NA_DH, (h + 1) * NA_DH)
        o = (_dot(vt_ref[0, hs, pl.ds(koff, band)], p_scr[slot, 0:band, :])
             + _dot(vt_ref[0, hs, n_lat:n_lat + n_ctx], p_scr[slot, band:nk, :]))
        o_ref[0, hs, :] = (o / l).astype(BF16)

    @pl.when(bounded_ref[0] != 0)
    def _():
        def probs(h):
            pb, pc = (jnp.exp2(s) for s in scores(h))
            p_scr[h % 2, 0:band, :] = pb.astype(BF16)
            p_scr[h % 2, band:nk, :] = pc.astype(BF16)
            return jnp.sum(pb, axis=0, keepdims=True) + jnp.sum(pc, axis=0, keepdims=True)

        l = probs(0)
        for h in range(NA_HEADS):
            l_next = probs(h + 1) if h + 1 < NA_HEADS else None
            finish(h, l)
            l = l_next

    @pl.when(bounded_ref[0] == 0)
    def _():
        def stage(h):
            s_scr[h % 2, 0:band, :], s_scr[h % 2, band:nk, :] = scores(h)

        stage(0)
        for h in range(NA_HEADS):
            slot = h % 2
            if h + 1 < NA_HEADS:
                stage(h + 1)
            ls = []
            for st in range(tq // LANE):
                cs = slice(st * LANE, (st + 1) * LANE)
                m = jnp.max(s_scr[slot, :, cs], axis=0, keepdims=True)
                p = jnp.exp2(s_scr[slot, :, cs] - m)
                ls.append(jnp.sum(p, axis=0, keepdims=True))
                p_scr[slot, :, cs] = p.astype(BF16)
            finish(h, jnp.concatenate(ls, axis=1))


def _natten(bounded, qn, kn, vnt, bias, n_lat):
    B, P, S, _ = qn.shape
    rows = n_lat // GRID_W
    nblk = rows // NA_Q_ROWS
    tq = NA_Q_ROWS * GRID_W
    band = NA_BAND_ROWS * GRID_W
    nk = band + S - n_lat
    btype = lambda i: jnp.where(i == 0, 0, jnp.where(i == nblk - 1, 2, 1))
    return pl.pallas_call(
        functools.partial(_natten_kernel, n_lat=n_lat, n_ctx=S - n_lat, rows=rows),
        out_shape=jax.ShapeDtypeStruct((B, NA_HEADS * NA_DH, n_lat), BF16),
        grid=(B, nblk),
        in_specs=[pl.BlockSpec(memory_space=pltpu.SMEM),
                  pl.BlockSpec((1, P, tq, LANE), lambda b, i: (b, 0, i, 0)),
                  pl.BlockSpec((1, P, S, LANE), lambda b, i: (b, 0, 0, 0)),
                  pl.BlockSpec((1, NA_HEADS * NA_DH, S), lambda b, i: (b, 0, 0)),
                  pl.BlockSpec((NA_HEADS, 1, band, tq), lambda b, i: (0, btype(i), 0, 0))],
        out_specs=pl.BlockSpec((1, NA_HEADS * NA_DH, tq), lambda b, i: (b, 0, i)),
        scratch_shapes=[pltpu.VMEM((2, nk, tq), F32), pltpu.VMEM((2, nk, tq), BF16)],
        compiler_params=_params(2), name="natten",
    )(bounded, qn, kn, vnt, bias)


def _na_bias_kernel(rpb_ref, o_ref, bc_ref, *, rows):
    l, h = pl.program_id(0), pl.program_id(1)
    tq = NA_Q_ROWS * GRID_W
    kc = lax.broadcasted_iota(jnp.int32, (GRID_W, tq), 0)
    ql = lax.broadcasted_iota(jnp.int32, (GRID_W, tq), 1)
    c = ql & (GRID_W - 1)
    qi = ql >> 6
    dc = kc - c + (NA_WIN_C - 1)
    cs = jnp.clip(c - NA_WIN_C // 2, 0, GRID_W - NA_WIN_C)
    col_ok = (kc >= cs) & (kc < cs + NA_WIN_C)
    n_dr, n_dc = 2 * NA_WIN_R - 1, 2 * NA_WIN_C - 1
    base = (l * NA_HEADS + h) * (n_dr * n_dc)
    for dr in range(n_dr):
        acc = jnp.zeros((GRID_W, tq), F32)
        for j in range(n_dc):
            acc = jnp.where(dc == j, rpb_ref[base + dr * n_dc + j], acc)
        bc_ref[dr] = jnp.where(col_ok, acc * LOG2E, NEG_INF)
    nblk = rows // NA_Q_ROWS
    for t, blk in enumerate((0, 1, nblk - 1)):
        r0 = blk * NA_Q_ROWS
        u0 = min(max(r0 - NA_WIN_R // 2, 0), rows - NA_BAND_ROWS)
        for j in range(NA_BAND_ROWS):
            tile = jnp.full((GRID_W, tq), NEG_INF, F32)
            for i in range(NA_Q_ROWS):
                r, kr = r0 + i, u0 + j
                rs = min(max(r - NA_WIN_R // 2, 0), rows - NA_WIN_R)
                if rs <= kr < rs + NA_WIN_R:
                    tile = jnp.where(qi == i, bc_ref[kr - r + NA_WIN_R - 1], tile)
            o_ref[0, 0, t, j * GRID_W:(j + 1) * GRID_W, :] = tile


def _na_bias(rpb, rows):
    L = rpb.shape[0]
    tq = NA_Q_ROWS * GRID_W
    band = NA_BAND_ROWS * GRID_W
    return pl.pallas_call(
        functools.partial(_na_bias_kernel, rows=rows),
        out_shape=jax.ShapeDtypeStruct((L, NA_HEADS, 3, band, tq), F32),
        grid=(L, NA_HEADS),
        in_specs=[pl.BlockSpec(memory_space=pltpu.SMEM)],
        out_specs=pl.BlockSpec((1, 1, 3, band, tq), lambda l, h: (l, h, 0, 0, 0)),
        scratch_shapes=[pltpu.VMEM((2 * NA_WIN_R - 1, GRID_W, tq), F32)],
        compiler_params=_params(2), name="na_bias",
    )(rpb.reshape(-1))


def _ctx_attn_kernel(qm_ref, km_ref, vmt_ref, qn_ref, kn_ref, vnt_ref, mo_ref, no_ref):
    lane = lax.broadcasted_iota(jnp.int32, (1, LANE), 1)
    for h in range(MLA_HEADS):
        hs = slice(h * MLA_V, (h + 1) * MLA_V)
        mo_ref[0, hs, :] = _softmax_pv_t([(km_ref[0, h], qm_ref[0, h], vmt_ref[0, hs, :], None)]).astype(BF16)
        qh = qn_ref[0, h // 2] * ((lane < NA_DH) if h % 2 == 0 else (lane >= NA_DH)).astype(F32).astype(BF16)
        no_ref[0, hs, :] = _softmax_pv_t([(kn_ref[0, h // 2], qh, vnt_ref[0, hs, :], None)]).astype(BF16)


def _ctx_attn(qm, km, vmt, qn, kn, vnt, n_lat):
    B, H, S, _ = qm.shape
    n_ctx = S - n_lat
    t = n_lat // n_ctx
    head = lambda b: (b, 0, t, 0)
    rows_t = lambda b: (b, 0, t)
    out = lambda b: (b, 0, 0)
    return pl.pallas_call(
        _ctx_attn_kernel,
        out_shape=[jax.ShapeDtypeStruct((B, H * MLA_V, n_ctx), BF16),
                   jax.ShapeDtypeStruct((B, NA_HEADS * NA_DH, n_ctx), BF16)],
        grid=(B,),
        in_specs=[pl.BlockSpec((1, H, n_ctx, LANE), head), pl.BlockSpec((1, H, n_ctx, LANE), head),
                  pl.BlockSpec((1, H * MLA_V, n_ctx), rows_t),
                  pl.BlockSpec((1, NA_HEADS // 2, n_ctx, LANE), head), pl.BlockSpec((1, NA_HEADS // 2, n_ctx, LANE), head),
                  pl.BlockSpec((1, NA_HEADS * NA_DH, n_ctx), rows_t)],
        out_specs=[pl.BlockSpec((1, H * MLA_V, n_ctx), out), pl.BlockSpec((1, NA_HEADS * NA_DH, n_ctx), out)],
        compiler_params=_params(1), name="ctx_attn",
    )(qm, km, vmt, qn, kn, vnt)


def _block_out_kernel(x_ref, xc_ref, mod_ref, g1_ref, g2_ref, hf_ref, hb_ref, o_ref, mo_ref, no_ref, moc_ref, noc_ref,
                      wg_ref, gout_ref, wml_ref, wmla_ref, wna_ref, wout_ref, w1_ref, w2_ref, out_ref, *, n_lat_tiles):
    is_ctx = pl.program_id(1) >= n_lat_tiles
    x = jnp.where(is_ctx, xc_ref[0], x_ref[0])
    m = mod_ref[0]
    hb = (_rms(x, g1_ref[...], D_MODEL) * (1.0 + m[1:2]) + m[0:1]).astype(BF16)
    gates = jax.nn.sigmoid(_dot(hb, wg_ref[...]))
    hs = hf_ref[0] + hb_ref[0]
    og = jax.nn.sigmoid(o_ref[0])
    gout = gout_ref[...]
    hn = jnp.concatenate(
        [_rms(hs[:, h * ML_DV:(h + 1) * ML_DV], gout[:, h * ML_DV:(h + 1) * ML_DV], ML_DV) for h in range(ML_HEADS)],
        axis=1)
    y_a = _dot((hn * og).astype(BF16), wml_ref[...])
    y_b = _dot_tn(jnp.where(is_ctx, moc_ref[0], mo_ref[0]), wmla_ref[...])
    y_c = _dot_tn(jnp.where(is_ctx, noc_ref[0], no_ref[0]), wna_ref[...])
    mg = gates[:, 0:D_MODEL] * y_a + gates[:, D_MODEL:2 * D_MODEL] * y_b + gates[:, 2 * D_MODEL:] * y_c
    x1 = x + m[2:3] * _dot(mg.astype(BF16), wout_ref[...])

    h2 = (_rms(x1, g2_ref[...], D_MODEL) * (1.0 + m[4:5]) + m[3:4]).astype(BF16)
    acc = jnp.zeros((TM, D_MODEL), F32)
    fc = 1024
    for c in range(D_FF // fc):
        u = jnp.maximum(_dot(h2, w1_ref[:, c * fc:(c + 1) * fc]), 0.0)
        acc = acc + _dot((u * u).astype(BF16), w2_ref[c * fc:(c + 1) * fc, :])
    out_ref[0] = x1 + m[5:6] * acc


def _block_out(x, xc, ctx_block, mods, g1, g2, hf, hb, o, mla_o, na_o, mla_oc, na_oc, w, nt, n_lat_tiles):
    B = x.shape[0]
    tok = lambda b, t: (b, t, 0)
    lat_t = lambda b, t: (b, 0, jnp.minimum(t, n_lat_tiles - 1))
    ctx_t = lambda b, t: (b, 0, 0)
    in_specs = _stream_specs(ctx_block, n_lat_tiles)
    in_specs += [pl.BlockSpec((1, 8, D_MODEL), lambda b, t: (jnp.where(t >= n_lat_tiles, B, b), 0, 0)),
                 _const_spec(g1.shape), _const_spec(g2.shape),
                 pl.BlockSpec((1, TM, 512), tok), pl.BlockSpec((1, TM, 512), tok), pl.BlockSpec((1, TM, 512), tok),
                 pl.BlockSpec((1, 512, TM), lat_t), pl.BlockSpec((1, 512, TM), lat_t),
                 pl.BlockSpec((1, 512, TM), ctx_t), pl.BlockSpec((1, 512, TM), ctx_t)]
    names = ("wg", "gout", "wml", "wmla", "wna", "wout", "w1", "w2")
    in_specs += [_const_spec(w[k].shape) for k in names]
    return pl.pallas_call(
        functools.partial(_block_out_kernel, n_lat_tiles=n_lat_tiles),
        out_shape=jax.ShapeDtypeStruct((B, nt * TM, D_MODEL), F32),
        grid=(B, nt), in_specs=in_specs, out_specs=pl.BlockSpec((1, TM, D_MODEL), tok),
        compiler_params=_params(2), name="block_out",
    )(x, xc, mods, g1, g2, hf, hb, o, mla_o, na_o, mla_oc, na_oc, *[w[k] for k in names])


def _rope_tables(n_lat, n_ctx):
    nf = MLA_ROPE // 4
    inv = jnp.power(ROPE_THETA, -jnp.arange(nf, dtype=F32) / nf)
    pos = jnp.arange(n_lat)
    ang_r = (pos // GRID_W).astype(F32)[:, None] * inv
    ang_c = (pos % GRID_W).astype(F32)[:, None] * inv
    one = lambda n: jnp.ones((n_lat, n), F32)
    zero = lambda n: jnp.zeros((n_lat, n), F32)
    cr, sr, cc, sc = jnp.cos(ang_r), jnp.sin(ang_r), jnp.cos(ang_c), jnp.sin(ang_c)
    cos = jnp.concatenate([one(MLA_NOPE), cr, cr, cc, cc, one(LANE - MLA_DQK)], axis=1)
    sin = jnp.concatenate([zero(MLA_NOPE), -sr, sr, -sc, sc, zero(LANE - MLA_DQK)], axis=1)
    ident = (jnp.ones((n_ctx, LANE), F32), jnp.zeros((n_ctx, LANE), F32))
    return tuple(jnp.concatenate([t, i], axis=0) for t, i in zip((cos, sin), ident))


def _rope_partner(a):
    half = MLA_ROPE // 4
    lane = np.arange(LANE)
    r = lane - MLA_NOPE
    is_rope = (r >= 0) & (r < MLA_ROPE)
    src = np.where((r // half) % 2 == 0, lane + half, lane - half)
    return jnp.where(is_rope, a[..., np.where(is_rope, src, lane)], 0)


def _layer_weights(l, w_in, ml_i_bias, ml_f_bias, ml_g_out, ml_w_o, mla_g_cq, mla_w_uq, mla_g_ckv, mla_w_ukv,
                   mla_g_q, mla_g_k, mla_w_o, na_g_q, na_g_k, na_w_o, w_out, w_ff1, w_ff2):
    offs = np.cumsum(IN_SIZES)[:-1].tolist()
    mlq, mlk, mlv, mlo, mlg, dq, dkv, kr, naq, nak, nav, gates = jnp.split(w_in[l], offs, axis=-1)
    padc = lambda a, n: jnp.pad(a, ((0, 0), (0, n - a.shape[1])))
    kr_slab = jnp.pad(kr, ((0, 0), (MLA_NOPE, LANE - MLA_DQK)))
    w_uq = jnp.pad(mla_w_uq[l].reshape(MLA_Q_RANK, MLA_HEADS, MLA_DQK), ((0, 0), (0, 0), (0, LANE - MLA_DQK)))
    w_ukv = mla_w_ukv[l].reshape(MLA_KV_RANK, MLA_HEADS, MLA_NOPE + MLA_V)
    w_uk = jnp.pad(w_ukv[..., :MLA_NOPE], ((0, 0), (0, 0), (0, LANE - MLA_NOPE)))
    w_uv = w_ukv[..., MLA_NOPE:].reshape(MLA_KV_RANK, MLA_HEADS * MLA_V)
    row = lambda a: a.reshape(1, -1).astype(F32)
    bias16 = jnp.stack([ml_i_bias[l, 0], ml_f_bias[l, 0], ml_i_bias[l, 1], ml_f_bias[l, 1]]).reshape(-1).astype(F32)
    return {
        "wa": jnp.concatenate([mlq * (ML_DQK ** -0.5), mlv, mlo, padc(mlg, LANE)], axis=1).astype(BF16),
        "wat": jnp.concatenate([mlk.T, mlg.T], axis=0).astype(BF16),
        "wb": jnp.concatenate([dq, dkv, kr_slab, _rope_partner(kr_slab)], axis=1).astype(BF16),
        "wc": jnp.concatenate([naq, nak], axis=1).astype(BF16),
        "wct": nav.T.astype(BF16),
        "wuq": w_uq.reshape(MLA_Q_RANK, MLA_HEADS * LANE).astype(BF16),
        "wuqs": _rope_partner(w_uq).reshape(MLA_Q_RANK, MLA_HEADS * LANE).astype(BF16),
        "wuk": w_uk.reshape(MLA_KV_RANK, MLA_HEADS * LANE).astype(BF16),
        "wuvt": w_uv.T.astype(BF16),
        "gcq": row(mla_g_cq[l]), "gckv": row(mla_g_ckv[l]),
        "gq": padc(row(mla_g_q[l]) * (MLA_DQK ** -0.5 * LOG2E), LANE), "gk": padc(row(mla_g_k[l]), LANE),
        "ngq": row(jnp.tile(na_g_q[l], 2)) * (NA_DH ** -0.5 * LOG2E), "ngk": row(jnp.tile(na_g_k[l], 2)),
        "ng": jnp.concatenate([row(jnp.tile(na_g_q[l], NA_HEADS)) * (NA_DH ** -0.5 * LOG2E),
                               row(jnp.tile(na_g_k[l], NA_HEADS))], axis=1),
        "brow": padc(bias16.reshape(1, 16), LANE), "bcol": bias16.reshape(16, 1),
        "wg": gates.astype(BF16), "gout": row(ml_g_out[l]),
        "wml": ml_w_o[l].astype(BF16), "wmla": mla_w_o[l].astype(BF16), "wna": na_w_o[l].astype(BF16),
        "wout": w_out[l].astype(BF16), "w1": w_ff1[l].astype(BF16), "w2": w_ff2[l].astype(BF16),
    }


def kernel(x, c, ctx, c_ctx, w_mod, b_mod, g_norm1, g_norm2, w_in, ml_i_bias, ml_f_bias, ml_g_out, ml_w_o, mla_g_cq, mla_w_uq, mla_g_ckv, mla_w_ukv, mla_g_q, mla_g_k, mla_w_o, na_g_q, na_g_k, na_rpb, na_w_o, w_out, w_ff1, w_ff2):
    B, T, D = x.shape
    C = ctx.shape[1]
    depth = w_in.shape[0]
    assert D == D_MODEL and C == TM and T % MLA_TQ == 0 and B < 16
    rows = T // GRID_W
    assert rows % NA_Q_ROWS == 0 and rows >= NA_BAND_ROWS
    n_lat_tiles = T // TM

    cc = jnp.zeros((16, D), F32).at[:B].set(c).at[B].set(c_ctx)
    mod = _modulation(cc, w_mod, b_mod).reshape(depth, 16, 6, D)
    mod = jnp.pad(mod, ((0, 0), (0, 0), (0, 2), (0, 0)))
    tabs = _rope_tables(T, C)
    na_bias = _na_bias(na_rpb.astype(F32), rows)
    stream = (x, ctx, 0)

    for l in range(depth):
        last = l == depth - 1
        w = _layer_weights(l, w_in, ml_i_bias, ml_f_bias, ml_g_out, ml_w_o, mla_g_cq, mla_w_uq, mla_g_ckv,
                           mla_w_ukv, mla_g_q, mla_g_k, mla_w_o, na_g_q, na_g_k, na_w_o, w_out, w_ff1, w_ff2)
        g1 = g_norm1[l].reshape(1, D).astype(F32)
        g2 = g_norm2[l].reshape(1, D).astype(F32)
        cos, sin = tabs
        gain_tabs = (cos * w["gq"], sin * _rope_partner(w["gq"]), cos * w["gk"], sin * _rope_partner(w["gk"]))
        q, kt, v, o, gcol, grow, qm, km, vmt, qn, kn, vnt = _inproj(*stream, mod[l], g1, w, gain_tabs, n_lat_tiles)
        hf, hb = _mlstm(q, kt, v, gcol, grow, w["brow"], w["bcol"], n_lat_tiles)
        mla_bound = MLA_DQK * jnp.max(jnp.abs(w["gq"])) * jnp.max(jnp.abs(w["gk"]))
        na_bound = (NA_DH * jnp.max(jnp.abs(w["ngq"])) * jnp.max(jnp.abs(w["ngk"]))
                    + LOG2E * jnp.max(jnp.abs(na_rpb[l].astype(F32))))
        mla_o = _mla((mla_bound <= SCORE_BOUND_LIMIT).astype(jnp.int32).reshape(1), qm, km, vmt, T)
        na_o = _natten((na_bound <= SCORE_BOUND_LIMIT).astype(jnp.int32).reshape(1), qn, kn, vnt, na_bias[l], T)
        mla_oc, na_oc = (mla_o, na_o) if last else _ctx_attn(qm, km, vmt, qn, kn, vnt, T)
        nt = n_lat_tiles if last else n_lat_tiles + 1
        xs = _block_out(*stream, mod[l], g1, g2, hf, hb, o, mla_o, na_o, mla_oc, na_oc, w, nt, n_lat_tiles)
        stream = (xs, xs, n_lat_tiles)
    return xs
```

```python
import functools

import numpy as np
import jax
import jax.numpy as jnp
from jax import lax
from jax.experimental import pallas as pl
from jax.experimental.pallas import tpu as pltpu

F32 = jnp.float32
BF16 = jnp.bfloat16

D_MODEL = 1024
GRID_W = 64
ML_HEADS, ML_DQK, ML_DV = 4, 64, 128
ML_CHUNK = 128
ML_SAMPLES = 1
MLA_HEADS, MLA_Q_RANK, MLA_KV_RANK, MLA_NOPE, MLA_ROPE, MLA_V = 8, 384, 256, 64, 32, 64
MLA_DQK = MLA_NOPE + MLA_ROPE
NA_HEADS, NA_DH, NA_WIN_R, NA_WIN_C = 8, 64, 8, 16
NA_Q_ROWS = 4
NA_BAND_ROWS = NA_Q_ROWS + NA_WIN_R
D_FF = 4 * D_MODEL
ROPE_THETA = 10000.0
EPS = 1e-6
LOG2E = 1.4426950408889634
SCORE_BOUND_LIMIT = 60.0
NEG_INF = -1e30
IN_SIZES = (256, 256, 512, 512, 16, MLA_Q_RANK, MLA_KV_RANK, MLA_ROPE, 512, 512, 512, 3 * D_MODEL)
LANE = 128
TM = 256
MLA_TQ = 1024
VMEM_CAP_MIB = 56


def _dot(a, b):
    return jnp.dot(a, b, preferred_element_type=F32)


def _dot_nt(a, b):
    return lax.dot_general(a, b, (((1,), (1,)), ((), ())), preferred_element_type=F32)


def _dot_tn(a, b):
    return lax.dot_general(a, b, (((0,), (0,)), ((), ())), preferred_element_type=F32)


def _rms(x, g, n):
    ms = jnp.sum(x * x, axis=-1, keepdims=True) * (1.0 / n)
    return x * lax.rsqrt(ms + EPS) * g


def _split3(x):
    hi = x.astype(BF16)
    r = x - hi.astype(F32)
    mid = r.astype(BF16)
    lo = (r - mid.astype(F32)).astype(BF16)
    return hi, mid, lo


def _params(n_axes, vmem_mib=VMEM_CAP_MIB):
    assert vmem_mib <= VMEM_CAP_MIB
    return pltpu.CompilerParams(dimension_semantics=("arbitrary",) * n_axes, vmem_limit_bytes=vmem_mib << 20)


def _const_spec(shape):
    nd = len(shape)
    return pl.BlockSpec(shape, lambda *_: (0,) * nd, pipeline_mode=pl.Buffered(1))


def _mod_kernel(c_ref, w_ref, b_ref, o_ref):
    c = c_ref[...]
    s = c * jax.nn.sigmoid(c)
    o_ref[0] = _dot(s, w_ref[0]) + b_ref[0]


def _modulation(cc, w_mod, b_mod):
    L = w_mod.shape[0]
    tn = 1024
    return pl.pallas_call(
        _mod_kernel,
        out_shape=jax.ShapeDtypeStruct((L, 16, 6 * D_MODEL), F32),
        grid=(L, 6 * D_MODEL // tn),
        in_specs=[pl.BlockSpec((16, D_MODEL), lambda l, n: (0, 0)),
                  pl.BlockSpec((1, D_MODEL, tn), lambda l, n: (l, 0, n)),
                  pl.BlockSpec((1, 1, tn), lambda l, n: (l, 0, n))],
        out_specs=pl.BlockSpec((1, 16, tn), lambda l, n: (l, 0, n)),
        compiler_params=_params(2),
        name="modulation",
    )(cc, w_mod, b_mod.reshape(L, 1, 6 * D_MODEL))


def _inproj_kernel(x_ref, xc_ref, mod_ref, g1_ref, wa_ref, wat_ref, wb_ref, wc_ref, wct_ref, wuq_ref, wuqs_ref, wuk_ref,
                   wuvt_ref, gcq_ref, gckv_ref, ng_ref, cgq_ref, sgq_ref, cgk_ref, sgk_ref,
                   q_ref, kt_ref, v_ref, o_ref, gcol_ref, grow_ref, qm_ref, km_ref, vmt_ref, qn_ref, kn_ref, vnt_ref,
                   *, n_lat_tiles):
    x = jnp.where(pl.program_id(1) >= n_lat_tiles, xc_ref[0], x_ref[0])
    m = mod_ref[0]
    hb = (_rms(x, g1_ref[...], D_MODEL) * (1.0 + m[1:2]) + m[0:1]).astype(BF16)

    a = _dot(hb, wa_ref[...])
    q_ref[0] = a[:, 0:256].astype(BF16)
    v_ref[0] = a[:, 256:768].astype(BF16)
    o_ref[0] = a[:, 768:1280]
    gcol_ref[0] = a[:, 1280:1408]
    at = _dot_nt(wat_ref[...], hb)
    kt_ref[0] = at[0:256].astype(BF16)
    grow_ref[0] = at[256:272]

    bs = _dot(hb, wb_ref[...])
    dqn = _rms(bs[:, 0:MLA_Q_RANK], gcq_ref[...], MLA_Q_RANK).astype(BF16)
    dkvn = _rms(bs[:, MLA_Q_RANK:MLA_Q_RANK + MLA_KV_RANK], gckv_ref[...], MLA_KV_RANK).astype(BF16)
    kr = bs[:, MLA_Q_RANK + MLA_KV_RANK:MLA_Q_RANK + MLA_KV_RANK + LANE]
    kr_sw = bs[:, MLA_Q_RANK + MLA_KV_RANK + LANE:]
    qh_all = _dot(dqn, wuq_ref[...])
    qsw_all = _dot(dqn, wuqs_ref[...])
    kh_all = _dot(dkvn, wuk_ref[...]) + jnp.concatenate([kr] * MLA_HEADS, axis=1)
    cgq, sgq, cgk, sgk = cgq_ref[...], sgq_ref[...], cgk_ref[...], sgk_ref[...]
    inv_rms = lambda y: lax.rsqrt(jnp.sum(y * y, axis=-1, keepdims=True) * (1.0 / MLA_DQK) + EPS)
    for h in range(MLA_HEADS):
        sl = slice(h * LANE, (h + 1) * LANE)
        qh, kh = qh_all[:, sl], kh_all[:, sl]
        qm_ref[0, h] = ((qh * cgq + qsw_all[:, sl] * sgq) * inv_rms(qh)).astype(BF16)
        km_ref[0, h] = ((kh * cgk + kr_sw * sgk) * inv_rms(kh)).astype(BF16)
    vmt_ref[0] = _dot_nt(wuvt_ref[...], dkvn).astype(BF16)

    c = _dot(hb, wc_ref[...])
    lo = lax.broadcasted_iota(jnp.int32, (1, LANE), 1) < NA_DH
    ng = ng_ref[...]
    for p in range(NA_HEADS // 2):
        for base, out_ref in ((0, qn_ref), (NA_HEADS * NA_DH, kn_ref)):
            sl = slice(base + p * LANE, base + (p + 1) * LANE)
            s = c[:, sl]
            sq = s * s
            s_lo = jnp.sum(jnp.where(lo, sq, 0.0), axis=-1, keepdims=True)
            s_hi = jnp.sum(jnp.where(lo, 0.0, sq), axis=-1, keepdims=True)
            ms = jnp.where(lo, s_lo, s_hi) * (1.0 / NA_DH)
            out_ref[0, p] = (s * lax.rsqrt(ms + EPS) * ng[:, sl]).astype(BF16)
    vnt_ref[0] = _dot_nt(wct_ref[...], hb).astype(BF16)


def _stream_specs(ctx_block, n_lat_tiles):
    return [pl.BlockSpec((1, TM, D_MODEL), lambda b, t: (b, jnp.minimum(t, n_lat_tiles - 1), 0)),
            pl.BlockSpec((1, TM, D_MODEL), lambda b, t: (b, ctx_block, 0))]


def _inproj(x, xc, ctx_block, mods, g1, w, tabs, n_lat_tiles):
    B = x.shape[0]
    nt = n_lat_tiles + 1
    S = nt * TM
    tok = lambda b, t: (b, t, 0)
    tokT = lambda b, t: (b, 0, t)
    head = lambda b, t: (b, 0, t, 0)
    in_specs = _stream_specs(ctx_block, n_lat_tiles)
    in_specs += [pl.BlockSpec((1, 8, D_MODEL), lambda b, t: (jnp.where(t >= n_lat_tiles, B, b), 0, 0)),
                 _const_spec(g1.shape)]
    consts = ("wa", "wat", "wb", "wc", "wct", "wuq", "wuqs", "wuk", "wuvt", "gcq", "gckv", "ng")
    in_specs += [_const_spec(w[k].shape) for k in consts]
    in_specs += [pl.BlockSpec((TM, LANE), lambda b, t: (t, 0))] * 4
    out_shape = [jax.ShapeDtypeStruct((B, S, 256), BF16),
                 jax.ShapeDtypeStruct((B, 256, S), BF16),
                 jax.ShapeDtypeStruct((B, S, 512), BF16),
                 jax.ShapeDtypeStruct((B, S, 512), F32),
                 jax.ShapeDtypeStruct((B, S, LANE), F32),
                 jax.ShapeDtypeStruct((B, 16, S), F32),
                 jax.ShapeDtypeStruct((B, MLA_HEADS, S, LANE), BF16),
                 jax.ShapeDtypeStruct((B, MLA_HEADS, S, LANE), BF16),
                 jax.ShapeDtypeStruct((B, MLA_HEADS * MLA_V, S), BF16),
                 jax.ShapeDtypeStruct((B, NA_HEADS // 2, S, LANE), BF16),
                 jax.ShapeDtypeStruct((B, NA_HEADS // 2, S, LANE), BF16),
                 jax.ShapeDtypeStruct((B, NA_HEADS * NA_DH, S), BF16)]
    out_specs = [pl.BlockSpec((1, TM, 256), tok), pl.BlockSpec((1, 256, TM), tokT),
                 pl.BlockSpec((1, TM, 512), tok), pl.BlockSpec((1, TM, 512), tok),
                 pl.BlockSpec((1, TM, LANE), tok), pl.BlockSpec((1, 16, TM), tokT),
                 pl.BlockSpec((1, MLA_HEADS, TM, LANE), head), pl.BlockSpec((1, MLA_HEADS, TM, LANE), head),
                 pl.BlockSpec((1, MLA_HEADS * MLA_V, TM), tokT),
                 pl.BlockSpec((1, NA_HEADS // 2, TM, LANE), head), pl.BlockSpec((1, NA_HEADS // 2, TM, LANE), head),
                 pl.BlockSpec((1, NA_HEADS * NA_DH, TM), tokT)]
    return pl.pallas_call(
        functools.partial(_inproj_kernel, n_lat_tiles=n_lat_tiles),
        out_shape=out_shape, grid=(B, nt), in_specs=in_specs, out_specs=out_specs,
        compiler_params=_params(2, 40), name="inproj",
    )(x, xc, mods, g1, *[w[k] for k in consts], *tabs)


def _log_sigmoid(x):
    return jnp.minimum(x, 0.0) - jnp.log1p(jnp.exp(-jnp.abs(x)))


def _mlstm_kernel(qf_ref, ktf_ref, vf_ref, gcf_ref, grf_ref, qb_ref, ktb_ref, vb_ref, gcb_ref, grb_ref,
                  brow_ref, bcol_ref, hf_ref, hb_ref, c_ref, m_ref):
    @pl.when(pl.program_id(1) == 0)
    def _():
        c_ref[...] = jnp.zeros_like(c_ref)
        m_ref[...] = jnp.zeros_like(m_ref)

    dirs = ((qf_ref, ktf_ref, vf_ref, gcf_ref, grf_ref, hf_ref), (qb_ref, ktb_ref, vb_ref, gcb_ref, grb_ref, hb_ref))
    samples = [_mlstm_sample(bb, dirs, brow_ref, bcol_ref, c_ref, m_ref) for bb in range(qf_ref.shape[0])]
    for _ in zip(*samples):
        pass


def _mlstm_sample(bb, dirs, brow_ref, bcol_ref, c_ref, m_ref):
    L = ML_CHUNK
    row = lax.broadcasted_iota(jnp.int32, (L, L), 0)
    col = lax.broadcasted_iota(jnp.int32, (L, L), 1)
    lane = lax.broadcasted_iota(jnp.int32, (1, LANE), 1)
    isf_row = (lane & ML_HEADS) != 0
    sub = lax.broadcasted_iota(jnp.int32, (16, 1), 0)
    isf_col = (sub & ML_HEADS) != 0
    half_mask = [(lane < ML_DQK).astype(F32).astype(BF16), (lane >= ML_DQK).astype(F32).astype(BF16)]
    e0 = jnp.broadcast_to((lane == 0).astype(F32).astype(BF16), (L, LANE))
    for d, (q_ref, kt_ref, v_ref, gc_ref, gr_ref, h_ref) in enumerate(dirs):
        causal = (col <= row) if d == 0 else (col >= row)
        tri = causal.astype(F32).astype(BF16)
        tri_t = ((row <= col) if d == 0 else (row >= col)).astype(F32).astype(BF16)
        local = {}
        for ch in range(TM // L):
            rs = slice(ch * L, (ch + 1) * L)
            pre_c = gc_ref[bb, rs, :] + brow_ref[...]
            val_c = jnp.where(isf_row, _log_sigmoid(pre_c), pre_c)
            pre_r = gr_ref[bb, :, rs] + bcol_ref[...]
            val_r = jnp.where(isf_col, _log_sigmoid(pre_r), pre_r)
            cum_c = sum(_dot(tri, t) for t in _split3(val_c))
            cum_r = sum(_dot(t, tri_t) for t in _split3(val_r))
            tot_r = jnp.sum(val_r, axis=1, keepdims=True)
            for h in range(ML_HEADS):
                ji, jf = (2 * d) * ML_HEADS + h, (2 * d + 1) * ML_HEADS + h
                p, half = h // 2, h % 2
                bcum_c = cum_c[:, jf:jf + 1]
                bcum_r = cum_r[jf:jf + 1, :]
                i_r = val_r[ji:ji + 1, :]
                btot = tot_r[jf:jf + 1, :]
                w_end = btot - bcum_r + i_r
                m_w = jnp.max(w_end, axis=1, keepdims=True)
                dm = jnp.where(causal, bcum_c - bcum_r + i_r, -jnp.inf)
                m_loc = jnp.max(dm, axis=1, keepdims=True)
                q_h = q_ref[bb, rs, p * LANE:(p + 1) * LANE] * half_mask[half]
                kt_pair = kt_ref[bb, p * LANE:(p + 1) * LANE, rs]
                kt_h = kt_ref[bb, h * ML_DQK:(h + 1) * ML_DQK, rs]
                v_ext = jnp.concatenate([v_ref[bb, rs, h * ML_DV:(h + 1) * ML_DV], e0], axis=1)
                s = _dot(q_h, kt_pair) * jnp.exp(dm - m_loc)
                sv = _dot(s.astype(BF16), v_ext)
                kw = (kt_h.astype(F32) * jnp.exp(w_end - m_w)).astype(BF16)
                dc = _dot(kw, v_ext)
                local[ch, h] = (q_h, bcum_c, btot, m_w, m_loc, sv, dc)
                yield
        for cc in range(TM // L):
            ch = cc if d == 0 else TM // L - 1 - cc
            rs = slice(ch * L, (ch + 1) * L)
            for h in range(ML_HEADS):
                p, half = h // 2, h % 2
                idx = d * ML_HEADS + h
                q_h, bcum_c, btot, m_w, m_loc, sv, dc = local[ch, h]
                m_old = m_ref[bb, idx, 0:1, 0:1]
                c_pair = c_ref[bb, d, p]
                inter = bcum_c + m_old
                m_t = jnp.maximum(inter, m_loc)
                qc = _dot(q_h, c_pair.astype(BF16))
                a_inter, a_intra = jnp.exp(inter - m_t), jnp.exp(m_loc - m_t)
                den = a_inter * qc[:, ML_DV:ML_DV + 1] + a_intra * sv[:, ML_DV:ML_DV + 1]
                r = 1.0 / jnp.maximum(jnp.abs(den), jnp.exp(-m_t))
                h_ref[bb, rs, h * ML_DV:(h + 1) * ML_DV] = (a_inter * r) * qc[:, 0:ML_DV] + (a_intra * r) * sv[:, 0:ML_DV]
                m_new = jnp.maximum(btot + m_old, m_w)
                hs = slice(half * ML_DQK, (half + 1) * ML_DQK)
                c_ref[bb, d, p, hs, :] = jnp.exp(btot + m_old - m_new) * c_pair[hs, :] + jnp.exp(m_w - m_new) * dc
                m_ref[bb, idx] = jnp.broadcast_to(m_new, (8, LANE))
                yield


def _mlstm(q, kt, v, gcol, grow, brow, bcol, n_lat_tiles):
    B, S, _ = q.shape
    nblk = S // TM
    fwd = lambda j: jnp.where(j == 0, n_lat_tiles, j - 1)
    bwd = lambda j: jnp.where(j == 0, n_lat_tiles, n_lat_tiles - j)

    nb = ML_SAMPLES if B % ML_SAMPLES == 0 else 1

    def specs(order):
        return [pl.BlockSpec((nb, TM, 256), lambda b, j: (b, order(j), 0)),
                pl.BlockSpec((nb, 256, TM), lambda b, j: (b, 0, order(j))),
                pl.BlockSpec((nb, TM, 512), lambda b, j: (b, order(j), 0)),
                pl.BlockSpec((nb, TM, LANE), lambda b, j: (b, order(j), 0)),
                pl.BlockSpec((nb, 16, TM), lambda b, j: (b, 0, order(j)))]

    return pl.pallas_call(
        _mlstm_kernel,
        out_shape=[jax.ShapeDtypeStruct((B, S, 512), F32)] * 2,
        grid=(B // nb, nblk),
        in_specs=specs(fwd) + specs(bwd) + [_const_spec(brow.shape), _const_spec(bcol.shape)],
        out_specs=[pl.BlockSpec((nb, TM, 512), lambda b, j: (b, fwd(j), 0)),
                   pl.BlockSpec((nb, TM, 512), lambda b, j: (b, bwd(j), 0))],
        scratch_shapes=[pltpu.VMEM((nb, 2, ML_HEADS // 2, 2 * ML_DQK, 2 * ML_DV), F32),
                        pltpu.VMEM((nb, 2 * ML_HEADS, 8, LANE), F32)],
        compiler_params=_params(2), name="mlstm",
    )(q, kt, v, gcol, grow, q, kt, v, gcol, grow, brow, bcol)


def _softmax_pv_t(parts):
    ss = []
    for k, q, _, bias in parts:
        s = _dot_nt(k, q)
        ss.append(s if bias is None else s + bias)
    m = functools.reduce(jnp.maximum, [jnp.max(s, axis=0, keepdims=True) for s in ss])
    ps = [jnp.exp2(s - m) for s in ss]
    l = sum(jnp.sum(p, axis=0, keepdims=True) for p in ps)
    o = sum(_dot(vt, p.astype(BF16)) for (_, _, vt, _), p in zip(parts, ps))
    return o / l


def _mla_kernel(bounded_ref, q_ref, k_ref, vt_ref, o_ref, s_scr, p_scr, acc_scr, *, chunks):
    @pl.when(bounded_ref[0] != 0)
    def _():
        _mla_bounded(q_ref, k_ref, vt_ref, o_ref, p_scr, acc_scr, chunks)

    @pl.when(bounded_ref[0] == 0)
    def _():
        _mla_online(q_ref, k_ref, vt_ref, o_ref, s_scr, p_scr, acc_scr, chunks)


def _mla_bounded(q_ref, k_ref, vt_ref, o_ref, p_scr, acc_scr, chunks):
    n_heads = q_ref.shape[1]
    items = [(hh, ci) for hh in range(n_heads) for ci in range(len(chunks))]

    def probs(n):
        hh, ci = items[n]
        off, size = chunks[ci]
        p = jnp.exp2(_dot_nt(k_ref[0, hh, off:off + size, :], q_ref[0, hh]))
        p_scr[n % 2, 0:size, :] = p.astype(BF16)
        return jnp.sum(p, axis=0, keepdims=True)

    lsum = probs(0)
    l = None
    for n, (hh, ci) in enumerate(items):
        off, size = chunks[ci]
        l = lsum if ci == 0 else l + lsum
        if n + 1 < len(items):
            lsum = probs(n + 1)
        pv = _dot(vt_ref[0, hh * MLA_V:(hh + 1) * MLA_V, off:off + size], p_scr[n % 2, 0:size, :])
        if ci == 0:
            acc_scr[hh] = pv
        else:
            acc_scr[hh] += pv
        if ci == len(chunks) - 1:
            o_ref[0, hh * MLA_V:(hh + 1) * MLA_V, :] = (acc_scr[hh] / l).astype(BF16)


def _mla_online(q_ref, k_ref, vt_ref, o_ref, s_scr, p_scr, acc_scr, chunks):
    n_heads, tq = q_ref.shape[1], q_ref.shape[2]
    items = [(hh, ci) for hh in range(n_heads) for ci in range(len(chunks))]

    def scores(n):
        hh, ci = items[n]
        off, size = chunks[ci]
        s_scr[n % 2, 0:size, :] = _dot_nt(k_ref[0, hh, off:off + size, :], q_ref[0, hh])

    scores(0)
    m = l = None
    for n, (hh, ci) in enumerate(items):
        off, size = chunks[ci]
        slot = n % 2
        if n + 1 < len(items):
            scores(n + 1)
        if ci == 0:
            m = [jnp.full((1, LANE), -jnp.inf, F32)] * (tq // LANE)
            l = [jnp.zeros((1, LANE), F32)] * (tq // LANE)
        alphas = []
        for st in range(tq // LANE):
            cs = slice(st * LANE, (st + 1) * LANE)
            m_new = jnp.maximum(m[st], jnp.max(s_scr[slot, 0:size, cs], axis=0, keepdims=True))
            alpha = jnp.exp2(m[st] - m_new)
            p = jnp.exp2(s_scr[slot, 0:size, cs] - m_new)
            l[st] = alpha * l[st] + jnp.sum(p, axis=0, keepdims=True)
            m[st] = m_new
            p_scr[slot, 0:size, cs] = p.astype(BF16)
            alphas.append(alpha)
        pv = _dot(vt_ref[0, hh * MLA_V:(hh + 1) * MLA_V, off:off + size], p_scr[slot, 0:size, :])
        if ci == 0:
            acc_scr[hh] = pv
        else:
            acc_scr[hh] = jnp.concatenate(alphas, axis=1) * acc_scr[hh] + pv
        if ci == len(chunks) - 1:
            o_ref[0, hh * MLA_V:(hh + 1) * MLA_V, :] = (acc_scr[hh] / jnp.concatenate(l, axis=1)).astype(BF16)


def _mla(bounded, qm, km, vmt, n_lat):
    B, H, S, _ = qm.shape
    kv, hp = 512, 2
    chunks = tuple((i * kv, kv) for i in range(n_lat // kv)) + ((n_lat, S - n_lat),)
    return pl.pallas_call(
        functools.partial(_mla_kernel, chunks=chunks),
        out_shape=jax.ShapeDtypeStruct((B, H * MLA_V, n_lat), BF16),
        grid=(B, H // hp, n_lat // MLA_TQ),
        in_specs=[pl.BlockSpec(memory_space=pltpu.SMEM),
                  pl.BlockSpec((1, hp, MLA_TQ, LANE), lambda b, h, i: (b, h, i, 0)),
                  pl.BlockSpec((1, hp, S, LANE), lambda b, h, i: (b, h, 0, 0)),
                  pl.BlockSpec((1, hp * MLA_V, S), lambda b, h, i: (b, h, 0))],
        out_specs=pl.BlockSpec((1, hp * MLA_V, MLA_TQ), lambda b, h, i: (b, h, i)),
        scratch_shapes=[pltpu.VMEM((2, kv, MLA_TQ), F32), pltpu.VMEM((2, kv, MLA_TQ), BF16),
                        pltpu.VMEM((hp, MLA_V, MLA_TQ), F32)],
        compiler_params=_params(3), name="mla_attn",
    )(bounded, qm, km, vmt)


def _natten_kernel(bounded_ref, q_ref, k_ref, vt_ref, bias_ref, o_ref, s_scr, p_scr, *, n_lat, n_ctx, rows):
    blk = pl.program_id(1)
    band = NA_BAND_ROWS * GRID_W
    nk = band + n_ctx
    tq = q_ref.shape[2]
    u0 = jnp.clip(blk * NA_Q_ROWS - NA_WIN_R // 2, 0, rows - NA_BAND_ROWS)
    koff = pl.multiple_of(u0 * GRID_W, 2 * LANE)
    lane = lax.broadcasted_iota(jnp.int32, (1, LANE), 1)
    half_mask = [(lane < NA_DH).astype(F32).astype(BF16), (lane >= NA_DH).astype(F32).astype(BF16)]

    def scores(h):
        qh = q_ref[0, h // 2] * half_mask[h % 2]
        return (_dot_nt(k_ref[0, h // 2, pl.ds(koff, band), :], qh) + bias_ref[h, 0],
                _dot_nt(k_ref[0, h // 2, n_lat:n_lat + n_ctx, :], qh))

    def finish(h, l):
        slot = h % 2
        hs = slice(h * NA_DH, (h + 1) * NA_DH)
        o = (_dot(vt_ref[0, hs, pl.ds(koff, band)], p_scr[slot, 0:band, :])
             + _dot(vt_ref[0, hs, n_lat:n_lat + n_ctx], p_scr[slot, band:nk, :]))
        o_ref[0, hs, :] = (o / l).astype(BF16)

    @pl.when(bounded_ref[0] != 0)
    def _():
        def probs(h):
            pb, pc = (jnp.exp2(s) for s in scores(h))
            p_scr[h % 2, 0:band, :] = pb.astype(BF16)
            p_scr[h % 2, band:nk, :] = pc.astype(BF16)
            return jnp.sum(pb, axis=0, keepdims=True) + jnp.sum(pc, axis=0, keepdims=True)

        l = probs(0)
        for h in range(NA_HEADS):
            l_next = probs(h + 1) if h + 1 < NA_HEADS else None
            finish(h, l)
            l = l_next

    @pl.when(bounded_ref[0] == 0)
    def _():
        def stage(h):
            s_scr[h % 2, 0:band, :], s_scr[h % 2, band:nk, :] = scores(h)

        stage(0)
        for h in range(NA_HEADS):
            slot = h % 2
            if h + 1 < NA_HEADS:
                stage(h + 1)
            ls = []
            for st in range(tq // LANE):
                cs = slice(st * LANE, (st + 1) * LANE)
                m = jnp.max(s_scr[slot, :, cs], axis=0, keepdims=True)
                p = jnp.exp2(s_scr[slot, :, cs] - m)
                ls.append(jnp.sum(p, axis=0, keepdims=True))
                p_scr[slot, :, cs] = p.astype(BF16)
            finish(h, jnp.concatenate(ls, axis=1))


def _natten(bounded, qn, kn, vnt, bias, n_lat):
    B, P, S, _ = qn.shape
    rows = n_lat // GRID_W
    nblk = rows // NA_Q_ROWS
    tq = NA_Q_ROWS * GRID_W
    band = NA_BAND_ROWS * GRID_W
    nk = band + S - n_lat
    btype = lambda i: jnp.where(i == 0, 0, jnp.where(i == nblk - 1, 2, 1))
    return pl.pallas_call(
        functools.partial(_natten_kernel, n_lat=n_lat, n_ctx=S - n_lat, rows=rows),
        out_shape=jax.ShapeDtypeStruct((B, NA_HEADS * NA_DH, n_lat), BF16),
        grid=(B, nblk),
        in_specs=[pl.BlockSpec(memory_space=pltpu.SMEM),
                  pl.BlockSpec((1, P, tq, LANE), lambda b, i: (b, 0, i, 0)),
                  pl.BlockSpec((1, P, S, LANE), lambda b, i: (b, 0, 0, 0)),
                  pl.BlockSpec((1, NA_HEADS * NA_DH, S), lambda b, i: (b, 0, 0)),
                  pl.BlockSpec((NA_HEADS, 1, band, tq), lambda b, i: (0, btype(i), 0, 0))],
        out_specs=pl.BlockSpec((1, NA_HEADS * NA_DH, tq), lambda b, i: (b, 0, i)),
        scratch_shapes=[pltpu.VMEM((2, nk, tq), F32), pltpu.VMEM((2, nk, tq), BF16)],
        compiler_params=_params(2), name="natten",
    )(bounded, qn, kn, vnt, bias)


def _na_bias_kernel(rpb_ref, o_ref, bc_ref, *, rows):
    l, h = pl.program_id(0), pl.program_id(1)
    tq = NA_Q_ROWS * GRID_W
    kc = lax.broadcasted_iota(jnp.int32, (GRID_W, tq), 0)
    ql = lax.broadcasted_iota(jnp.int32, (GRID_W, tq), 1)
    c = ql & (GRID_W - 1)
    qi = ql >> 6
    dc = kc - c + (NA_WIN_C - 1)
    cs = jnp.clip(c - NA_WIN_C // 2, 0, GRID_W - NA_WIN_C)
    col_ok = (kc >= cs) & (kc < cs + NA_WIN_C)
    n_dr, n_dc = 2 * NA_WIN_R - 1, 2 * NA_WIN_C - 1
    base = (l * NA_HEADS + h) * (n_dr * n_dc)
    for dr in range(n_dr):
        acc = jnp.zeros((GRID_W, tq), F32)
        for j in range(n_dc):
            acc = jnp.where(dc == j, rpb_ref[base + dr * n_dc + j], acc)
        bc_ref[dr] = jnp.where(col_ok, acc * LOG2E, NEG_INF)
    nblk = rows // NA_Q_ROWS
    for t, blk in enumerate((0, 1, nblk - 1)):
        r0 = blk * NA_Q_ROWS
        u0 = min(max(r0 - NA_WIN_R // 2, 0), rows - NA_BAND_ROWS)
        for j in range(NA_BAND_ROWS):
            tile = jnp.full((GRID_W, tq), NEG_INF, F32)
            for i in range(NA_Q_ROWS):
                r, kr = r0 + i, u0 + j
                rs = min(max(r - NA_WIN_R // 2, 0), rows - NA_WIN_R)
                if rs <= kr < rs + NA_WIN_R:
                    tile = jnp.where(qi == i, bc_ref[kr - r + NA_WIN_R - 1], tile)
            o_ref[0, 0, t, j * GRID_W:(j + 1) * GRID_W, :] = tile


def _na_bias(rpb, rows):
    L = rpb.shape[0]
    tq = NA_Q_ROWS * GRID_W
    band = NA_BAND_ROWS * GRID_W
    return pl.pallas_call(
        functools.partial(_na_bias_kernel, rows=rows),
        out_shape=jax.ShapeDtypeStruct((L, NA_HEADS, 3, band, tq), F32),
        grid=(L, NA_HEADS),
        in_specs=[pl.BlockSpec(memory_space=pltpu.SMEM)],
        out_specs=pl.BlockSpec((1, 1, 3, band, tq), lambda l, h: (l, h, 0, 0, 0)),
        scratch_shapes=[pltpu.VMEM((2 * NA_WIN_R - 1, GRID_W, tq), F32)],
        compiler_params=_params(2), name="na_bias",
    )(rpb.reshape(-1))


def _ctx_attn_kernel(qm_ref, km_ref, vmt_ref, qn_ref, kn_ref, vnt_ref, mo_ref, no_ref):
    lane = lax.broadcasted_iota(jnp.int32, (1, LANE), 1)
    for h in range(MLA_HEADS):
        hs = slice(h * MLA_V, (h + 1) * MLA_V)
        mo_ref[0, hs, :] = _softmax_pv_t([(km_ref[0, h], qm_ref[0, h], vmt_ref[0, hs, :], None)]).astype(BF16)
        qh = qn_ref[0, h // 2] * ((lane < NA_DH) if h % 2 == 0 else (lane >= NA_DH)).astype(F32).astype(BF16)
        no_ref[0, hs, :] = _softmax_pv_t([(kn_ref[0, h // 2], qh, vnt_ref[0, hs, :], None)]).astype(BF16)


def _ctx_attn(qm, km, vmt, qn, kn, vnt, n_lat):
    B, H, S, _ = qm.shape
    n_ctx = S - n_lat
    t = n_lat // n_ctx
    head = lambda b: (b, 0, t, 0)
    rows_t = lambda b: (b, 0, t)
    out = lambda b: (b, 0, 0)
    return pl.pallas_call(
        _ctx_attn_kernel,
        out_shape=[jax.ShapeDtypeStruct((B, H * MLA_V, n_ctx), BF16),
                   jax.ShapeDtypeStruct((B, NA_HEADS * NA_DH, n_ctx), BF16)],
        grid=(B,),
        in_specs=[pl.BlockSpec((1, H, n_ctx, LANE), head), pl.BlockSpec((1, H, n_ctx, LANE), head),
                  pl.BlockSpec((1, H * MLA_V, n_ctx), rows_t),
                  pl.BlockSpec((1, NA_HEADS // 2, n_ctx, LANE), head), pl.BlockSpec((1, NA_HEADS // 2, n_ctx, LANE), head),
                  pl.BlockSpec((1, NA_HEADS * NA_DH, n_ctx), rows_t)],
        out_specs=[pl.BlockSpec((1, H * MLA_V, n_ctx), out), pl.BlockSpec((1, NA_HEADS * NA_DH, n_ctx), out)],
        compiler_params=_params(1), name="ctx_attn",
    )(qm, km, vmt, qn, kn, vnt)


def _block_out_kernel(x_ref, xc_ref, mod_ref, g1_ref, g2_ref, hf_ref, hb_ref, o_ref, mo_ref, no_ref, moc_ref, noc_ref,
                      wg_ref, gout_ref, wml_ref, wmla_ref, wna_ref, wout_ref, w1_ref, w2_ref, out_ref, *, n_lat_tiles):
    is_ctx = pl.program_id(1) >= n_lat_tiles
    x = jnp.where(is_ctx, xc_ref[0], x_ref[0])
    m = mod_ref[0]
    hb = (_rms(x, g1_ref[...], D_MODEL) * (1.0 + m[1:2]) + m[0:1]).astype(BF16)
    gates = jax.nn.sigmoid(_dot(hb, wg_ref[...]))
    hs = hf_ref[0] + hb_ref[0]
    og = jax.nn.sigmoid(o_ref[0])
    gout = gout_ref[...]
    hn = jnp.concatenate(
        [_rms(hs[:, h * ML_DV:(h + 1) * ML_DV], gout[:, h * ML_DV:(h + 1) * ML_DV], ML_DV) for h in range(ML_HEADS)],
        axis=1)
    y_a = _dot((hn * og).astype(BF16), wml_ref[...])
    y_b = _dot_tn(jnp.where(is_ctx, moc_ref[0], mo_ref[0]), wmla_ref[...])
    y_c = _dot_tn(jnp.where(is_ctx, noc_ref[0], no_ref[0]), wna_ref[...])
    mg = gates[:, 0:D_MODEL] * y_a + gates[:, D_MODEL:2 * D_MODEL] * y_b + gates[:, 2 * D_MODEL:] * y_c
    x1 = x + m[2:3] * _dot(mg.astype(BF16), wout_ref[...])

    h2 = (_rms(x1, g2_ref[...], D_MODEL) * (1.0 + m[4:5]) + m[3:4]).astype(BF16)
    acc = jnp.zeros((TM, D_MODEL), F32)
    fc = 1024
    for c in range(D_FF // fc):
        u = jnp.maximum(_dot(h2, w1_ref[:, c * fc:(c + 1) * fc]), 0.0)
        acc = acc + _dot((u * u).astype(BF16), w2_ref[c * fc:(c + 1) * fc, :])
    out_ref[0] = x1 + m[5:6] * acc


def _block_out(x, xc, ctx_block, mods, g1, g2, hf, hb, o, mla_o, na_o, mla_oc, na_oc, w, nt, n_lat_tiles):
    B = x.shape[0]
    tok = lambda b, t: (b, t, 0)
    lat_t = lambda b, t: (b, 0, jnp.minimum(t, n_lat_tiles - 1))
    ctx_t = lambda b, t: (b, 0, 0)
    in_specs = _stream_specs(ctx_block, n_lat_tiles)
    in_specs += [pl.BlockSpec((1, 8, D_MODEL), lambda b, t: (jnp.where(t >= n_lat_tiles, B, b), 0, 0)),
                 _const_spec(g1.shape), _const_spec(g2.shape),
                 pl.BlockSpec((1, TM, 512), tok), pl.BlockSpec((1, TM, 512), tok), pl.BlockSpec((1, TM, 512), tok),
                 pl.BlockSpec((1, 512, TM), lat_t), pl.BlockSpec((1, 512, TM), lat_t),
                 pl.BlockSpec((1, 512, TM), ctx_t), pl.BlockSpec((1, 512, TM), ctx_t)]
    names = ("wg", "gout", "wml", "wmla", "wna", "wout", "w1", "w2")
    in_specs += [_const_spec(w[k].shape) for k in names]
    return pl.pallas_call(
        functools.partial(_block_out_kernel, n_lat_tiles=n_lat_tiles),
        out_shape=jax.ShapeDtypeStruct((B, nt * TM, D_MODEL), F32),
        grid=(B, nt), in_specs=in_specs, out_specs=pl.BlockSpec((1, TM, D_MODEL), tok),
        compiler_params=_params(2), name="block_out",
    )(x, xc, mods, g1, g2, hf, hb, o, mla_o, na_o, mla_oc, na_oc, *[w[k] for k in names])


def _rope_tables(n_lat, n_ctx):
    nf = MLA_ROPE // 4
    inv = jnp.power(ROPE_THETA, -jnp.arange(nf, dtype=F32) / nf)
    pos = jnp.arange(n_lat)
    ang_r = (pos // GRID_W).astype(F32)[:, None] * inv
    ang_c = (pos % GRID_W).astype(F32)[:, None] * inv
    one = lambda n: jnp.ones((n_lat, n), F32)
    zero = lambda n: jnp.zeros((n_lat, n), F32)
    cr, sr, cc, sc = jnp.cos(ang_r), jnp.sin(ang_r), jnp.cos(ang_c), jnp.sin(ang_c)
    cos = jnp.concatenate([one(MLA_NOPE), cr, cr, cc, cc, one(LANE - MLA_DQK)], axis=1)
    sin = jnp.concatenate([zero(MLA_NOPE), -sr, sr, -sc, sc, zero(LANE - MLA_DQK)], axis=1)
    ident = (jnp.ones((n_ctx, LANE), F32), jnp.zeros((n_ctx, LANE), F32))
    return tuple(jnp.concatenate([t, i], axis=0) for t, i in zip((cos, sin), ident))


def _rope_partner(a):
    half = MLA_ROPE // 4
    lane = np.arange(LANE)
    r = lane - MLA_NOPE
    is_rope = (r >= 0) & (r < MLA_ROPE)
    src = np.where((r // half) % 2 == 0, lane + half, lane - half)
    return jnp.where(is_rope, a[..., np.where(is_rope, src, lane)], 0)


def _layer_weights(l, w_in, ml_i_bias, ml_f_bias, ml_g_out, ml_w_o, mla_g_cq, mla_w_uq, mla_g_ckv, mla_w_ukv,
                   mla_g_q, mla_g_k, mla_w_o, na_g_q, na_g_k, na_w_o, w_out, w_ff1, w_ff2):
    offs = np.cumsum(IN_SIZES)[:-1].tolist()
    mlq, mlk, mlv, mlo, mlg, dq, dkv, kr, naq, nak, nav, gates = jnp.split(w_in[l], offs, axis=-1)
    padc = lambda a, n: jnp.pad(a, ((0, 0), (0, n - a.shape[1])))
    kr_slab = jnp.pad(kr, ((0, 0), (MLA_NOPE, LANE - MLA_DQK)))
    w_uq = jnp.pad(mla_w_uq[l].reshape(MLA_Q_RANK, MLA_HEADS, MLA_DQK), ((0, 0), (0, 0), (0, LANE - MLA_DQK)))
    w_ukv = mla_w_ukv[l].reshape(MLA_KV_RANK, MLA_HEADS, MLA_NOPE + MLA_V)
    w_uk = jnp.pad(w_ukv[..., :MLA_NOPE], ((0, 0), (0, 0), (0, LANE - MLA_NOPE)))
    w_uv = w_ukv[..., MLA_NOPE:].reshape(MLA_KV_RANK, MLA_HEADS * MLA_V)
    row = lambda a: a.reshape(1, -1).astype(F32)
    bias16 = jnp.stack([ml_i_bias[l, 0], ml_f_bias[l, 0], ml_i_bias[l, 1], ml_f_bias[l, 1]]).reshape(-1).astype(F32)
    return {
        "wa": jnp.concatenate([mlq * (ML_DQK ** -0.5), mlv, mlo, padc(mlg, LANE)], axis=1).astype(BF16),
        "wat": jnp.concatenate([mlk.T, mlg.T], axis=0).astype(BF16),
        "wb": jnp.concatenate([dq, dkv, kr_slab, _rope_partner(kr_slab)], axis=1).astype(BF16),
        "wc": jnp.concatenate([naq, nak], axis=1).astype(BF16),
        "wct": nav.T.astype(BF16),
        "wuq": w_uq.reshape(MLA_Q_RANK, MLA_HEADS * LANE).astype(BF16),
        "wuqs": _rope_partner(w_uq).reshape(MLA_Q_RANK, MLA_HEADS * LANE).astype(BF16),
        "wuk": w_uk.reshape(MLA_KV_RANK, MLA_HEADS * LANE).astype(BF16),
        "wuvt": w_uv.T.astype(BF16),
        "gcq": row(mla_g_cq[l]), "gckv": row(mla_g_ckv[l]),
        "gq": padc(row(mla_g_q[l]) * (MLA_DQK ** -0.5 * LOG2E), LANE), "gk": padc(row(mla_g_k[l]), LANE),
        "ngq": row(jnp.tile(na_g_q[l], 2)) * (NA_DH ** -0.5 * LOG2E), "ngk": row(jnp.tile(na_g_k[l], 2)),
        "ng": jnp.concatenate([row(jnp.tile(na_g_q[l], NA_HEADS)) * (NA_DH ** -0.5 * LOG2E),
                               row(jnp.tile(na_g_k[l], NA_HEADS))], axis=1),
        "brow": padc(bias16.reshape(1, 16), LANE), "bcol": bias16.reshape(16, 1),
        "wg": gates.astype(BF16), "gout": row(ml_g_out[l]),
        "wml": ml_w_o[l].astype(BF16), "wmla": mla_w_o[l].astype(BF16), "wna": na_w_o[l].astype(BF16),
        "wout": w_out[l].astype(BF16), "w1": w_ff1[l].astype(BF16), "w2": w_ff2[l].astype(BF16),
    }


def kernel(x, c, ctx, c_ctx, w_mod, b_mod, g_norm1, g_norm2, w_in, ml_i_bias, ml_f_bias, ml_g_out, ml_w_o, mla_g_cq, mla_w_uq, mla_g_ckv, mla_w_ukv, mla_g_q, mla_g_k, mla_w_o, na_g_q, na_g_k, na_rpb, na_w_o, w_out, w_ff1, w_ff2):
    B, T, D = x.shape
    C = ctx.shape[1]
    depth = w_in.shape[0]
    assert D == D_MODEL and C == TM and T % MLA_TQ == 0 and B < 16
    rows = T // GRID_W
    assert rows % NA_Q_ROWS == 0 and rows >= NA_BAND_ROWS
    n_lat_tiles = T // TM

    cc = jnp.zeros((16, D), F32).at[:B].set(c).at[B].set(c_ctx)
    mod = _modulation(cc, w_mod, b_mod).reshape(depth, 16, 6, D)
    mod = jnp.pad(mod, ((0, 0), (0, 0), (0, 2), (0, 0)))
    tabs = _rope_tables(T, C)
    na_bias = _na_bias(na_rpb.astype(F32), rows)
    stream = (x, ctx, 0)

    for l in range(depth):
        last = l == depth - 1
        w = _layer_weights(l, w_in, ml_i_bias, ml_f_bias, ml_g_out, ml_w_o, mla_g_cq, mla_w_uq, mla_g_ckv,
                           mla_w_ukv, mla_g_q, mla_g_k, mla_w_o, na_g_q, na_g_k, na_w_o, w_out, w_ff1, w_ff2)
        g1 = g_norm1[l].reshape(1, D).astype(F32)
        g2 = g_norm2[l].reshape(1, D).astype(F32)
        cos, sin = tabs
        gain_tabs = (cos * w["gq"], sin * _rope_partner(w["gq"]), cos * w["gk"], sin * _rope_partner(w["gk"]))
        q, kt, v, o, gcol, grow, qm, km, vmt, qn, kn, vnt = _inproj(*stream, mod[l], g1, w, gain_tabs, n_lat_tiles)
        hf, hb = _mlstm(q, kt, v, gcol, grow, w["brow"], w["bcol"], n_lat_tiles)
        mla_bound = MLA_DQK * jnp.max(jnp.abs(w["gq"])) * jnp.max(jnp.abs(w["gk"]))
        na_bound = (NA_DH * jnp.max(jnp.abs(w["ngq"])) * jnp.max(jnp.abs(w["ngk"]))
                    + LOG2E * jnp.max(jnp.abs(na_rpb[l].astype(F32))))
        mla_o = _mla((mla_bound <= SCORE_BOUND_LIMIT).astype(jnp.int32).reshape(1), qm, km, vmt, T)
        na_o = _natten((na_bound <= SCORE_BOUND_LIMIT).astype(jnp.int32).reshape(1), qn, kn, vnt, na_bias[l], T)
        mla_oc, na_oc = (mla_o, na_o) if last else _ctx_attn(qm, km, vmt, qn, kn, vnt, T)
        nt = n_lat_tiles if last else n_lat_tiles + 1
        xs = _block_out(*stream, mod[l], g1, g2, hf, hb, o, mla_o, na_o, mla_oc, na_oc, w, nt, n_lat_tiles)
        stream = (xs, xs, n_lat_tiles)
    return xs
```
